```python
import math
import jax, jax.numpy as jnp
from jax import lax
import numpy as np

D_MODEL = 1024
BATCH = 8
SEQ = 2048
DEPTH = 1

D_MIX = D_MODEL
D_S5 = D_MIX // 2
D_RWKV = D_MIX - D_S5
S5_GROUP = 16
S5_GROUPS = D_S5 // S5_GROUP
S5_STATE = 64
RWKV_HEAD = 64
RWKV_HEADS = D_RWKV // RWKV_HEAD
D_DECAY_LORA = 64
D_AAA_LORA = 64
D_GATE_LORA = 160
D_RWKV_IN = 3 * D_RWKV + D_DECAY_LORA + D_AAA_LORA + D_GATE_LORA
D_IN = D_S5 + D_RWKV_IN
D_FF = 4 * D_MODEL
NORM_EPS = 1e-6
GN_EPS = 64e-5

kernel_name = "hymba_s5_rwkv7_hybrid_block"

F32 = jnp.float32


def rms_norm(x, g):
    xf = x.astype(F32)
    return xf * lax.rsqrt(jnp.mean(xf * xf, axis=-1, keepdims=True) + NORM_EPS) * g.astype(F32)


def s5_mixer(u, lam_re, lam_im, log_dt, b_re, b_im, c_re, c_im, d_skip, w_glu, b_glu):
    bsz, seqlen, _ = u.shape
    ug = u.reshape(bsz, seqlen, S5_GROUPS, S5_GROUP)
    lam = lax.complex(lam_re.astype(F32), lam_im.astype(F32))
    dt = jnp.exp(log_dt.astype(F32))[:, None]
    lam_bar = jnp.exp(lam * dt)
    b_bar = ((lam_bar - 1.0) / lam)[:, :, None] * lax.complex(b_re.astype(F32), b_im.astype(F32))
    bu = jnp.einsum('blgh,gph->blgp', ug.astype(jnp.complex64), b_bar)
    a_elems = jnp.broadcast_to(lam_bar, (1, seqlen) + lam_bar.shape)

    def combine(left, right):
        a1, b1 = left
        a2, b2 = right
        return a2 * a1, a2 * b1 + b2

    _, states = lax.associative_scan(combine, (a_elems, bu), axis=1)
    y = (jnp.einsum('ghp,blgp->blgh', c_re.astype(F32), states.real)
         - jnp.einsum('ghp,blgp->blgh', c_im.astype(F32), states.imag))
    y = y.reshape(bsz, seqlen, D_S5) + d_skip.astype(F32) * u
    g = jax.nn.gelu(y, approximate=False)
    return g * jax.nn.sigmoid(g @ w_glu.astype(F32) + b_glu.astype(F32))


def token_shift(z, mu):
    z_prev = jnp.pad(z, ((0, 0), (1, 0), (0, 0)))[:, :-1]
    return z + (z_prev - z) * mu.astype(F32)


def rwkv7_mixer(z, mu_shift, w0, w2, a0, a2, g2, k_k, k_a, r_k, ln_x_w, ln_x_b):
    bsz, seqlen, _ = z.shape
    z = token_shift(z, mu_shift)
    r, k, v, zw, za, zg = jnp.split(
        z, [D_RWKV, 2 * D_RWKV, 3 * D_RWKV, 3 * D_RWKV + D_DECAY_LORA,
            3 * D_RWKV + D_DECAY_LORA + D_AAA_LORA], axis=-1)
    w = -jax.nn.softplus(-(w0.astype(F32) + jnp.tanh(zw) @ w2.astype(F32))) - 0.5
    decay = jnp.exp(-jnp.exp(w))
    a = jax.nn.sigmoid(a0.astype(F32) + za @ a2.astype(F32))
    g = jax.nn.sigmoid(zg) @ g2.astype(F32)

    def heads(t):
        return t.reshape(bsz, seqlen, RWKV_HEADS, RWKV_HEAD)

    kk = heads(k * k_k.astype(F32))
    kk = kk / jnp.maximum(jnp.sqrt(jnp.sum(kk * kk, axis=-1, keepdims=True)), 1e-12)
    k = k * (1.0 + (a - 1.0) * k_a.astype(F32))
    r_h, k_h, v_h, a_h, d_h = heads(r), heads(k), heads(v), heads(a), heads(decay)

    def step(state, inp):
        r_t, d_t, k_t, v_t, kk_t, a_t = inp
        sa = jnp.einsum('bhvk,bhk->bhv', state, -kk_t)
        state = (state * d_t[:, :, None, :]
                 + sa[..., None] * (kk_t * a_t)[:, :, None, :]
                 + v_t[..., None] * k_t[:, :, None, :])
        return state, jnp.einsum('bhvk,bhk->bhv', state, r_t)

    xs = tuple(jnp.moveaxis(t, 1, 0) for t in (r_h, d_h, k_h, v_h, kk, a_h))
    s0 = jnp.zeros((bsz, RWKV_HEADS, RWKV_HEAD, RWKV_HEAD), F32)
    _, ys = lax.scan(step, s0, xs)
    y = jnp.moveaxis(ys, 0, 1)
    mean = jnp.mean(y, axis=-1, keepdims=True)
    var = jnp.mean(jnp.square(y - mean), axis=-1, keepdims=True)
    y = ((y - mean) * lax.rsqrt(var + GN_EPS)).reshape(bsz, seqlen, D_RWKV)
    y = y * ln_x_w.astype(F32) + ln_x_b.astype(F32)
    bonus = jnp.sum(r_h * k_h * r_k.astype(F32), axis=-1, keepdims=True) * v_h
    return (y + bonus.reshape(bsz, seqlen, D_RWKV)) * g


def setup_inputs(seed: int = 0) -> dict:
    key = jax.random.key(seed)
    ks = jax.random.split(key, 32)
    L = DEPTH

    def nrm(k, shape, scale):
        return jax.random.normal(k, shape, F32) * scale

    def gain(k, shape):
        return 1.0 + 0.02 * jax.random.normal(k, shape, F32)

    n = jnp.arange(D_RWKV, dtype=F32)
    decay_speed = -7.0 + 5.0 * (n / (D_RWKV - 1)) ** 0.85
    return {
        "x": jax.random.normal(ks[0], (BATCH, SEQ, D_MODEL), F32),
        "norm_mix_g": gain(ks[1], (L, D_MODEL)),
        "w_in": nrm(ks[2], (L, D_MODEL, D_IN), D_MODEL ** -0.5),
        "lam_re": -0.5 * jnp.exp(0.02 * jax.random.normal(ks[3], (L, S5_GROUPS, S5_STATE), F32)),
        "lam_im": math.pi * jnp.arange(S5_STATE, dtype=F32) + 0.02 * jax.random.normal(ks[4], (L, S5_GROUPS, S5_STATE), F32),
        "log_dt": jax.random.uniform(ks[5], (L, S5_GROUPS), F32, math.log(1e-3), math.log(1e-1)),
        "b_re": nrm(ks[6], (L, S5_GROUPS, S5_STATE, S5_GROUP), (2 * S5_GROUP) ** -0.5),
        "b_im": nrm(ks[7], (L, S5_GROUPS, S5_STATE, S5_GROUP), (2 * S5_GROUP) ** -0.5),
        "c_re": nrm(ks[8], (L, S5_GROUPS, S5_GROUP, S5_STATE), S5_STATE ** -0.5),
        "c_im": nrm(ks[9], (L, S5_GROUPS, S5_GROUP, S5_STATE), S5_STATE ** -0.5),
        "d_skip": nrm(ks[10], (L, D_S5), 1.0),
        "w_glu": nrm(ks[11], (L, D_S5, D_S5), D_S5 ** -0.5),
        "b_glu": nrm(ks[12], (L, D_S5), 0.01),
        "s5_out_g": gain(ks[13], (L, D_S5)),
        "mu_shift": jax.random.uniform(ks[14], (L, D_RWKV_IN), F32),
        "w0": decay_speed + 0.5 + 0.01 * jax.random.normal(ks[15], (L, D_RWKV), F32),
        "w2": nrm(ks[16], (L, D_DECAY_LORA, D_RWKV), 0.1 * D_DECAY_LORA ** -0.5),
        "a0": nrm(ks[17], (L, D_RWKV), 0.01),
        "a2": nrm(ks[18], (L, D_AAA_LORA, D_RWKV), 0.1 * D_AAA_LORA ** -0.5),
        "g2": nrm(ks[19], (L, D_GATE_LORA, D_RWKV), D_GATE_LORA ** -0.5),
        "k_k": 0.85 + 0.02 * jax.random.normal(ks[20], (L, D_RWKV), F32),
        "k_a": gain(ks[21], (L, D_RWKV)),
        "r_k": nrm(ks[22], (L, RWKV_HEADS, RWKV_HEAD), 0.1),
        "ln_x_w": gain(ks[23], (L, D_RWKV)),
        "ln_x_b": nrm(ks[24], (L, D_RWKV), 0.01),
        "w_out": nrm(ks[25], (L, D_MIX, D_MODEL), D_MIX ** -0.5),
        "norm_ffn_g": gain(ks[26], (L, D_MODEL)),
        "w_up": nrm(ks[27], (L, D_MODEL, D_FF), D_MODEL ** -0.5),
        "w_down": nrm(ks[28], (L, D_FF, D_MODEL), D_FF ** -0.5),
        "norm_final_g": gain(ks[29], (D_MODEL,)),
    }


def reference(x, norm_mix_g, w_in, lam_re, lam_im, log_dt, b_re, b_im, c_re, c_im,
              d_skip, w_glu, b_glu, s5_out_g, mu_shift, w0, w2, a0, a2, g2, k_k, k_a,
              r_k, ln_x_w, ln_x_b, w_out, norm_ffn_g, w_up, w_down, norm_final_g):
    h = x.astype(F32)
    for i in range(DEPTH):
        xn = rms_norm(h, norm_mix_g[i])
        z = xn @ w_in[i].astype(F32)
        y_s5 = s5_mixer(z[..., :D_S5], lam_re[i], lam_im[i], log_dt[i], b_re[i], b_im[i],
                        c_re[i], c_im[i], d_skip[i], w_glu[i], b_glu[i])
        y_s5 = rms_norm(y_s5, s5_out_g[i])
        y_rwkv = rwkv7_mixer(z[..., D_S5:], mu_shift[i], w0[i], w2[i], a0[i], a2[i], g2[i],
                             k_k[i], k_a[i], r_k[i], ln_x_w[i], ln_x_b[i])
        h = h + jnp.concatenate([y_s5, y_rwkv], axis=-1) @ w_out[i].astype(F32)
        hn = rms_norm(h, norm_ffn_g[i])
        h = h + jnp.square(jax.nn.relu(hn @ w_up[i].astype(F32))) @ w_down[i].astype(F32)
    return rms_norm(h, norm_final_g).astype(x.dtype)
```

```python
import functools
import math

import jax
import jax.numpy as jnp
from jax import lax
from jax.experimental import pallas as pl
from jax.experimental.pallas import tpu as pltpu

F32 = jnp.float32
BF16 = jnp.bfloat16

NORM_EPS = 1e-6
GN_EPS = 64e-5

S5_GROUP = 16
S5_STATE = 64
RWKV_HEAD = 64
LANES = 128
SUBLANES = 8
VMEM_LIMIT = 56 * 1024 * 1024

IN_TM = 512
S5_TC = 128
S5_STRIP = 512
RWKV_LB = 256
RWKV_C = 64
FFN_TM = 256
FFN_CHUNK = 1024


def _dot(a, b, dims=(((1,), (0,)), ((), ()))):
    return lax.dot_general(a.astype(BF16), b.astype(BF16), dims,
                           preferred_element_type=F32)


def _split2(a):
    hi = a.astype(BF16)
    lo = (a - hi.astype(F32)).astype(BF16)
    return hi, lo


def _dot3(a, b, dims=(((1,), (0,)), ((), ()))):
    ah, al = _split2(a)
    bh, bl = _split2(b)
    d = functools.partial(lax.dot_general, dimension_numbers=dims,
                          preferred_element_type=F32)
    return d(ah, bh) + (d(ah, bl) + d(al, bh))


def _dot_exact_lhs(a_bf16, b, passes):
    acc = None
    rem = b
    for _ in range(passes):
        part = rem.astype(BF16)
        term = jnp.dot(a_bf16, part, preferred_element_type=F32)
        acc = term if acc is None else acc + term
        rem = rem - part.astype(F32)
    return acc


NT = (((1,), (1,)), ((), ()))
TN = (((0,), (0,)), ((), ()))


def _in_proj_kernel(x_ref, g_ref, w_ref, u_ref, z_ref, *, d_s5):
    x = x_ref[0]
    ms = jnp.mean(x * x, axis=-1, keepdims=True)
    xn = (x * lax.rsqrt(ms + NORM_EPS) * g_ref[...]).astype(BF16)
    u_ref[...] = jnp.dot(xn, w_ref[:, :d_s5], preferred_element_type=F32)
    z_ref[0] = jnp.dot(xn, w_ref[:, d_s5:], preferred_element_type=F32)


def _in_proj(x, g, w, d_s5):
    bsz, seqlen, d_model = x.shape
    d_rw = w.shape[1] - d_s5
    tm = IN_TM
    return pl.pallas_call(
        functools.partial(_in_proj_kernel, d_s5=d_s5),
        grid=(bsz, seqlen // tm),
        in_specs=[
            pl.BlockSpec((1, tm, d_model), lambda b, i: (b, i, 0)),
            pl.BlockSpec((1, d_model), lambda b, i: (0, 0)),
            pl.BlockSpec(w.shape, lambda b, i: (0, 0)),
        ],
        out_specs=[
            pl.BlockSpec((tm, d_s5), lambda b, i: (i, b)),
            pl.BlockSpec((1, tm, d_rw), lambda b, i: (b, i, 0)),
        ],
        out_shape=[
            jax.ShapeDtypeStruct((seqlen, bsz * d_s5), F32),
            jax.ShapeDtypeStruct((bsz, seqlen, d_rw), F32),
        ],
        compiler_params=pltpu.CompilerParams(
            dimension_semantics=("arbitrary", "arbitrary"),
            vmem_limit_bytes=VMEM_LIMIT),
        name="in_proj",
    )(x, g, w)


def _s5_kernel(u_ref, wb_ref, are_ref, aim_ref, wcre_ref, wcim_ref, dskip_ref,
               wglu_ref, bglu_ref, gout_ref, o_ref, st_ref, carry_ref, *, bsz, tc):
    n_state = are_ref.shape[1]

    @pl.when(pl.program_id(0) == 0)
    def _():
        carry_ref[...] = jnp.zeros_like(carry_ref)

    u_bf = u_ref[...].astype(BF16)
    n_tiles = wb_ref.shape[0]
    tile_w = wb_ref.shape[2]
    tiles_per_half = n_tiles // 2
    for j in range(n_tiles):
        kb = (j % tiles_per_half) * tile_w // (S5_STATE // S5_GROUP * LANES)
        st_ref[:, j * tile_w:(j + 1) * tile_w] = jnp.dot(
            u_bf[:, kb * LANES:(kb + 1) * LANES], wb_ref[j],
            preferred_element_type=F32)

    for s in range(n_state // S5_STRIP):
        re_sl = slice(s * S5_STRIP, (s + 1) * S5_STRIP)
        im_sl = slice(n_state + s * S5_STRIP, n_state + (s + 1) * S5_STRIP)
        a_re = jnp.broadcast_to(are_ref[:, re_sl], (bsz, S5_STRIP))
        a_im = jnp.broadcast_to(aim_ref[:, re_sl], (bsz, S5_STRIP))

        def step(t, carry, re_sl=re_sl, im_sl=im_sl, a_re=a_re, a_im=a_im):
            s_re, s_im = carry
            rows = pl.ds(pl.multiple_of(t * bsz, bsz), bsz)
            n_re = a_re * s_re - a_im * s_im + st_ref[rows, re_sl]
            n_im = a_re * s_im + a_im * s_re + st_ref[rows, im_sl]
            st_ref[rows, re_sl] = n_re
            st_ref[rows, im_sl] = n_im
            return n_re, n_im

        s_re, s_im = lax.fori_loop(
            0, tc, step, (carry_ref[:, re_sl], carry_ref[:, im_sl]), unroll=4)
        carry_ref[:, re_sl] = s_re
        carry_ref[:, im_sl] = s_im

    n_out_tiles = wcre_ref.shape[0]
    kw = wcre_ref.shape[1]
    ys = []
    for m in range(n_out_tiles):
        x_re = st_ref[:, m * kw:(m + 1) * kw].astype(BF16)
        x_im = st_ref[:, n_state + m * kw:n_state + (m + 1) * kw].astype(BF16)
        ys.append(jnp.dot(x_re, wcre_ref[m], preferred_element_type=F32)
                  + jnp.dot(x_im, wcim_ref[m], preferred_element_type=F32))
    y = jnp.concatenate(ys, axis=-1) + dskip_ref[...] * u_ref[...]
    g = 0.5 * y * (1.0 + lax.erf(y * (1.0 / math.sqrt(2.0))))
    gate = jax.nn.sigmoid(
        jnp.dot(g.astype(BF16), wglu_ref[...], preferred_element_type=F32) + bglu_ref[...])
    out = g * gate
    ms = jnp.mean(out * out, axis=-1, keepdims=True)
    o_ref[...] = out * lax.rsqrt(ms + NORM_EPS) * gout_ref[...]


def _s5(u_tm, wb, a_re, a_im, wc_re, wc_im, d_skip, w_glu, b_glu, g_out, bsz):
    rows, d_s5 = u_tm.shape
    seqlen = rows // bsz
    tc = S5_TC
    n_state = a_re.shape[1]
    const = lambda shape: pl.BlockSpec(shape, lambda i: (0,) * len(shape))
    return pl.pallas_call(
        functools.partial(_s5_kernel, bsz=bsz, tc=tc),
        grid=(seqlen // tc,),
        in_specs=[
            pl.BlockSpec((tc * bsz, d_s5), lambda i: (i, 0)),
            const(wb.shape), const(a_re.shape), const(a_im.shape),
            const(wc_re.shape), const(wc_im.shape), const(d_skip.shape),
            const(w_glu.shape), const(b_glu.shape), const(g_out.shape),
        ],
        out_specs=pl.BlockSpec((tc * bsz, d_s5), lambda i: (i, 0)),
        out_shape=jax.ShapeDtypeStruct((rows, d_s5), F32),
        scratch_shapes=[
            pltpu.VMEM((tc * bsz, 2 * n_state), F32),
            pltpu.VMEM((bsz, 2 * n_state), F32),
        ],
        compiler_params=pltpu.CompilerParams(
            dimension_semantics=("arbitrary",),
            vmem_limit_bytes=VMEM_LIMIT),
        name="s5",
    )(u_tm, wb, a_re, a_im, wc_re, wc_im, d_skip, w_glu, b_glu, g_out)


def _seg_sum(x, ones_bd):
    w = ones_bd.shape[0]
    outs = [_dot_exact_lhs_t(x[:, i * w:(i + 1) * w], ones_bd)
            for i in range(x.shape[1] // w)]
    return jnp.concatenate(outs, axis=-1)


def _dot_exact_lhs_t(x, ones_bd, passes=2):
    acc = None
    rem = x
    for _ in range(passes):
        part = rem.astype(BF16)
        term = jnp.dot(part, ones_bd, preferred_element_type=F32)
        acc = term if acc is None else acc + term
        rem = rem - part.astype(F32)
    return acc


def _rwkv_kernel(z_ref, mu_ref, w0_ref, w2_ref, a0_ref, a2_ref, g2_ref, kk_ref,
                 ka_ref, rk_ref, lnw_ref, lnb_ref, o_ref,
                 st_ref, carry_ref, r_s, k_s, v_s, kk_s, b_s, ld_s, y_s, *, d_rw, lora_w, lora_g):
    lb = z_ref.shape[1]
    c = RWKV_C
    n_pairs = d_rw // LANES
    hd = RWKV_HEAD

    @pl.when(pl.program_id(1) == 0)
    def _():
        st_ref[...] = jnp.zeros_like(st_ref)
        carry_ref[...] = jnp.zeros_like(carry_ref)

    z = z_ref[0]
    row = lax.broadcasted_iota(jnp.int32, z.shape, 0)
    z_prev = jnp.where(row == 0, carry_ref[0:1, :], pltpu.roll(z, 1, 0))
    carry_ref[0:1, :] = z[lb - 1:lb, :]
    zz = z + (z_prev - z) * mu_ref[...]

    o = 3 * d_rw
    r = zz[:, 0:d_rw]
    k = zz[:, d_rw:2 * d_rw]
    v = zz[:, 2 * d_rw:o]
    zw = zz[:, o:o + lora_w]
    za = zz[:, o + lora_w:o + 2 * lora_w]
    zg = zz[:, o + 2 * lora_w:o + 2 * lora_w + lora_g]

    wpre = w0_ref[...] + _dot(jnp.tanh(zw), w2_ref[...])
    w = -(jnp.maximum(-wpre, 0.0) + jnp.log1p(jnp.exp(-jnp.abs(wpre)))) - 0.5
    logd = -jnp.exp(w)
    a = jax.nn.sigmoid(a0_ref[...] + _dot(za, a2_ref[...]))
    gate = _dot(jax.nn.sigmoid(zg), g2_ref[...])

    seg_w = 2 * LANES
    li = lax.broadcasted_iota(jnp.int32, (seg_w, seg_w), 0) // hd
    lj = lax.broadcasted_iota(jnp.int32, (seg_w, seg_w), 1) // hd
    ones_bd = jnp.where(li == lj, 1.0, 0.0).astype(BF16)

    kk = k * kk_ref[...]
    kk = kk / jnp.maximum(jnp.sqrt(_seg_sum(kk * kk, ones_bd)), 1e-12)
    k2 = k * (1.0 + (a - 1.0) * ka_ref[...])
    bonus = _seg_sum(r * k2 * rk_ref[...], ones_bd) * v

    r_s[...] = r
    k_s[...] = k2
    v_s[...] = v
    kk_s[...] = kk
    b_s[...] = kk * a
    ld_s[...] = logd

    ci = lax.broadcasted_iota(jnp.int32, (c, c), 0)
    cj = lax.broadcasted_iota(jnp.int32, (c, c), 1)
    ltri = jnp.where(cj <= ci, 1.0, 0.0).astype(BF16)

    ri = lax.broadcasted_iota(jnp.int32, (c, LANES), 0)
    rl = lax.broadcasted_iota(jnp.int32, (c, LANES), 1)
    head0 = rl < hd
    incl = (rl % hd) <= ri
    strict = (rl % hd) < ri

    pi = lax.broadcasted_iota(jnp.int32, (2 * c, 2 * c), 0)
    pj = lax.broadcasted_iota(jnp.int32, (2 * c, 2 * c), 1)
    level = pi ^ pj
    eye = jnp.where(pi == pj, 1.0, 0.0)
    n_levels = int(math.log2(c))

    def stack_heads(x):
        return jnp.concatenate([jnp.where(head0, x, 0.0), jnp.where(head0, 0.0, x)], axis=0)

    def chunk(ic, _):
        rows = pl.ds(pl.multiple_of(ic * c, c), c)
        logd_c = ld_s[rows, :]
        cum = _dot_exact_lhs(ltri, logd_c, 3)
        e_inc = jnp.exp(cum)
        e_exc = jnp.exp(cum - logd_c)
        e_inv = jnp.exp(-cum)
        cum_end = cum[c - 1:c, :]
        e_end = jnp.exp(cum_end - cum)
        d_end = jnp.exp(cum_end)

        q_t = r_s[rows, :] * e_inc
        kk_t = kk_s[rows, :] * e_exc
        k_hat = k_s[rows, :] * e_inv
        b_hat = b_s[rows, :] * e_inv
        k_end = k_s[rows, :] * e_end
        b_end = b_s[rows, :] * e_end
        v_c = v_s[rows, :]

        for p in range(n_pairs):
            ls = slice(p * LANES, (p + 1) * LANES)
            lhs = jnp.concatenate([q_t[:, ls], kk_t[:, ls]], axis=0)
            rhs = jnp.concatenate([stack_heads(k_hat[:, ls]), stack_heads(b_hat[:, ls])], axis=0)
            p1 = _dot3(lhs, rhs, NT)
            a_rk = jnp.where(incl, p1[0:c, 0:LANES], 0.0)
            a_rb = jnp.where(incl, p1[0:c, LANES:2 * LANES], 0.0)
            a_kk = jnp.where(strict, p1[c:2 * c, 0:LANES], 0.0)
            a_kb = jnp.where(strict, p1[c:2 * c, LANES:2 * LANES], 0.0)

            n_bd = stack_heads(a_kb)
            t_inv = eye - jnp.where(level == 1, n_bd, 0.0)
            for lg in range(1, n_levels):
                n_off = jnp.where((level >> lg) == 1, n_bd, 0.0)
                t_inv = t_inv - _dot3(t_inv, _dot3(n_off, t_inv))

            v_bd = stack_heads(v_c[:, ls])
            kk_bd = stack_heads(kk_t[:, ls])
            av = _dot3(stack_heads(a_kk), v_bd)
            w12 = _dot3(t_inv, jnp.concatenate([kk_bd, av], axis=1))
            w1 = w12[:, 0:LANES]
            w2 = w12[:, LANES:2 * LANES]

            y_intra = _dot3(jnp.concatenate([a_rk, a_rb], axis=1),
                            jnp.concatenate([v_bd, -w2], axis=0))
            q_eff = q_t[:, ls] - _dot3(a_rb, w1)
            kb_end = jnp.concatenate([stack_heads(k_end[:, ls]), stack_heads(b_end[:, ls])], axis=0)
            g_mat = _dot3(kb_end, jnp.concatenate([v_bd, -w2], axis=0), TN)
            m_mat = jnp.where(pi == pj, jnp.broadcast_to(d_end[:, ls], (2 * c, LANES)), 0.0) \
                - _dot3(stack_heads(b_end[:, ls]), w1, TN)

            s_old = st_ref[p]
            y_s[rows, ls] = _dot3(q_eff, s_old) + y_intra
            st_ref[p] = _dot3(m_mat, s_old) + g_mat
        return 0

    lax.fori_loop(0, lb // c, chunk, 0)

    y = y_s[...]
    mean = _seg_sum(y, ones_bd) * (1.0 / hd)
    yc = y - mean
    var = _seg_sum(yc * yc, ones_bd) * (1.0 / hd)
    yn = yc * lax.rsqrt(var + GN_EPS) * lnw_ref[...] + lnb_ref[...]
    o_ref[0] = (yn + bonus) * gate


def _rwkv(z, mu, w0, w2p, a0, a2p, g2p, k_k, k_a, r_k, ln_w, ln_b, d_rw, lora_w, lora_g):
    bsz, seqlen, zw = z.shape
    lb = RWKV_LB
    n_pairs = d_rw // LANES
    const = lambda shape: pl.BlockSpec(shape, lambda b, i: (0,) * len(shape))
    params = (mu, w0, w2p, a0, a2p, g2p, k_k, k_a, r_k, ln_w, ln_b)
    return pl.pallas_call(
        functools.partial(_rwkv_kernel, d_rw=d_rw, lora_w=lora_w, lora_g=lora_g),
        grid=(bsz, seqlen // lb),
        in_specs=[pl.BlockSpec((1, lb, zw), lambda b, i: (b, i, 0))]
                 + [const(p.shape) for p in params],
        out_specs=pl.BlockSpec((1, lb, d_rw), lambda b, i: (b, i, 0)),
        out_shape=jax.ShapeDtypeStruct((bsz, seqlen, d_rw), F32),
        scratch_shapes=[
            pltpu.VMEM((n_pairs, LANES, LANES), F32),
            pltpu.VMEM((SUBLANES, zw), F32),
        ] + [pltpu.VMEM((lb, d_rw), F32)] * 7,
        compiler_params=pltpu.CompilerParams(
            dimension_semantics=("arbitrary", "arbitrary"),
            vmem_limit_bytes=VMEM_LIMIT),
        name="rwkv",
    )(z, *params)


def _out_ffn_kernel(x_ref, ys_ref, yr_ref, wo_ref, gf_ref, wu_ref, wd_ref, gl_ref, o_ref, *, final_norm):
    x = x_ref[0]
    mix = jnp.concatenate([ys_ref[...].astype(BF16), yr_ref[0].astype(BF16)], axis=1)
    h = x + jnp.dot(mix, wo_ref[...], preferred_element_type=F32)
    ms = jnp.mean(h * h, axis=-1, keepdims=True)
    hn = (h * lax.rsqrt(ms + NORM_EPS) * gf_ref[...]).astype(BF16)
    d_ff = wu_ref.shape[1]
    acc = h
    for j in range(d_ff // FFN_CHUNK):
        sl = slice(j * FFN_CHUNK, (j + 1) * FFN_CHUNK)
        up = jnp.dot(hn, wu_ref[:, sl], preferred_element_type=F32)
        up = jnp.square(jnp.maximum(up, 0.0)).astype(BF16)
        acc = acc + jnp.dot(up, wd_ref[sl, :], preferred_element_type=F32)
    if final_norm:
        ms2 = jnp.mean(acc * acc, axis=-1, keepdims=True)
        acc = acc * lax.rsqrt(ms2 + NORM_EPS) * gl_ref[...]
    o_ref[0] = acc


def _out_ffn(x, y_s5_tm, y_rw, w_out, g_ffn, w_up, w_down, g_final, d_s5, final_norm):
    bsz, seqlen, d_model = x.shape
    d_rw = y_rw.shape[2]
    tm = FFN_TM
    const = lambda shape: pl.BlockSpec(shape, lambda b, i: (0,) * len(shape))
    return pl.pallas_call(
        functools.partial(_out_ffn_kernel, final_norm=final_norm),
        grid=(bsz, seqlen // tm),
        in_specs=[
            pl.BlockSpec((1, tm, d_model), lambda b, i: (b, i, 0)),
            pl.BlockSpec((tm, d_s5), lambda b, i: (i, b)),
            pl.BlockSpec((1, tm, d_rw), lambda b, i: (b, i, 0)),
            const(w_out.shape), const(g_ffn.shape), const(w_up.shape),
            const(w_down.shape), const(g_final.shape),
        ],
        out_specs=pl.BlockSpec((1, tm, d_model), lambda b, i: (b, i, 0)),
        out_shape=jax.ShapeDtypeStruct((bsz, seqlen, d_model), x.dtype),
        compiler_params=pltpu.CompilerParams(
            dimension_semantics=("arbitrary", "arbitrary"),
            vmem_limit_bytes=VMEM_LIMIT),
        name="out_ffn",
    )(x, y_s5_tm, y_rw, w_out, g_ffn, w_up, w_down, g_final)


def _pad_cols(w, width):
    return jnp.pad(w, ((0, 0), (0, width - w.shape[1])))


def _pad_rows(w, height):
    return jnp.pad(w, ((0, height - w.shape[0]), (0, 0)))


def _s5_weights(lam_re, lam_im, log_dt, b_re, b_im, c_re, c_im):
    groups, n_p = lam_re.shape
    dt = jnp.exp(log_dt.astype(F32))[:, None]
    lr, li = lam_re.astype(F32), lam_im.astype(F32)
    mag = jnp.exp(lr * dt)
    a_re = mag * jnp.cos(li * dt)
    a_im = mag * jnp.sin(li * dt)
    den = lr * lr + li * li
    q_re = ((a_re - 1.0) * lr + a_im * li) / den
    q_im = (a_im * lr - (a_re - 1.0) * li) / den
    bb_re = q_re[:, :, None] * b_re - q_im[:, :, None] * b_im
    bb_im = q_re[:, :, None] * b_im + q_im[:, :, None] * b_re
    eye = jnp.eye(groups, dtype=F32)
    n_h = b_re.shape[2]

    def b_full(bb):
        return jnp.einsum('gph,gk->ghkp', bb, eye).reshape(groups * n_h, groups * n_p)

    def c_full(cc):
        return jnp.einsum('ghp,gk->gpkh', cc, eye).reshape(groups * n_p, groups * n_h)

    tile_w = 2 * LANES
    groups_per_tile = tile_w // n_p
    groups_per_kblock = LANES // n_h
    n_tiles = groups * n_p // tile_w

    def b_tiles(full):
        tiles = []
        for j in range(n_tiles):
            kb = (j * groups_per_tile) // groups_per_kblock
            tiles.append(full[kb * LANES:(kb + 1) * LANES, j * tile_w:(j + 1) * tile_w])
        return jnp.stack(tiles)

    wb = jnp.concatenate([b_tiles(b_full(bb_re)), b_tiles(b_full(bb_im))], axis=0).astype(BF16)

    kw = LANES // n_h * n_p
    n_out = groups * n_h // LANES

    def c_tiles(full):
        return jnp.stack([full[m * kw:(m + 1) * kw, m * LANES:(m + 1) * LANES] for m in range(n_out)])

    wc_re = c_tiles(c_full(c_re.astype(F32))).astype(BF16)
    wc_im = c_tiles(c_full(-c_im.astype(F32))).astype(BF16)
    return wb, a_re.reshape(1, -1), a_im.reshape(1, -1), wc_re, wc_im


def kernel(x, norm_mix_g, w_in, lam_re, lam_im, log_dt, b_re, b_im, c_re, c_im, d_skip, w_glu, b_glu, s5_out_g, mu_shift, w0, w2, a0, a2, g2, k_k, k_a, r_k, ln_x_w, ln_x_b, w_out, norm_ffn_g, w_up, w_down, norm_final_g):
    depth = w_in.shape[0]
    bsz = x.shape[0]
    d_s5 = d_skip.shape[1]
    d_rw = w0.shape[1]
    n_w, n_a, n_g = w2.shape[1], a2.shape[1], g2.shape[1]
    lora_w = -(-max(n_w, n_a) // LANES) * LANES
    lora_g = -(-n_g // LANES) * LANES
    row = lambda p: p.reshape(1, -1).astype(F32)

    h = x
    for i in range(depth):
        o = d_s5 + 3 * d_rw
        wi = w_in[i]
        w_cat = jnp.concatenate([
            wi[:, :o],
            _pad_cols(wi[:, o:o + n_w], lora_w),
            _pad_cols(wi[:, o + n_w:o + n_w + n_a], lora_w),
            _pad_cols(wi[:, o + n_w + n_a:], lora_g)], axis=1).astype(BF16)
        mu = mu_shift[i].reshape(1, -1)
        o2 = 3 * d_rw
        mu_cat = jnp.concatenate([
            mu[:, :o2],
            _pad_cols(mu[:, o2:o2 + n_w], lora_w),
            _pad_cols(mu[:, o2 + n_w:o2 + n_w + n_a], lora_w),
            _pad_cols(mu[:, o2 + n_w + n_a:], lora_g)], axis=1).astype(F32)

        u_tm, z_rw = _in_proj(h, row(norm_mix_g[i]), w_cat, d_s5)

        wb, a_re, a_im, wc_re, wc_im = _s5_weights(
            lam_re[i], lam_im[i], log_dt[i], b_re[i], b_im[i], c_re[i], c_im[i])
        y_s5 = _s5(u_tm.reshape(-1, d_s5), wb, a_re, a_im, wc_re, wc_im, row(d_skip[i]),
                   w_glu[i].astype(BF16), row(b_glu[i]), row(s5_out_g[i]), bsz)
        y_s5 = y_s5.reshape(-1, bsz * d_s5)

        y_rw = _rwkv(z_rw, mu_cat, row(w0[i]),
                     _pad_rows(w2[i], lora_w).astype(BF16), row(a0[i]),
                     _pad_rows(a2[i], lora_w).astype(BF16),
                     _pad_rows(g2[i], lora_g).astype(BF16),
                     row(k_k[i]), row(k_a[i]), row(r_k[i]), row(ln_x_w[i]), row(ln_x_b[i]),
                     d_rw, lora_w, lora_g)

        h = _out_ffn(h, y_s5, y_rw, w_out[i].astype(BF16), row(norm_ffn_g[i]),
                     w_up[i].astype(BF16), w_down[i].astype(BF16), row(norm_final_g), d_s5,
                     final_norm=(i == depth - 1))
    return h
```

```python
import functools
import math

import jax
import jax.numpy as jnp
from jax import lax
from jax.experimental import pallas as pl
from jax.experimental.pallas import tpu as pltpu

F32 = jnp.float32
BF16 = jnp.bfloat16

NORM_EPS = 1e-6
GN_EPS = 64e-5

S5_GROUP = 16
S5_STATE = 64
RWKV_HEAD = 64
LANES = 128
SUBLANES = 8
VMEM_LIMIT = 56 * 1024 * 1024

IN_TM = 512
S5_TC = 128
S5_STRIP = 512
RWKV_LB = 256
RWKV_C = 64
FFN_TM = 256
FFN_CHUNK = 1024


def _dot(a, b, dims=(((1,), (0,)), ((), ()))):
    return lax.dot_general(a.astype(BF16), b.astype(BF16), dims,
                           preferred_element_type=F32)


def _split2(a):
    hi = a.astype(BF16)
    lo = (a - hi.astype(F32)).astype(BF16)
    return hi, lo


def _dot3(a, b, dims=(((1,), (0,)), ((), ()))):
    ah, al = _split2(a)
    bh, bl = _split2(b)
    d = functools.partial(lax.dot_general, dimension_numbers=dims,
                          preferred_element_type=F32)
    return d(ah, bh) + (d(ah, bl) + d(al, bh))


def _dot_exact_lhs(a_bf16, b, passes):
    acc = None
    rem = b
    for _ in range(passes):
        part = rem.astype(BF16)
        term = jnp.dot(a_bf16, part, preferred_element_type=F32)
        acc = term if acc is None else acc + term
        rem = rem - part.astype(F32)
    return acc


MM = (((1,), (0,)), ((), ()))
NT = (((1,), (1,)), ((), ()))
TN = (((0,), (0,)), ((), ()))

_PASSES = {"p1": 1, "inv": 1, "w": 1, "out": 1, "state": 1}


def _mm(a, b, dims, passes):
    return _dot(a, b, dims) if passes == 1 else _dot3(a, b, dims)


def _in_proj_kernel(x_ref, g_ref, w_ref, u_ref, z_ref, *, d_s5):
    x = x_ref[0]
    ms = jnp.mean(x * x, axis=-1, keepdims=True)
    xn = (x * lax.rsqrt(ms + NORM_EPS) * g_ref[...]).astype(BF16)
    u_ref[...] = jnp.dot(xn, w_ref[:, :d_s5], preferred_element_type=F32)
    z_ref[0] = jnp.dot(xn, w_ref[:, d_s5:], preferred_element_type=F32)


def _in_proj(x, g, w, d_s5):
    bsz, seqlen, d_model = x.shape
    d_rw = w.shape[1] - d_s5
    tm = IN_TM
    return pl.pallas_call(
        functools.partial(_in_proj_kernel, d_s5=d_s5),
        grid=(bsz, seqlen // tm),
        in_specs=[
            pl.BlockSpec((1, tm, d_model), lambda b, i: (b, i, 0)),
            pl.BlockSpec((1, d_model), lambda b, i: (0, 0)),
            pl.BlockSpec(w.shape, lambda b, i: (0, 0)),
        ],
        out_specs=[
            pl.BlockSpec((tm, d_s5), lambda b, i: (i, b)),
            pl.BlockSpec((1, tm, d_rw), lambda b, i: (b, i, 0)),
        ],
        out_shape=[
            jax.ShapeDtypeStruct((seqlen, bsz * d_s5), F32),
            jax.ShapeDtypeStruct((bsz, seqlen, d_rw), F32),
        ],
        compiler_params=pltpu.CompilerParams(
            dimension_semantics=("arbitrary", "arbitrary"),
            vmem_limit_bytes=VMEM_LIMIT),
        name="in_proj",
    )(x, g, w)


def _s5_kernel(u_ref, wb_ref, are_ref, aim_ref, wcre_ref, wcim_ref, dskip_ref,
               wglu_ref, bglu_ref, gout_ref, o_ref, st_ref, carry_ref, *, bsz, tc):
    n_state = are_ref.shape[1]

    @pl.when(pl.program_id(0) == 0)
    def _():
        carry_ref[...] = jnp.zeros_like(carry_ref)

    u_bf = u_ref[...].astype(BF16)
    n_tiles = wb_ref.shape[0]
    tile_w = wb_ref.shape[2]
    tiles_per_half = n_tiles // 2
    for j in range(n_tiles):
        kb = (j % tiles_per_half) * tile_w // (S5_STATE // S5_GROUP * LANES)
        st_ref[:, j * tile_w:(j + 1) * tile_w] = jnp.dot(
            u_bf[:, kb * LANES:(kb + 1) * LANES], wb_ref[j],
            preferred_element_type=F32)

    for s in range(n_state // S5_STRIP):
        re_sl = slice(s * S5_STRIP, (s + 1) * S5_STRIP)
        im_sl = slice(n_state + s * S5_STRIP, n_state + (s + 1) * S5_STRIP)
        a_re = jnp.broadcast_to(are_ref[:, re_sl], (bsz, S5_STRIP))
        a_im = jnp.broadcast_to(aim_ref[:, re_sl], (bsz, S5_STRIP))

        def step(t, carry, re_sl=re_sl, im_sl=im_sl, a_re=a_re, a_im=a_im):
            s_re, s_im = carry
            rows = pl.ds(pl.multiple_of(t * bsz, bsz), bsz)
            n_re = a_re * s_re - a_im * s_im + st_ref[rows, re_sl]
            n_im = a_re * s_im + a_im * s_re + st_ref[rows, im_sl]
            st_ref[rows, re_sl] = n_re
            st_ref[rows, im_sl] = n_im
            return n_re, n_im

        s_re, s_im = lax.fori_loop(
            0, tc, step, (carry_ref[:, re_sl], carry_ref[:, im_sl]), unroll=4)
        carry_ref[:, re_sl] = s_re
        carry_ref[:, im_sl] = s_im

    n_out_tiles = wcre_ref.shape[0]
    kw = wcre_ref.shape[1]
    ys = []
    for m in range(n_out_tiles):
        x_re = st_ref[:, m * kw:(m + 1) * kw].astype(BF16)
        x_im = st_ref[:, n_state + m * kw:n_state + (m + 1) * kw].astype(BF16)
        ys.append(jnp.dot(x_re, wcre_ref[m], preferred_element_type=F32)
                  + jnp.dot(x_im, wcim_ref[m], preferred_element_type=F32))
    y = jnp.concatenate(ys, axis=-1) + dskip_ref[...] * u_ref[...]
    g = 0.5 * y * (1.0 + lax.erf(y * (1.0 / math.sqrt(2.0))))
    gate = jax.nn.sigmoid(
        jnp.dot(g.astype(BF16), wglu_ref[...], preferred_element_type=F32) + bglu_ref[...])
    out = g * gate
    ms = jnp.mean(out * out, axis=-1, keepdims=True)
    o_ref[...] = out * lax.rsqrt(ms + NORM_EPS) * gout_ref[...]


def _s5(u_tm, wb, a_re, a_im, wc_re, wc_im, d_skip, w_glu, b_glu, g_out, bsz):
    rows, d_s5 = u_tm.shape
    seqlen = rows // bsz
    tc = S5_TC
    n_state = a_re.shape[1]
    const = lambda shape: pl.BlockSpec(shape, lambda i: (0,) * len(shape))
    return pl.pallas_call(
        functools.partial(_s5_kernel, bsz=bsz, tc=tc),
        grid=(seqlen // tc,),
        in_specs=[
            pl.BlockSpec((tc * bsz, d_s5), lambda i: (i, 0)),
            const(wb.shape), const(a_re.shape), const(a_im.shape),
            const(wc_re.shape), const(wc_im.shape), const(d_skip.shape),
            const(w_glu.shape), const(b_glu.shape), const(g_out.shape),
        ],
        out_specs=pl.BlockSpec((tc * bsz, d_s5), lambda i: (i, 0)),
        out_shape=jax.ShapeDtypeStruct((rows, d_s5), F32),
        scratch_shapes=[
            pltpu.VMEM((tc * bsz, 2 * n_state), F32),
            pltpu.VMEM((bsz, 2 * n_state), F32),
        ],
        compiler_params=pltpu.CompilerParams(
            dimension_semantics=("arbitrary",),
            vmem_limit_bytes=VMEM_LIMIT),
        name="s5",
    )(u_tm, wb, a_re, a_im, wc_re, wc_im, d_skip, w_glu, b_glu, g_out)


def _seg_sum(x, ones_bd):
    w = ones_bd.shape[0]
    outs = [_dot_exact_lhs_t(x[:, i * w:(i + 1) * w], ones_bd)
            for i in range(x.shape[1] // w)]
    return jnp.concatenate(outs, axis=-1)


def _dot_exact_lhs_t(x, ones_bd, passes=2):
    acc = None
    rem = x
    for _ in range(passes):
        part = rem.astype(BF16)
        term = jnp.dot(part, ones_bd, preferred_element_type=F32)
        acc = term if acc is None else acc + term
        rem = rem - part.astype(F32)
    return acc


def _rwkv_kernel(z_ref, mu_ref, w0_ref, w2_ref, a0_ref, a2_ref, g2_ref, kk_ref,
                 ka_ref, rk_ref, lnw_ref, lnb_ref, o_ref,
                 st_ref, carry_ref, *, d_rw, lora_w, lora_g):
    lb = z_ref.shape[1]
    c = RWKV_C
    n_pairs = d_rw // LANES
    hd = RWKV_HEAD

    @pl.when(pl.program_id(1) == 0)
    def _():
        st_ref[...] = jnp.zeros_like(st_ref)
        carry_ref[...] = jnp.zeros_like(carry_ref)

    z = z_ref[0]
    row = lax.broadcasted_iota(jnp.int32, z.shape, 0)
    z_prev = jnp.where(row == 0, carry_ref[0:1, :], pltpu.roll(z, 1, 0))
    carry_ref[0:1, :] = z[lb - 1:lb, :]
    zz = z + (z_prev - z) * mu_ref[...]

    o = 3 * d_rw
    r = zz[:, 0:d_rw]
    k = zz[:, d_rw:2 * d_rw]
    v = zz[:, 2 * d_rw:o]
    zw = zz[:, o:o + lora_w]
    za = zz[:, o + lora_w:o + 2 * lora_w]
    zg = zz[:, o + 2 * lora_w:o + 2 * lora_w + lora_g]

    wpre = w0_ref[...] + _dot(jnp.tanh(zw), w2_ref[...])
    w = -(jnp.maximum(-wpre, 0.0) + jnp.log1p(jnp.exp(-jnp.abs(wpre)))) - 0.5
    logd = -jnp.exp(w)
    a = jax.nn.sigmoid(a0_ref[...] + _dot(za, a2_ref[...]))
    gate = _dot(jax.nn.sigmoid(zg), g2_ref[...])

    seg_w = 2 * LANES
    li = lax.broadcasted_iota(jnp.int32, (seg_w, seg_w), 0) // hd
    lj = lax.broadcasted_iota(jnp.int32, (seg_w, seg_w), 1) // hd
    ones_bd = jnp.where(li == lj, 1.0, 0.0).astype(BF16)

    kk = k * kk_ref[...]
    kk = kk / jnp.maximum(jnp.sqrt(_seg_sum(kk * kk, ones_bd)), 1e-12)
    k2 = k * (1.0 + (a - 1.0) * ka_ref[...])
    bonus = _seg_sum(r * k2 * rk_ref[...], ones_bd) * v

    bvec = kk * a

    n_chunks = lb // c
    ci = lax.broadcasted_iota(jnp.int32, (c, c), 0)
    cj = lax.broadcasted_iota(jnp.int32, (c, c), 1)
    ltri = jnp.where(cj <= ci, 1.0, 0.0).astype(BF16)

    ri = lax.broadcasted_iota(jnp.int32, (c, LANES), 0)
    rl = lax.broadcasted_iota(jnp.int32, (c, LANES), 1)
    head0 = rl < hd
    incl = (rl % hd) <= ri
    strict = (rl % hd) < ri

    pi = lax.broadcasted_iota(jnp.int32, (2 * c, 2 * c), 0)
    pj = lax.broadcasted_iota(jnp.int32, (2 * c, 2 * c), 1)
    level = pi ^ pj
    diag = pi == pj
    eye = jnp.where(diag, 1.0, 0.0)
    n_levels = int(math.log2(c))

    def stack_heads(x):
        return jnp.concatenate([jnp.where(head0, x, 0.0), jnp.where(head0, 0.0, x)], axis=0)

    chains = [(ic, p) for ic in range(n_chunks) for p in range(n_pairs)]
    lhs1, rhs1, v_bd, kk_bd, kb_end, b_end_bd, q_t, d_end = {}, {}, {}, {}, {}, {}, {}, {}
    for ic in range(n_chunks):
        rows = slice(ic * c, (ic + 1) * c)
        logd_c = logd[rows]
        cum = _dot_exact_lhs(ltri, logd_c, 3)
        cum_end = cum[c - 1:c, :]
        e_inv = jnp.exp(-cum)
        e_end = jnp.exp(cum_end - cum)
        q_c = r[rows] * jnp.exp(cum)
        kk_c = kk[rows] * jnp.exp(cum - logd_c)
        k_hat = k2[rows] * e_inv
        b_hat = bvec[rows] * e_inv
        k_end = k2[rows] * e_end
        b_end = bvec[rows] * e_end
        d_c = jnp.exp(cum_end)
        for p in range(n_pairs):
            ls = slice(p * LANES, (p + 1) * LANES)
            key = (ic, p)
            q_t[key] = q_c[:, ls]
            d_end[key] = d_c[:, ls]
            kk_bd[key] = stack_heads(kk_c[:, ls])
            v_bd[key] = stack_heads(v[rows, ls])
            b_end_bd[key] = stack_heads(b_end[:, ls])
            lhs1[key] = jnp.concatenate([q_c[:, ls], kk_c[:, ls]], axis=0)
            rhs1[key] = jnp.concatenate([stack_heads(k_hat[:, ls]), stack_heads(b_hat[:, ls])], axis=0)
            kb_end[key] = jnp.concatenate([stack_heads(k_end[:, ls]), b_end_bd[key]], axis=0)

    p1 = {k_: _mm(lhs1[k_], rhs1[k_], NT, _PASSES["p1"]) for k_ in chains}
    a_rk = {k_: jnp.where(incl, p1[k_][0:c, 0:LANES], 0.0) for k_ in chains}
    a_rb = {k_: jnp.where(incl, p1[k_][0:c, LANES:2 * LANES], 0.0) for k_ in chains}
    a_kk = {k_: jnp.where(strict, p1[k_][c:2 * c, 0:LANES], 0.0) for k_ in chains}
    n_bd = {k_: stack_heads(jnp.where(strict, p1[k_][c:2 * c, LANES:2 * LANES], 0.0)) for k_ in chains}

    t_inv = {k_: eye - jnp.where(level == 1, n_bd[k_], 0.0) for k_ in chains}
    for lg in range(1, n_levels):
        nt = {k_: _mm(jnp.where((level >> lg) == 1, n_bd[k_], 0.0), t_inv[k_], MM, _PASSES["inv"])
              for k_ in chains}
        t_inv = {k_: t_inv[k_] - _mm(t_inv[k_], nt[k_], MM, _PASSES["inv"]) for k_ in chains}

    av = {k_: _mm(stack_heads(a_kk[k_]), v_bd[k_], MM, _PASSES["w"]) for k_ in chains}
    w12 = {k_: _mm(t_inv[k_], jnp.concatenate([kk_bd[k_], av[k_]], axis=1), MM, _PASSES["w"])
           for k_ in chains}
    w1 = {k_: w12[k_][:, 0:LANES] for k_ in chains}
    vw = {k_: jnp.concatenate([v_bd[k_], -w12[k_][:, LANES:2 * LANES]], axis=0) for k_ in chains}

    y_intra = {k_: _mm(jnp.concatenate([a_rk[k_], a_rb[k_]], axis=1), vw[k_], MM, _PASSES["out"])
               for k_ in chains}
    q_eff = {k_: q_t[k_] - _mm(a_rb[k_], w1[k_], MM, _PASSES["out"]) for k_ in chains}
    g_mat = {k_: _mm(kb_end[k_], vw[k_], TN, _PASSES["state"]) for k_ in chains}
    m_mat = {k_: jnp.where(diag, jnp.broadcast_to(d_end[k_], (2 * c, LANES)), 0.0)
             - _mm(b_end_bd[k_], w1[k_], TN, _PASSES["state"]) for k_ in chains}

    state = [st_ref[p] for p in range(n_pairs)]
    y_rows = []
    for ic in range(n_chunks):
        y_c = [_mm(q_eff[(ic, p)], state[p], MM, _PASSES["state"]) + y_intra[(ic, p)]
               for p in range(n_pairs)]
        state = [_mm(m_mat[(ic, p)], state[p], MM, _PASSES["state"]) + g_mat[(ic, p)]
                 for p in range(n_pairs)]
        y_rows.append(jnp.concatenate(y_c, axis=1))
    for p in range(n_pairs):
        st_ref[p] = state[p]

    y = jnp.concatenate(y_rows, axis=0)
    mean = _seg_sum(y, ones_bd) * (1.0 / hd)
    yc = y - mean
    var = _seg_sum(yc * yc, ones_bd) * (1.0 / hd)
    yn = yc * lax.rsqrt(var + GN_EPS) * lnw_ref[...] + lnb_ref[...]
    o_ref[0] = (yn + bonus) * gate


def _rwkv(z, mu, w0, w2p, a0, a2p, g2p, k_k, k_a, r_k, ln_w, ln_b, d_rw, lora_w, lora_g):
    bsz, seqlen, zw = z.shape
    lb = RWKV_LB
    n_pairs = d_rw // LANES
    const = lambda shape: pl.BlockSpec(shape, lambda b, i: (0,) * len(shape))
    params = (mu, w0, w2p, a0, a2p, g2p, k_k, k_a, r_k, ln_w, ln_b)
    return pl.pallas_call(
        functools.partial(_rwkv_kernel, d_rw=d_rw, lora_w=lora_w, lora_g=lora_g),
        grid=(bsz, seqlen // lb),
        in_specs=[pl.BlockSpec((1, lb, zw), lambda b, i: (b, i, 0))]
                 + [const(p.shape) for p in params],
        out_specs=pl.BlockSpec((1, lb, d_rw), lambda b, i: (b, i, 0)),
        out_shape=jax.ShapeDtypeStruct((bsz, seqlen, d_rw), F32),
        scratch_shapes=[
            pltpu.VMEM((n_pairs, LANES, LANES), F32),
            pltpu.VMEM((SUBLANES, zw), F32),
        ],
        compiler_params=pltpu.CompilerParams(
            dimension_semantics=("arbitrary", "arbitrary"),
            vmem_limit_bytes=VMEM_LIMIT),
        name="rwkv",
    )(z, *params)


def _out_ffn_kernel(x_ref, ys_ref, yr_ref, wo_ref, gf_ref, wu_ref, wd_ref, gl_ref, o_ref, *, final_norm):
    x = x_ref[0]
    mix = jnp.concatenate([ys_ref[...].astype(BF16), yr_ref[0].astype(BF16)], axis=1)
    h = x + jnp.dot(mix, wo_ref[...], preferred_element_type=F32)
    ms = jnp.mean(h * h, axis=-1, keepdims=True)
    hn = (h * lax.rsqrt(ms + NORM_EPS) * gf_ref[...]).astype(BF16)
    d_ff = wu_ref.shape[1]
    acc = h
    for j in range(d_ff // FFN_CHUNK):
        sl = slice(j * FFN_CHUNK, (j + 1) * FFN_CHUNK)
        up = jnp.dot(hn, wu_ref[:, sl], preferred_element_type=F32)
        up = jnp.square(jnp.maximum(up, 0.0)).astype(BF16)
        acc = acc + jnp.dot(up, wd_ref[sl, :], preferred_element_type=F32)
    if final_norm:
        ms2 = jnp.mean(acc * acc, axis=-1, keepdims=True)
        acc = acc * lax.rsqrt(ms2 + NORM_EPS) * gl_ref[...]
    o_ref[0] = acc


def _out_ffn(x, y_s5_tm, y_rw, w_out, g_ffn, w_up, w_down, g_final, d_s5, final_norm):
    bsz, seqlen, d_model = x.shape
    d_rw = y_rw.shape[2]
    tm = FFN_TM
    const = lambda shape: pl.BlockSpec(shape, lambda b, i: (0,) * len(shape))
    return pl.pallas_call(
        functools.partial(_out_ffn_kernel, final_norm=final_norm),
        grid=(bsz, seqlen // tm),
        in_specs=[
            pl.BlockSpec((1, tm, d_model), lambda b, i: (b, i, 0)),
            pl.BlockSpec((tm, d_s5), lambda b, i: (i, b)),
            pl.BlockSpec((1, tm, d_rw), lambda b, i: (b, i, 0)),
            const(w_out.shape), const(g_ffn.shape), const(w_up.shape),
            const(w_down.shape), const(g_final.shape),
        ],
        out_specs=pl.BlockSpec((1, tm, d_model), lambda b, i: (b, i, 0)),
        out_shape=jax.ShapeDtypeStruct((bsz, seqlen, d_model), x.dtype),
        compiler_params=pltpu.CompilerParams(
            dimension_semantics=("arbitrary", "arbitrary"),
            vmem_limit_bytes=VMEM_LIMIT),
        name="out_ffn",
    )(x, y_s5_tm, y_rw, w_out, g_ffn, w_up, w_down, g_final)


def _pad_cols(w, width):
    return jnp.pad(w, ((0, 0), (0, width - w.shape[1])))


def _pad_rows(w, height):
    return jnp.pad(w, ((0, height - w.shape[0]), (0, 0)))


def _s5_weights(lam_re, lam_im, log_dt, b_re, b_im, c_re, c_im):
    groups, n_p = lam_re.shape
    dt = jnp.exp(log_dt.astype(F32))[:, None]
    lr, li = lam_re.astype(F32), lam_im.astype(F32)
    mag = jnp.exp(lr * dt)
    a_re = mag * jnp.cos(li * dt)
    a_im = mag * jnp.sin(li * dt)
    den = lr * lr + li * li
    q_re = ((a_re - 1.0) * lr + a_im * li) / den
    q_im = (a_im * lr - (a_re - 1.0) * li) / den
    bb_re = q_re[:, :, None] * b_re - q_im[:, :, None] * b_im
    bb_im = q_re[:, :, None] * b_im + q_im[:, :, None] * b_re
    eye = jnp.eye(groups, dtype=F32)
    n_h = b_re.shape[2]

    def b_full(bb):
        return jnp.einsum('gph,gk->ghkp', bb, eye).reshape(groups * n_h, groups * n_p)

    def c_full(cc):
        return jnp.einsum('ghp,gk->gpkh', cc, eye).reshape(groups * n_p, groups * n_h)

    tile_w = 2 * LANES
    groups_per_tile = tile_w // n_p
    groups_per_kblock = LANES // n_h
    n_tiles = groups * n_p // tile_w

    def b_tiles(full):
        tiles = []
        for j in range(n_tiles):
            kb = (j * groups_per_tile) // groups_per_kblock
            tiles.append(full[kb * LANES:(kb + 1) * LANES, j * tile_w:(j + 1) * tile_w])
        return jnp.stack(tiles)

    wb = jnp.concatenate([b_tiles(b_full(bb_re)), b_tiles(b_full(bb_im))], axis=0).astype(BF16)

    kw = LANES // n_h * n_p
    n_out = groups * n_h // LANES

    def c_tiles(full):
        return jnp.stack([full[m * kw:(m + 1) * kw, m * LANES:(m + 1) * LANES] for m in range(n_out)])

    wc_re = c_tiles(c_full(c_re.astype(F32))).astype(BF16)
    wc_im = c_tiles(c_full(-c_im.astype(F32))).astype(BF16)
    return wb, a_re.reshape(1, -1), a_im.reshape(1, -1), wc_re, wc_im


def kernel(x, norm_mix_g, w_in, lam_re, lam_im, log_dt, b_re, b_im, c_re, c_im, d_skip, w_glu, b_glu, s5_out_g, mu_shift, w0, w2, a0, a2, g2, k_k, k_a, r_k, ln_x_w, ln_x_b, w_out, norm_ffn_g, w_up, w_down, norm_final_g):
    depth = w_in.shape[0]
    bsz = x.shape[0]
    d_s5 = d_skip.shape[1]
    d_rw = w0.shape[1]
    n_w, n_a, n_g = w2.shape[1], a2.shape[1], g2.shape[1]
    lora_w = -(-max(n_w, n_a) // LANES) * LANES
    lora_g = -(-n_g // LANES) * LANES
    row = lambda p: p.reshape(1, -1).astype(F32)

    h = x
    for i in range(depth):
        o = d_s5 + 3 * d_rw
        wi = w_in[i]
        w_cat = jnp.concatenate([
            wi[:, :o],
            _pad_cols(wi[:, o:o + n_w], lora_w),
            _pad_cols(wi[:, o + n_w:o + n_w + n_a], lora_w),
            _pad_cols(wi[:, o + n_w + n_a:], lora_g)], axis=1).astype(BF16)
        mu = mu_shift[i].reshape(1, -1)
        o2 = 3 * d_rw
        mu_cat = jnp.concatenate([
            mu[:, :o2],
            _pad_cols(mu[:, o2:o2 + n_w], lora_w),
            _pad_cols(mu[:, o2 + n_w:o2 + n_w + n_a], lora_w),
            _pad_cols(mu[:, o2 + n_w + n_a:], lora_g)], axis=1).astype(F32)

        u_tm, z_rw = _in_proj(h, row(norm_mix_g[i]), w_cat, d_s5)

        wb, a_re, a_im, wc_re, wc_im = _s5_weights(
            lam_re[i], lam_im[i], log_dt[i], b_re[i], b_im[i], c_re[i], c_im[i])
        y_s5 = _s5(u_tm.reshape(-1, d_s5), wb, a_re, a_im, wc_re, wc_im, row(d_skip[i]),
                   w_glu[i].astype(BF16), row(b_glu[i]), row(s5_out_g[i]), bsz)
        y_s5 = y_s5.reshape(-1, bsz * d_s5)

        y_rw = _rwkv(z_rw, mu_cat, row(w0[i]),
                     _pad_rows(w2[i], lora_w).astype(BF16), row(a0[i]),
                     _pad_rows(a2[i], lora_w).astype(BF16),
                     _pad_rows(g2[i], lora_g).astype(BF16),
                     row(k_k[i]), row(k_a[i]), row(r_k[i]), row(ln_x_w[i]), row(ln_x_b[i]),
                     d_rw, lora_w, lora_g)

        h = _out_ffn(h, y_s5, y_rw, w_out[i].astype(BF16), row(norm_ffn_g[i]),
                     w_up[i].astype(BF16), w_down[i].astype(BF16), row(norm_final_g), d_s5,
                     final_norm=(i == depth - 1))
    return h
```

```python
import functools
import math

import jax
import jax.numpy as jnp
from jax import lax
from jax.experimental import pallas as pl
from jax.experimental.pallas import tpu as pltpu

F32 = jnp.float32
BF16 = jnp.bfloat16

NORM_EPS = 1e-6
GN_EPS = 64e-5

S5_GROUP = 16
S5_STATE = 64
RWKV_HEAD = 64
LANES = 128
SUBLANES = 8
VMEM_LIMIT = 56 * 1024 * 1024

IN_TM = 512
IN_TN = 512
S5_TC = 128
S5_STRIP = 512
RWKV_LB = 256
RWKV_C = 64
FFN_TM = 256
FFN_CHUNK = 1024

MM = (((1,), (0,)), ((), ()))
NT = (((1,), (1,)), ((), ()))
TN = (((0,), (0,)), ((), ()))


def _dot(a, b, dims=MM):
    return lax.dot_general(a.astype(BF16), b.astype(BF16), dims,
                           preferred_element_type=F32)


def _dot_split_rhs(a_bf16, b, passes):
    acc = None
    rem = b
    for _ in range(passes):
        part = rem.astype(BF16)
        term = jnp.dot(a_bf16, part, preferred_element_type=F32)
        acc = term if acc is None else acc + term
        rem = rem - part.astype(F32)
    return acc


def _dot_split_lhs(a, b_bf16, passes):
    acc = None
    rem = a
    for _ in range(passes):
        part = rem.astype(BF16)
        term = jnp.dot(part, b_bf16, preferred_element_type=F32)
        acc = term if acc is None else acc + term
        rem = rem - part.astype(F32)
    return acc


def _in_proj_kernel(x_ref, g_ref, w_ref, z_ref):
    x = x_ref[0]
    ms = jnp.mean(x * x, axis=-1, keepdims=True)
    xn = (x * lax.rsqrt(ms + NORM_EPS) * g_ref[...]).astype(BF16)
    n = w_ref.shape[1]
    for j in range(n // IN_TN):
        sl = slice(j * IN_TN, (j + 1) * IN_TN)
        z_ref[0, :, sl] = jnp.dot(xn, w_ref[:, sl], preferred_element_type=F32).astype(BF16)


def _in_proj(x, g, w):
    bsz, seqlen, d_model = x.shape
    tm = IN_TM
    return pl.pallas_call(
        _in_proj_kernel,
        grid=(bsz, seqlen // tm),
        in_specs=[
            pl.BlockSpec((1, tm, d_model), lambda b, i: (b, i, 0)),
            pl.BlockSpec((1, d_model), lambda b, i: (0, 0)),
            pl.BlockSpec(w.shape, lambda b, i: (0, 0)),
        ],
        out_specs=pl.BlockSpec((1, tm, w.shape[1]), lambda b, i: (b, i, 0)),
        out_shape=jax.ShapeDtypeStruct((bsz, seqlen, w.shape[1]), BF16),
        compiler_params=pltpu.CompilerParams(
            dimension_semantics=("arbitrary", "arbitrary"),
            vmem_limit_bytes=VMEM_LIMIT),
        name="in_proj",
    )(x, g, w)


def _s5_kernel(u_ref, wb_ref, are_ref, aim_ref, wcre_ref, wcim_ref, dskip_ref,
               wglu_ref, bglu_ref, gout_ref, o_ref, st_ref, carry_ref, *, pitch):
    bsz, tc, d_s5 = u_ref.shape
    n_state = are_ref.shape[1]
    n_slabs_half = n_state // LANES
    pad = pitch - tc

    @pl.when(pl.program_id(0) == 0)
    def _():
        carry_ref[...] = jnp.zeros_like(carry_ref)

    zero_pad = jnp.zeros((pad, d_s5), BF16)
    u_bf = jnp.concatenate(
        [piece for b in range(bsz) for piece in (u_ref[b], zero_pad)], axis=0)

    n_tiles = wb_ref.shape[0]
    tile_w = wb_ref.shape[2]
    tiles_per_half = n_tiles // 2
    slabs_per_tile = tile_w // LANES
    for j in range(n_tiles):
        kb = (j % tiles_per_half) * tile_w // (S5_STATE // S5_GROUP * LANES)
        res = jnp.dot(u_bf[:, kb * LANES:(kb + 1) * LANES], wb_ref[j],
                      preferred_element_type=F32)
        for h in range(slabs_per_tile):
            st_ref[j * slabs_per_tile + h] = res[:, h * LANES:(h + 1) * LANES]

    slabs_per_strip = S5_STRIP // LANES
    for s in range(n_slabs_half // slabs_per_strip):
        re_ids = [s * slabs_per_strip + q for q in range(slabs_per_strip)]
        im_ids = [n_slabs_half + i for i in re_ids]
        a_re = [jnp.broadcast_to(are_ref[:, i * LANES:(i + 1) * LANES], (bsz, LANES)) for i in re_ids]
        a_im = [jnp.broadcast_to(aim_ref[:, i * LANES:(i + 1) * LANES], (bsz, LANES)) for i in re_ids]

        def step(t, carry, re_ids=re_ids, im_ids=im_ids, a_re=a_re, a_im=a_im):
            s_re, s_im = carry
            rows = pl.ds(t, bsz, stride=pitch)
            n_re, n_im = [], []
            for q in range(len(re_ids)):
                n_re.append(a_re[q] * s_re[q] - a_im[q] * s_im[q] + st_ref[re_ids[q], rows, :])
                n_im.append(a_re[q] * s_im[q] + a_im[q] * s_re[q] + st_ref[im_ids[q], rows, :])
            for q in range(len(re_ids)):
                st_ref[re_ids[q], rows, :] = n_re[q]
                st_ref[im_ids[q], rows, :] = n_im[q]
            return tuple(n_re), tuple(n_im)

        init = (tuple(carry_ref[i] for i in re_ids), tuple(carry_ref[i] for i in im_ids))
        s_re, s_im = lax.fori_loop(0, tc, step, init, unroll=4)
        for q in range(len(re_ids)):
            carry_ref[re_ids[q]] = s_re[q]
            carry_ref[im_ids[q]] = s_im[q]

    n_out_tiles = wcre_ref.shape[0]
    slabs_per_out = wcre_ref.shape[1] // LANES
    ys = []
    for m in range(n_out_tiles):
        ids = [m * slabs_per_out + q for q in range(slabs_per_out)]
        x_re = jnp.concatenate([st_ref[i] for i in ids], axis=1).astype(BF16)
        x_im = jnp.concatenate([st_ref[n_slabs_half + i] for i in ids], axis=1).astype(BF16)
        ys.append(jnp.dot(x_re, wcre_ref[m], preferred_element_type=F32)
                  + jnp.dot(x_im, wcim_ref[m], preferred_element_type=F32))
    y = jnp.concatenate(ys, axis=-1) + dskip_ref[...] * u_bf.astype(F32)
    g = 0.5 * y * (1.0 + lax.erf(y * (1.0 / math.sqrt(2.0))))
    gate = jax.nn.sigmoid(
        jnp.dot(g.astype(BF16), wglu_ref[...], preferred_element_type=F32) + bglu_ref[...])
    out = g * gate
    ms = jnp.mean(out * out, axis=-1, keepdims=True)
    out = (out * lax.rsqrt(ms + NORM_EPS) * gout_ref[...]).astype(o_ref.dtype)
    for b in range(bsz):
        o_ref[b] = out[b * pitch:b * pitch + tc]


def _s5(z, col_block, wb, a_re, a_im, wc_re, wc_im, d_skip, w_glu, b_glu, g_out):
    bsz, seqlen, _ = z.shape
    d_s5 = d_skip.shape[1]
    tc = S5_TC
    pitch = tc + SUBLANES
    n_state = a_re.shape[1]
    const = lambda shape: pl.BlockSpec(shape, lambda i: (0,) * len(shape))
    return pl.pallas_call(
        functools.partial(_s5_kernel, pitch=pitch),
        grid=(seqlen // tc,),
        in_specs=[
            pl.BlockSpec((bsz, tc, d_s5), lambda i: (0, i, col_block)),
            const(wb.shape), const(a_re.shape), const(a_im.shape),
            const(wc_re.shape), const(wc_im.shape), const(d_skip.shape),
            const(w_glu.shape), const(b_glu.shape), const(g_out.shape),
        ],
        out_specs=pl.BlockSpec((bsz, tc, d_s5), lambda i: (0, i, 0)),
        out_shape=jax.ShapeDtypeStruct((bsz, seqlen, d_s5), BF16),
        scratch_shapes=[
            pltpu.VMEM((2 * n_state // LANES, bsz * pitch, LANES), F32),
            pltpu.VMEM((2 * n_state // LANES, bsz, LANES), F32),
        ],
        compiler_params=pltpu.CompilerParams(
            dimension_semantics=("arbitrary",),
            vmem_limit_bytes=VMEM_LIMIT),
        name="s5",
    )(z, wb, a_re, a_im, wc_re, wc_im, d_skip, w_glu, b_glu, g_out)


def _seg_sum(x, ones_bd):
    w = ones_bd.shape[0]
    outs = [_dot_split_lhs(x[:, i * w:(i + 1) * w], ones_bd, 2)
            for i in range(x.shape[1] // w)]
    return jnp.concatenate(outs, axis=-1)


def _rwkv_kernel(z_ref, mu_ref, w0_ref, w2_ref, a0_ref, a2_ref, g2_ref, kk_ref,
                 ka_ref, rk_ref, lnw_ref, lnb_ref, o_ref,
                 st_ref, carry_ref, *, d_rw, lora_w, lora_g):
    lb = z_ref.shape[1]
    c = RWKV_C
    n_pairs = d_rw // LANES
    hd = RWKV_HEAD

    @pl.when(pl.program_id(1) == 0)
    def _():
        st_ref[...] = jnp.zeros_like(st_ref)
        carry_ref[...] = jnp.zeros_like(carry_ref)

    z = z_ref[0].astype(F32)
    row = lax.broadcasted_iota(jnp.int32, z.shape, 0)
    z_prev = jnp.where(row == 0, carry_ref[0:1, :], pltpu.roll(z, 1, 0))
    carry_ref[0:1, :] = z[lb - 1:lb, :]
    zz = z + (z_prev - z) * mu_ref[...]

    o = 3 * d_rw
    r = zz[:, 0:d_rw]
    k = zz[:, d_rw:2 * d_rw]
    v = zz[:, 2 * d_rw:o]
    zw = zz[:, o:o + lora_w]
    za = zz[:, o + lora_w:o + 2 * lora_w]
    zg = zz[:, o + 2 * lora_w:o + 2 * lora_w + lora_g]

    wpre = w0_ref[...] + _dot(jnp.tanh(zw), w2_ref[...])
    w = -(jnp.maximum(-wpre, 0.0) + jnp.log1p(jnp.exp(-jnp.abs(wpre)))) - 0.5
    logd = -jnp.exp(w)
    a = jax.nn.sigmoid(a0_ref[...] + _dot(za, a2_ref[...]))
    gate = _dot(jax.nn.sigmoid(zg), g2_ref[...])

    seg_w = 2 * LANES
    li = lax.broadcasted_iota(jnp.int32, (seg_w, seg_w), 0) // hd
    lj = lax.broadcasted_iota(jnp.int32, (seg_w, seg_w), 1) // hd
    ones_bd = jnp.where(li == lj, 1.0, 0.0).astype(BF16)

    kk = k * kk_ref[...]
    kk = kk / jnp.maximum(jnp.sqrt(_seg_sum(kk * kk, ones_bd)), 1e-12)
    k2 = k * (1.0 + (a - 1.0) * ka_ref[...])
    bonus = _seg_sum(r * k2 * rk_ref[...], ones_bd) * v
    bvec = kk * a

    n_chunks = lb // c
    ci = lax.broadcasted_iota(jnp.int32, (c, c), 0)
    cj = lax.broadcasted_iota(jnp.int32, (c, c), 1)
    ltri = jnp.where(cj <= ci, 1.0, 0.0).astype(BF16)

    ri = lax.broadcasted_iota(jnp.int32, (c, LANES), 0)
    rl = lax.broadcasted_iota(jnp.int32, (c, LANES), 1)
    head0 = rl < hd
    incl = (rl % hd) <= ri
    strict = (rl % hd) < ri

    pi = lax.broadcasted_iota(jnp.int32, (2 * c, 2 * c), 0)
    pj = lax.broadcasted_iota(jnp.int32, (2 * c, 2 * c), 1)
    level = pi ^ pj
    diag = pi == pj
    eye = jnp.where(diag, 1.0, 0.0)
    n_levels = int(math.log2(c))

    def stack_heads(x):
        return jnp.concatenate([jnp.where(head0, x, 0.0), jnp.where(head0, 0.0, x)], axis=0)

    chains = [(ic, p) for ic in range(n_chunks) for p in range(n_pairs)]
    lhs1, rhs1, v_bd, kk_bd, kb_end, b_end_bd, q_t, d_end = {}, {}, {}, {}, {}, {}, {}, {}
    for ic in range(n_chunks):
        rows = slice(ic * c, (ic + 1) * c)
        logd_c = logd[rows]
        cum = _dot_split_rhs(ltri, logd_c, 3)
        cum_end = cum[c - 1:c, :]
        e_inv = jnp.exp(-cum)
        e_end = jnp.exp(cum_end - cum)
        q_c = r[rows] * jnp.exp(cum)
        kk_c = kk[rows] * jnp.exp(cum - logd_c)
        k_hat = k2[rows] * e_inv
        b_hat = bvec[rows] * e_inv
        k_end = k2[rows] * e_end
        b_end = bvec[rows] * e_end
        d_c = jnp.exp(cum_end)
        for p in range(n_pairs):
            ls = slice(p * LANES, (p + 1) * LANES)
            key = (ic, p)
            q_t[key] = q_c[:, ls]
            d_end[key] = d_c[:, ls]
            kk_bd[key] = stack_heads(kk_c[:, ls])
            v_bd[key] = stack_heads(v[rows, ls])
            b_end_bd[key] = stack_heads(b_end[:, ls])
            lhs1[key] = jnp.concatenate([q_c[:, ls], kk_c[:, ls]], axis=0)
            rhs1[key] = jnp.concatenate([stack_heads(k_hat[:, ls]), stack_heads(b_hat[:, ls])], axis=0)
            kb_end[key] = jnp.concatenate([stack_heads(k_end[:, ls]), b_end_bd[key]], axis=0)

    p1 = {k_: _dot(lhs1[k_], rhs1[k_], NT) for k_ in chains}
    a_rk = {k_: jnp.where(incl, p1[k_][0:c, 0:LANES], 0.0) for k_ in chains}
    a_rb = {k_: jnp.where(incl, p1[k_][0:c, LANES:2 * LANES], 0.0) for k_ in chains}
    a_kk = {k_: jnp.where(strict, p1[k_][c:2 * c, 0:LANES], 0.0) for k_ in chains}
    n_bd = {k_: stack_heads(jnp.where(strict, p1[k_][c:2 * c, LANES:2 * LANES], 0.0)) for k_ in chains}

    t_inv = {k_: eye - jnp.where(level == 1, n_bd[k_], 0.0) for k_ in chains}
    for lg in range(1, n_levels):
        nt = {k_: _dot(jnp.where((level >> lg) == 1, n_bd[k_], 0.0), t_inv[k_]) for k_ in chains}
        t_inv = {k_: t_inv[k_] - _dot(t_inv[k_], nt[k_]) for k_ in chains}

    av = {k_: _dot(stack_heads(a_kk[k_]), v_bd[k_]) for k_ in chains}
    w12 = {k_: _dot(t_inv[k_], jnp.concatenate([kk_bd[k_], av[k_]], axis=1)) for k_ in chains}
    w1 = {k_: w12[k_][:, 0:LANES] for k_ in chains}
    vw = {k_: jnp.concatenate([v_bd[k_], -w12[k_][:, LANES:2 * LANES]], axis=0) for k_ in chains}

    y_intra = {k_: _dot(jnp.concatenate([a_rk[k_], a_rb[k_]], axis=1), vw[k_]) for k_ in chains}
    q_eff = {k_: q_t[k_] - _dot(a_rb[k_], w1[k_]) for k_ in chains}
    g_mat = {k_: _dot(kb_end[k_], vw[k_], TN) for k_ in chains}
    m_mat = {k_: jnp.where(diag, jnp.broadcast_to(d_end[k_], (2 * c, LANES)), 0.0)
             - _dot(b_end_bd[k_], w1[k_], TN) for k_ in chains}

    state = [st_ref[p] for p in range(n_pairs)]
    y_rows = []
    for ic in range(n_chunks):
        y_c = [_dot(q_eff[(ic, p)], state[p]) + y_intra[(ic, p)] for p in range(n_pairs)]
        state = [_dot(m_mat[(ic, p)], state[p]) + g_mat[(ic, p)] for p in range(n_pairs)]
        y_rows.append(jnp.concatenate(y_c, axis=1))
    for p in range(n_pairs):
        st_ref[p] = state[p]

    y = jnp.concatenate(y_rows, axis=0)
    mean = _seg_sum(y, ones_bd) * (1.0 / hd)
    yc = y - mean
    var = _seg_sum(yc * yc, ones_bd) * (1.0 / hd)
    yn = yc * lax.rsqrt(var + GN_EPS) * lnw_ref[...] + lnb_ref[...]
    o_ref[0] = ((yn + bonus) * gate).astype(o_ref.dtype)


def _rwkv(z, mu, w0, w2p, a0, a2p, g2p, k_k, k_a, r_k, ln_w, ln_b, d_rw, lora_w, lora_g):
    bsz, seqlen, _ = z.shape
    zw = 3 * d_rw + 2 * lora_w + lora_g
    lb = RWKV_LB
    n_pairs = d_rw // LANES
    const = lambda shape: pl.BlockSpec(shape, lambda b, i: (0,) * len(shape))
    params = (mu, w0, w2p, a0, a2p, g2p, k_k, k_a, r_k, ln_w, ln_b)
    return pl.pallas_call(
        functools.partial(_rwkv_kernel, d_rw=d_rw, lora_w=lora_w, lora_g=lora_g),
        grid=(bsz, seqlen // lb),
        in_specs=[pl.BlockSpec((1, lb, zw), lambda b, i: (b, i, 0))]
                 + [const(p.shape) for p in params],
        out_specs=pl.BlockSpec((1, lb, d_rw), lambda b, i: (b, i, 0)),
        out_shape=jax.ShapeDtypeStruct((bsz, seqlen, d_rw), BF16),
        scratch_shapes=[
            pltpu.VMEM((n_pairs, LANES, LANES), F32),
            pltpu.VMEM((SUBLANES, zw), F32),
        ],
        compiler_params=pltpu.CompilerParams(
            dimension_semantics=("arbitrary", "arbitrary"),
            vmem_limit_bytes=VMEM_LIMIT),
        name="rwkv",
    )(z, *params)


def _out_ffn_kernel(x_ref, ys_ref, yr_ref, wo_ref, gf_ref, wu_ref, wd_ref, gl_ref, o_ref, *, final_norm):
    x = x_ref[0]
    mix = jnp.concatenate([ys_ref[0], yr_ref[0]], axis=1)
    h = x + jnp.dot(mix, wo_ref[...], preferred_element_type=F32)
    ms = jnp.mean(h * h, axis=-1, keepdims=True)
    hn = (h * lax.rsqrt(ms + NORM_EPS) * gf_ref[...]).astype(BF16)
    d_ff = wu_ref.shape[1]
    acc = h
    for j in range(d_ff // FFN_CHUNK):
        sl = slice(j * FFN_CHUNK, (j + 1) * FFN_CHUNK)
        up = jnp.dot(hn, wu_ref[:, sl], preferred_element_type=F32)
        up = jnp.square(jnp.maximum(up, 0.0)).astype(BF16)
        acc = acc + jnp.dot(up, wd_ref[sl, :], preferred_element_type=F32)
    if final_norm:
        ms2 = jnp.mean(acc * acc, axis=-1, keepdims=True)
        acc = acc * lax.rsqrt(ms2 + NORM_EPS) * gl_ref[...]
    o_ref[0] = acc


def _out_ffn(x, y_s5, y_rw, w_out, g_ffn, w_up, w_down, g_final, final_norm):
    bsz, seqlen, d_model = x.shape
    tm = FFN_TM
    const = lambda shape: pl.BlockSpec(shape, lambda b, i: (0,) * len(shape))
    act = lambda width: pl.BlockSpec((1, tm, width), lambda b, i: (b, i, 0))
    return pl.pallas_call(
        functools.partial(_out_ffn_kernel, final_norm=final_norm),
        grid=(bsz, seqlen // tm),
        in_specs=[
            act(d_model), act(y_s5.shape[2]), act(y_rw.shape[2]),
            const(w_out.shape), const(g_ffn.shape), const(w_up.shape),
            const(w_down.shape), const(g_final.shape),
        ],
        out_specs=act(d_model),
        out_shape=jax.ShapeDtypeStruct((bsz, seqlen, d_model), x.dtype),
        compiler_params=pltpu.CompilerParams(
            dimension_semantics=("arbitrary", "arbitrary"),
            vmem_limit_bytes=VMEM_LIMIT),
        name="out_ffn",
    )(x, y_s5, y_rw, w_out, g_ffn, w_up, w_down, g_final)


def _pad_cols(w, width):
    return jnp.pad(w, ((0, 0), (0, width - w.shape[1])))


def _pad_rows(w, height):
    return jnp.pad(w, ((0, height - w.shape[0]), (0, 0)))


def _s5_weights(lam_re, lam_im, log_dt, b_re, b_im, c_re, c_im):
    groups, n_p = lam_re.shape
    dt = jnp.exp(log_dt.astype(F32))[:, None]
    lr, li = lam_re.astype(F32), lam_im.astype(F32)
    mag = jnp.exp(lr * dt)
    a_re = mag * jnp.cos(li * dt)
    a_im = mag * jnp.sin(li * dt)
    den = lr * lr + li * li
    q_re = ((a_re - 1.0) * lr + a_im * li) / den
    q_im = (a_im * lr - (a_re - 1.0) * li) / den
    bb_re = q_re[:, :, None] * b_re - q_im[:, :, None] * b_im
    bb_im = q_re[:, :, None] * b_im + q_im[:, :, None] * b_re
    eye = jnp.eye(groups, dtype=F32)
    n_h = b_re.shape[2]

    def b_full(bb):
        return jnp.einsum('gph,gk->ghkp', bb, eye).reshape(groups * n_h, groups * n_p)

    def c_full(cc):
        return jnp.einsum('ghp,gk->gpkh', cc, eye).reshape(groups * n_p, groups * n_h)

    tile_w = 2 * LANES
    groups_per_tile = tile_w // n_p
    groups_per_kblock = LANES // n_h
    n_tiles = groups * n_p // tile_w

    def b_tiles(full):
        tiles = []
        for j in range(n_tiles):
            kb = (j * groups_per_tile) // groups_per_kblock
            tiles.append(full[kb * LANES:(kb + 1) * LANES, j * tile_w:(j + 1) * tile_w])
        return jnp.stack(tiles)

    wb = jnp.concatenate([b_tiles(b_full(bb_re)), b_tiles(b_full(bb_im))], axis=0).astype(BF16)

    kw = LANES // n_h * n_p
    n_out = groups * n_h // LANES

    def c_tiles(full):
        return jnp.stack([full[m * kw:(m + 1) * kw, m * LANES:(m + 1) * LANES] for m in range(n_out)])

    wc_re = c_tiles(c_full(c_re.astype(F32))).astype(BF16)
    wc_im = c_tiles(c_full(-c_im.astype(F32))).astype(BF16)
    return wb, a_re.reshape(1, -1), a_im.reshape(1, -1), wc_re, wc_im


def kernel(x, norm_mix_g, w_in, lam_re, lam_im, log_dt, b_re, b_im, c_re, c_im, d_skip, w_glu, b_glu, s5_out_g, mu_shift, w0, w2, a0, a2, g2, k_k, k_a, r_k, ln_x_w, ln_x_b, w_out, norm_ffn_g, w_up, w_down, norm_final_g):
    depth = w_in.shape[0]
    d_s5 = d_skip.shape[1]
    d_rw = w0.shape[1]
    n_w, n_a, n_g = w2.shape[1], a2.shape[1], g2.shape[1]
    lora_w = -(-max(n_w, n_a) // LANES) * LANES
    lora_g = -(-n_g // LANES) * LANES
    zw = 3 * d_rw + 2 * lora_w + lora_g
    assert zw % d_s5 == 0
    row = lambda p: p.reshape(1, -1).astype(F32)

    def rwkv_cols(m):
        o = 3 * d_rw
        return jnp.concatenate([
            m[:, :o],
            _pad_cols(m[:, o:o + n_w], lora_w),
            _pad_cols(m[:, o + n_w:o + n_w + n_a], lora_w),
            _pad_cols(m[:, o + n_w + n_a:], lora_g)], axis=1)

    h = x
    for i in range(depth):
        wi = w_in[i]
        w_cat = jnp.concatenate([rwkv_cols(wi[:, d_s5:]), wi[:, :d_s5]], axis=1).astype(BF16)
        mu_cat = rwkv_cols(mu_shift[i].reshape(1, -1)).astype(F32)

        z = _in_proj(h, row(norm_mix_g[i]), w_cat)

        wb, a_re, a_im, wc_re, wc_im = _s5_weights(
            lam_re[i], lam_im[i], log_dt[i], b_re[i], b_im[i], c_re[i], c_im[i])
        y_s5 = _s5(z, zw // d_s5, wb, a_re, a_im, wc_re, wc_im, row(d_skip[i]),
                   w_glu[i].astype(BF16), row(b_glu[i]), row(s5_out_g[i]))

        y_rw = _rwkv(z, mu_cat, row(w0[i]),
                     _pad_rows(w2[i], lora_w).astype(BF16), row(a0[i]),
                     _pad_rows(a2[i], lora_w).astype(BF16),
                     _pad_rows(g2[i], lora_g).astype(BF16),
                     row(k_k[i]), row(k_a[i]), row(r_k[i]), row(ln_x_w[i]), row(ln_x_b[i]),
                     d_rw, lora_w, lora_g)

        h = _out_ffn(h, y_s5, y_rw, w_out[i].astype(BF16), row(norm_ffn_g[i]),
                     w_up[i].astype(BF16), w_down[i].astype(BF16), row(norm_final_g),
                     final_norm=(i == depth - 1))
    return h
```

```python
import functools
import math

import jax
import jax.numpy as jnp
from jax import lax
from jax.experimental import pallas as pl
from jax.experimental.pallas import tpu as pltpu

F32 = jnp.float32
BF16 = jnp.bfloat16

NORM_EPS = 1e-6
GN_EPS = 64e-5

S5_GROUP = 16
S5_STATE = 64
RWKV_HEAD = 64
LANES = 128
SUBLANES = 8
VMEM_LIMIT = 56 * 1024 * 1024

IN_TM = 512
IN_TN = 512
S5_TC = 128
S5_STRIP = 512
RWKV_LB = 256
RWKV_C = 64
FFN_TM = 256
FFN_CHUNK = 1024

MM = (((1,), (0,)), ((), ()))
NT = (((1,), (1,)), ((), ()))
TN = (((0,), (0,)), ((), ()))


def _dot(a, b, dims=MM):
    return lax.dot_general(a.astype(BF16), b.astype(BF16), dims,
                           preferred_element_type=F32)


def _dot_split_rhs(a_bf16, b, passes):
    acc = None
    rem = b
    for _ in range(passes):
        part = rem.astype(BF16)
        term = jnp.dot(a_bf16, part, preferred_element_type=F32)
        acc = term if acc is None else acc + term
        rem = rem - part.astype(F32)
    return acc


def _in_proj_kernel(x_ref, g_ref, w_ref, z_ref):
    x = x_ref[0]
    ms = jnp.mean(x * x, axis=-1, keepdims=True)
    xn = (x * lax.rsqrt(ms + NORM_EPS) * g_ref[...]).astype(BF16)
    n = w_ref.shape[1]
    for j in range(n // IN_TN):
        sl = slice(j * IN_TN, (j + 1) * IN_TN)
        z_ref[0, :, sl] = jnp.dot(xn, w_ref[:, sl], preferred_element_type=F32).astype(BF16)


def _in_proj(x, g, w):
    bsz, seqlen, d_model = x.shape
    tm = IN_TM
    return pl.pallas_call(
        _in_proj_kernel,
        grid=(bsz, seqlen // tm),
        in_specs=[
            pl.BlockSpec((1, tm, d_model), lambda b, i: (b, i, 0)),
            pl.BlockSpec((1, d_model), lambda b, i: (0, 0)),
            pl.BlockSpec(w.shape, lambda b, i: (0, 0)),
        ],
        out_specs=pl.BlockSpec((1, tm, w.shape[1]), lambda b, i: (b, i, 0)),
        out_shape=jax.ShapeDtypeStruct((bsz, seqlen, w.shape[1]), BF16),
        compiler_params=pltpu.CompilerParams(
            dimension_semantics=("arbitrary", "arbitrary"),
            vmem_limit_bytes=VMEM_LIMIT),
        name="in_proj",
    )(x, g, w)


def _s5_kernel(u_ref, wb_ref, are_ref, aim_ref, wcre_ref, wcim_ref, dskip_ref,
               wglu_ref, bglu_ref, gout_ref, o_ref, st_ref, carry_ref, *, pitch):
    bsz, tc, d_s5 = u_ref.shape
    n_state = are_ref.shape[1]
    n_slabs_half = n_state // LANES
    pad = pitch - tc

    @pl.when(pl.program_id(0) == 0)
    def _():
        carry_ref[...] = jnp.zeros_like(carry_ref)

    zero_pad = jnp.zeros((pad, d_s5), BF16)
    u_bf = jnp.concatenate(
        [piece for b in range(bsz) for piece in (u_ref[b], zero_pad)], axis=0)

    n_tiles = wb_ref.shape[0]
    tile_w = wb_ref.shape[2]
    tiles_per_half = n_tiles // 2
    slabs_per_tile = tile_w // LANES
    for j in range(n_tiles):
        kb = (j % tiles_per_half) * tile_w // (S5_STATE // S5_GROUP * LANES)
        res = jnp.dot(u_bf[:, kb * LANES:(kb + 1) * LANES], wb_ref[j],
                      preferred_element_type=F32)
        for h in range(slabs_per_tile):
            st_ref[j * slabs_per_tile + h] = res[:, h * LANES:(h + 1) * LANES]

    slabs_per_strip = S5_STRIP // LANES
    for s in range(n_slabs_half // slabs_per_strip):
        re_ids = [s * slabs_per_strip + q for q in range(slabs_per_strip)]
        im_ids = [n_slabs_half + i for i in re_ids]
        a_re = [jnp.broadcast_to(are_ref[:, i * LANES:(i + 1) * LANES], (bsz, LANES)) for i in re_ids]
        a_im = [jnp.broadcast_to(aim_ref[:, i * LANES:(i + 1) * LANES], (bsz, LANES)) for i in re_ids]

        def step(t, carry, re_ids=re_ids, im_ids=im_ids, a_re=a_re, a_im=a_im):
            s_re, s_im = carry
            rows = pl.ds(t, bsz, stride=pitch)
            n_re, n_im = [], []
            for q in range(len(re_ids)):
                n_re.append(a_re[q] * s_re[q] - a_im[q] * s_im[q] + st_ref[re_ids[q], rows, :])
                n_im.append(a_re[q] * s_im[q] + a_im[q] * s_re[q] + st_ref[im_ids[q], rows, :])
            for q in range(len(re_ids)):
                st_ref[re_ids[q], rows, :] = n_re[q]
                st_ref[im_ids[q], rows, :] = n_im[q]
            return tuple(n_re), tuple(n_im)

        init = (tuple(carry_ref[i] for i in re_ids), tuple(carry_ref[i] for i in im_ids))
        s_re, s_im = lax.fori_loop(0, tc, step, init, unroll=4)
        for q in range(len(re_ids)):
            carry_ref[re_ids[q]] = s_re[q]
            carry_ref[im_ids[q]] = s_im[q]

    n_out_tiles = wcre_ref.shape[0]
    slabs_per_out = wcre_ref.shape[1] // LANES
    ys = []
    for m in range(n_out_tiles):
        ids = [m * slabs_per_out + q for q in range(slabs_per_out)]
        x_re = jnp.concatenate([st_ref[i] for i in ids], axis=1).astype(BF16)
        x_im = jnp.concatenate([st_ref[n_slabs_half + i] for i in ids], axis=1).astype(BF16)
        ys.append(jnp.dot(x_re, wcre_ref[m], preferred_element_type=F32)
                  + jnp.dot(x_im, wcim_ref[m], preferred_element_type=F32))
    y = jnp.concatenate(ys, axis=-1) + dskip_ref[...] * u_bf.astype(F32)
    g = 0.5 * y * (1.0 + lax.erf(y * (1.0 / math.sqrt(2.0))))
    gate = jax.nn.sigmoid(
        jnp.dot(g.astype(BF16), wglu_ref[...], preferred_element_type=F32) + bglu_ref[...])
    out = g * gate
    ms = jnp.mean(out * out, axis=-1, keepdims=True)
    out = (out * lax.rsqrt(ms + NORM_EPS) * gout_ref[...]).astype(o_ref.dtype)
    for b in range(bsz):
        o_ref[b] = out[b * pitch:b * pitch + tc]


def _s5(z, col_block, wb, a_re, a_im, wc_re, wc_im, d_skip, w_glu, b_glu, g_out):
    bsz, seqlen, _ = z.shape
    d_s5 = d_skip.shape[1]
    tc = S5_TC
    pitch = tc + SUBLANES
    n_state = a_re.shape[1]
    const = lambda shape: pl.BlockSpec(shape, lambda i: (0,) * len(shape))
    return pl.pallas_call(
        functools.partial(_s5_kernel, pitch=pitch),
        grid=(seqlen // tc,),
        in_specs=[
            pl.BlockSpec((bsz, tc, d_s5), lambda i: (0, i, col_block)),
            const(wb.shape), const(a_re.shape), const(a_im.shape),
            const(wc_re.shape), const(wc_im.shape), const(d_skip.shape),
            const(w_glu.shape), const(b_glu.shape), const(g_out.shape),
        ],
        out_specs=pl.BlockSpec((bsz, tc, d_s5), lambda i: (0, i, 0)),
        out_shape=jax.ShapeDtypeStruct((bsz, seqlen, d_s5), BF16),
        scratch_shapes=[
            pltpu.VMEM((2 * n_state // LANES, bsz * pitch, LANES), F32),
            pltpu.VMEM((2 * n_state // LANES, bsz, LANES), F32),
        ],
        compiler_params=pltpu.CompilerParams(
            dimension_semantics=("arbitrary",),
            vmem_limit_bytes=VMEM_LIMIT),
        name="s5",
    )(z, wb, a_re, a_im, wc_re, wc_im, d_skip, w_glu, b_glu, g_out)


_RWKV_BUFS = (("lhs1", 2, BF16), ("rhs1", 4, BF16), ("v_bd", 2, BF16), ("kk_bd", 2, BF16),
              ("kb_end", 4, BF16), ("b_end", 2, BF16), ("q_t", 1, F32))


def _seg_sum(x, ones_bd):
    w = ones_bd.shape[0]
    outs = [_dot(x[:, i * w:(i + 1) * w], ones_bd) for i in range(x.shape[1] // w)]
    return jnp.concatenate(outs, axis=-1)


def _head_ones(hd):
    seg_w = 2 * LANES
    li = lax.broadcasted_iota(jnp.int32, (seg_w, seg_w), 0) // hd
    lj = lax.broadcasted_iota(jnp.int32, (seg_w, seg_w), 1) // hd
    return jnp.where(li == lj, 1.0, 0.0).astype(BF16)


def _rwkv_prepare(z_ref, prm, carry_ref, buf, *, d_rw, lora_w, lora_g):
    (mu_ref, w0_ref, w2_ref, a0_ref, a2_ref, g2_ref, kk_ref, ka_ref, rk_ref) = prm
    lb = z_ref.shape[1]
    c = RWKV_C
    hd = RWKV_HEAD
    n_pairs = d_rw // LANES

    z = z_ref[0].astype(F32)
    row = lax.broadcasted_iota(jnp.int32, z.shape, 0)
    z_prev = jnp.where(row == 0, carry_ref[0:1, :], pltpu.roll(z, 1, 0))
    carry_ref[0:1, :] = z[lb - 1:lb, :]
    zz = z + (z_prev - z) * mu_ref[...]

    o = 3 * d_rw
    r = zz[:, 0:d_rw]
    k = zz[:, d_rw:2 * d_rw]
    v = zz[:, 2 * d_rw:o]
    zw = zz[:, o:o + lora_w]
    za = zz[:, o + lora_w:o + 2 * lora_w]
    zg = zz[:, o + 2 * lora_w:o + 2 * lora_w + lora_g]

    wpre = w0_ref[...] + _dot(jnp.tanh(zw), w2_ref[...])
    logd = -math.exp(-0.5) * jax.nn.sigmoid(wpre)
    a = jax.nn.sigmoid(a0_ref[...] + _dot(za, a2_ref[...]))
    buf["gate"][...] = _dot(jax.nn.sigmoid(zg), g2_ref[...])

    ones_bd = _head_ones(hd)
    kk = k * kk_ref[...]
    kk = kk * lax.rsqrt(jnp.maximum(_seg_sum(kk * kk, ones_bd), 1e-24))
    k2 = k * (1.0 + (a - 1.0) * ka_ref[...])
    buf["bonus"][...] = _seg_sum(r * k2 * rk_ref[...], ones_bd) * v
    bvec = kk * a

    ci = lax.broadcasted_iota(jnp.int32, (c, c), 0)
    cj = lax.broadcasted_iota(jnp.int32, (c, c), 1)
    ltri = jnp.where(cj <= ci, 1.0, 0.0).astype(BF16)
    head0 = lax.broadcasted_iota(jnp.int32, (c, LANES), 1) < hd

    def stack_heads(x):
        return jnp.concatenate([jnp.where(head0, x, 0.0), jnp.where(head0, 0.0, x)], axis=0)

    for ic in range(lb // c):
        rows = slice(ic * c, (ic + 1) * c)
        logd_c = logd[rows]
        cum = _dot_split_rhs(ltri, logd_c, 3)
        cum_end = cum[c - 1:c, :]
        e_inv = jnp.exp(-cum)
        e_end = jnp.exp(cum_end - cum)
        q_c = r[rows] * jnp.exp(cum)
        kk_c = kk[rows] * jnp.exp(cum - logd_c)
        k_hat = k2[rows] * e_inv
        b_hat = bvec[rows] * e_inv
        k_end = k2[rows] * e_end
        b_end = bvec[rows] * e_end
        buf["d_end"][ic] = jnp.broadcast_to(jnp.exp(cum_end), (SUBLANES, d_rw))
        for p in range(n_pairs):
            ls = slice(p * LANES, (p + 1) * LANES)
            i = ic * n_pairs + p
            b_end_bd = stack_heads(b_end[:, ls])
            buf["q_t"][i] = q_c[:, ls]
            buf["kk_bd"][i] = stack_heads(kk_c[:, ls]).astype(BF16)
            buf["v_bd"][i] = stack_heads(v[rows, ls]).astype(BF16)
            buf["b_end"][i] = b_end_bd.astype(BF16)
            buf["lhs1"][i] = jnp.concatenate([q_c[:, ls], kk_c[:, ls]], axis=0).astype(BF16)
            buf["rhs1"][i] = jnp.concatenate(
                [stack_heads(k_hat[:, ls]), stack_heads(b_hat[:, ls])], axis=0).astype(BF16)
            buf["kb_end"][i] = jnp.concatenate(
                [stack_heads(k_end[:, ls]), b_end_bd], axis=0).astype(BF16)


def _rwkv_delta(buf, st_ref, lnw_ref, lnb_ref, o_ref, *, d_rw, n_chunks):
    c = RWKV_C
    hd = RWKV_HEAD
    n_pairs = d_rw // LANES
    chains = [(ic, p) for ic in range(n_chunks) for p in range(n_pairs)]
    idx = lambda k_: k_[0] * n_pairs + k_[1]

    ri = lax.broadcasted_iota(jnp.int32, (c, LANES), 0)
    rl = lax.broadcasted_iota(jnp.int32, (c, LANES), 1)
    head0 = rl < hd
    incl = (rl % hd) <= ri
    strict = (rl % hd) < ri
    pi = lax.broadcasted_iota(jnp.int32, (2 * c, 2 * c), 0)
    pj = lax.broadcasted_iota(jnp.int32, (2 * c, 2 * c), 1)
    level = pi ^ pj
    diag = pi == pj
    eye = jnp.where(diag, 1.0, 0.0)
    n_levels = int(math.log2(c))

    def stack_heads(x):
        return jnp.concatenate([jnp.where(head0, x, 0.0), jnp.where(head0, 0.0, x)], axis=0)

    p1 = {k_: lax.dot_general(buf["lhs1"][idx(k_)], buf["rhs1"][idx(k_)], NT,
                              preferred_element_type=F32) for k_ in chains}
    a_rk = {k_: jnp.where(incl, p1[k_][0:c, 0:LANES], 0.0) for k_ in chains}
    a_rb = {k_: jnp.where(incl, p1[k_][0:c, LANES:2 * LANES], 0.0) for k_ in chains}
    a_kk = {k_: jnp.where(strict, p1[k_][c:2 * c, 0:LANES], 0.0) for k_ in chains}
    n_bd = {k_: stack_heads(jnp.where(strict, p1[k_][c:2 * c, LANES:2 * LANES], 0.0)) for k_ in chains}

    t_inv = {k_: eye - jnp.where(level == 1, n_bd[k_], 0.0) for k_ in chains}
    for lg in range(1, n_levels):
        nt = {k_: _dot(jnp.where((level >> lg) == 1, n_bd[k_], 0.0), t_inv[k_]) for k_ in chains}
        t_inv = {k_: t_inv[k_] - _dot(t_inv[k_], nt[k_]) for k_ in chains}

    av = {k_: _dot(stack_heads(a_kk[k_]), buf["v_bd"][idx(k_)]) for k_ in chains}
    w12 = {k_: _dot(t_inv[k_], jnp.concatenate([buf["kk_bd"][idx(k_)], av[k_].astype(BF16)], axis=1))
           for k_ in chains}
    w1 = {k_: w12[k_][:, 0:LANES].astype(BF16) for k_ in chains}
    vw = {k_: jnp.concatenate([buf["v_bd"][idx(k_)], (-w12[k_][:, LANES:2 * LANES]).astype(BF16)], axis=0)
          for k_ in chains}

    y_intra = {k_: _dot(jnp.concatenate([a_rk[k_], a_rb[k_]], axis=1), vw[k_]) for k_ in chains}
    q_eff = {k_: buf["q_t"][idx(k_)] - _dot(a_rb[k_], w1[k_]) for k_ in chains}
    g_mat = {k_: _dot(buf["kb_end"][idx(k_)], vw[k_], TN) for k_ in chains}
    m_mat = {}
    for k_ in chains:
        ls = slice(k_[1] * LANES, (k_[1] + 1) * LANES)
        d_end = jnp.broadcast_to(buf["d_end"][k_[0], 0:1, ls], (2 * c, LANES))
        m_mat[k_] = jnp.where(diag, d_end, 0.0) - _dot(buf["b_end"][idx(k_)], w1[k_], TN)

    state = [st_ref[p] for p in range(n_pairs)]
    y_rows = []
    for ic in range(n_chunks):
        y_c = [_dot(q_eff[(ic, p)], state[p]) + y_intra[(ic, p)] for p in range(n_pairs)]
        state = [_dot(m_mat[(ic, p)], state[p]) + g_mat[(ic, p)] for p in range(n_pairs)]
        y_rows.append(jnp.concatenate(y_c, axis=1))
    for p in range(n_pairs):
        st_ref[p] = state[p]

    y = jnp.concatenate(y_rows, axis=0)
    ones_bd = _head_ones(hd)
    mean = _seg_sum(y, ones_bd) * (1.0 / hd)
    yc = y - mean
    var = _seg_sum(yc * yc, ones_bd) * (1.0 / hd)
    yn = yc * lax.rsqrt(var + GN_EPS) * lnw_ref[...] + lnb_ref[...]
    o_ref[0] = ((yn + buf["bonus"][...]) * buf["gate"][...]).astype(o_ref.dtype)


def _rwkv_kernel(*refs, d_rw, lora_w, lora_g, n_buf_refs):
    z_ref = refs[0]
    prm = refs[1:10]
    lnw_ref, lnb_ref, o_ref, st_ref, carry_ref = refs[10:15]
    names = [n for n, _, _ in _RWKV_BUFS] + ["d_end", "bonus", "gate"]
    buf = dict(zip(names, refs[15:15 + n_buf_refs]))
    n_chunks = z_ref.shape[1] // RWKV_C

    @pl.when(pl.program_id(1) == 0)
    def _():
        st_ref[...] = jnp.zeros_like(st_ref)
        carry_ref[...] = jnp.zeros_like(carry_ref)

    _rwkv_prepare(z_ref, prm, carry_ref, buf, d_rw=d_rw, lora_w=lora_w, lora_g=lora_g)
    _rwkv_delta(buf, st_ref, lnw_ref, lnb_ref, o_ref, d_rw=d_rw, n_chunks=n_chunks)


def _rwkv(z, mu, w0, w2p, a0, a2p, g2p, k_k, k_a, r_k, ln_w, ln_b, d_rw, lora_w, lora_g):
    bsz, seqlen, _ = z.shape
    zw = 3 * d_rw + 2 * lora_w + lora_g
    lb = RWKV_LB
    c = RWKV_C
    n_blocks = seqlen // lb
    n_pairs = d_rw // LANES
    n_chains = (lb // c) * n_pairs
    const = lambda shape: pl.BlockSpec(shape, lambda b, j: (0,) * len(shape))
    params = (mu, w0, w2p, a0, a2p, g2p, k_k, k_a, r_k, ln_w, ln_b)
    bufs = [pltpu.VMEM((n_chains, rows * c, LANES), dt) for _, rows, dt in _RWKV_BUFS]
    bufs += [pltpu.VMEM((lb // c, SUBLANES, d_rw), F32),
             pltpu.VMEM((lb, d_rw), F32),
             pltpu.VMEM((lb, d_rw), F32)]
    return pl.pallas_call(
        functools.partial(_rwkv_kernel, d_rw=d_rw, lora_w=lora_w, lora_g=lora_g,
                          n_buf_refs=len(bufs)),
        grid=(bsz, n_blocks),
        in_specs=[pl.BlockSpec((1, lb, zw), lambda b, j: (b, j, 0))]
                 + [const(p.shape) for p in params],
        out_specs=pl.BlockSpec((1, lb, d_rw), lambda b, j: (b, j, 0)),
        out_shape=jax.ShapeDtypeStruct((bsz, seqlen, d_rw), BF16),
        scratch_shapes=[
            pltpu.VMEM((n_pairs, LANES, LANES), F32),
            pltpu.VMEM((SUBLANES, zw), F32),
        ] + bufs,
        compiler_params=pltpu.CompilerParams(
            dimension_semantics=("arbitrary", "arbitrary"),
            vmem_limit_bytes=VMEM_LIMIT),
        name="rwkv",
    )(z, *params)


def _out_ffn_kernel(x_ref, ys_ref, yr_ref, wo_ref, gf_ref, wu_ref, wd_ref, gl_ref, o_ref, *, final_norm):
    x = x_ref[0]
    mix = jnp.concatenate([ys_ref[0], yr_ref[0]], axis=1)
    h = x + jnp.dot(mix, wo_ref[...], preferred_element_type=F32)
    ms = jnp.mean(h * h, axis=-1, keepdims=True)
    hn = (h * lax.rsqrt(ms + NORM_EPS) * gf_ref[...]).astype(BF16)
    d_ff = wu_ref.shape[1]
    acc = h
    for j in range(d_ff // FFN_CHUNK):
        sl = slice(j * FFN_CHUNK, (j + 1) * FFN_CHUNK)
        up = jnp.dot(hn, wu_ref[:, sl], preferred_element_type=F32)
        up = jnp.square(jnp.maximum(up, 0.0)).astype(BF16)
        acc = acc + jnp.dot(up, wd_ref[sl, :], preferred_element_type=F32)
    if final_norm:
        ms2 = jnp.mean(acc * acc, axis=-1, keepdims=True)
        acc = acc * lax.rsqrt(ms2 + NORM_EPS) * gl_ref[...]
    o_ref[0] = acc


def _out_ffn(x, y_s5, y_rw, w_out, g_ffn, w_up, w_down, g_final, final_norm):
    bsz, seqlen, d_model = x.shape
    tm = FFN_TM
    const = lambda shape: pl.BlockSpec(shape, lambda b, i: (0,) * len(shape))
    act = lambda width: pl.BlockSpec((1, tm, width), lambda b, i: (b, i, 0))
    return pl.pallas_call(
        functools.partial(_out_ffn_kernel, final_norm=final_norm),
        grid=(bsz, seqlen // tm),
        in_specs=[
            act(d_model), act(y_s5.shape[2]), act(y_rw.shape[2]),
            const(w_out.shape), const(g_ffn.shape), const(w_up.shape),
            const(w_down.shape), const(g_final.shape),
        ],
        out_specs=act(d_model),
        out_shape=jax.ShapeDtypeStruct((bsz, seqlen, d_model), x.dtype),
        compiler_params=pltpu.CompilerParams(
            dimension_semantics=("arbitrary", "arbitrary"),
            vmem_limit_bytes=VMEM_LIMIT),
        name="out_ffn",
    )(x, y_s5, y_rw, w_out, g_ffn, w_up, w_down, g_final)


def _pad_cols(w, width):
    return jnp.pad(w, ((0, 0), (0, width - w.shape[1])))


def _pad_rows(w, height):
    return jnp.pad(w, ((0, height - w.shape[0]), (0, 0)))


def _s5_weights(lam_re, lam_im, log_dt, b_re, b_im, c_re, c_im):
    groups, n_p = lam_re.shape
    dt = jnp.exp(log_dt.astype(F32))[:, None]
    lr, li = lam_re.astype(F32), lam_im.astype(F32)
    mag = jnp.exp(lr * dt)
    a_re = mag * jnp.cos(li * dt)
    a_im = mag * jnp.sin(li * dt)
    den = lr * lr + li * li
    q_re = ((a_re - 1.0) * lr + a_im * li) / den
    q_im = (a_im * lr - (a_re - 1.0) * li) / den
    bb_re = q_re[:, :, None] * b_re - q_im[:, :, None] * b_im
    bb_im = q_re[:, :, None] * b_im + q_im[:, :, None] * b_re
    eye = jnp.eye(groups, dtype=F32)
    n_h = b_re.shape[2]

    def b_full(bb):
        return jnp.einsum('gph,gk->ghkp', bb, eye).reshape(groups * n_h, groups * n_p)

    def c_full(cc):
        return jnp.einsum('ghp,gk->gpkh', cc, eye).reshape(groups * n_p, groups * n_h)

    tile_w = 2 * LANES
    groups_per_tile = tile_w // n_p
    groups_per_kblock = LANES // n_h
    n_tiles = groups * n_p // tile_w

    def b_tiles(full):
        tiles = []
        for j in range(n_tiles):
            kb = (j * groups_per_tile) // groups_per_kblock
            tiles.append(full[kb * LANES:(kb + 1) * LANES, j * tile_w:(j + 1) * tile_w])
        return jnp.stack(tiles)

    wb = jnp.concatenate([b_tiles(b_full(bb_re)), b_tiles(b_full(bb_im))], axis=0).astype(BF16)

    kw = LANES // n_h * n_p
    n_out = groups * n_h // LANES

    def c_tiles(full):
        return jnp.stack([full[m * kw:(m + 1) * kw, m * LANES:(m + 1) * LANES] for m in range(n_out)])

    wc_re = c_tiles(c_full(c_re.astype(F32))).astype(BF16)
    wc_im = c_tiles(c_full(-c_im.astype(F32))).astype(BF16)
    return wb, a_re.reshape(1, -1), a_im.reshape(1, -1), wc_re, wc_im


def kernel(x, norm_mix_g, w_in, lam_re, lam_im, log_dt, b_re, b_im, c_re, c_im, d_skip, w_glu, b_glu, s5_out_g, mu_shift, w0, w2, a0, a2, g2, k_k, k_a, r_k, ln_x_w, ln_x_b, w_out, norm_ffn_g, w_up, w_down, norm_final_g):
    depth = w_in.shape[0]
    d_s5 = d_skip.shape[1]
    d_rw = w0.shape[1]
    n_w, n_a, n_g = w2.shape[1], a2.shape[1], g2.shape[1]
    lora_w = -(-max(n_w, n_a) // LANES) * LANES
    lora_g = -(-n_g // LANES) * LANES
    zw = 3 * d_rw + 2 * lora_w + lora_g
    assert zw % d_s5 == 0
    row = lambda p: p.reshape(1, -1).astype(F32)

    def rwkv_cols(m):
        o = 3 * d_rw
        return jnp.concatenate([
            m[:, :o],
            _pad_cols(m[:, o:o + n_w], lora_w),
            _pad_cols(m[:, o + n_w:o + n_w + n_a], lora_w),
            _pad_cols(m[:, o + n_w + n_a:], lora_g)], axis=1)

    h = x
    for i in range(depth):
        wi = w_in[i]
        w_cat = jnp.concatenate([rwkv_cols(wi[:, d_s5:]), wi[:, :d_s5]], axis=1).astype(BF16)
        mu_cat = rwkv_cols(mu_shift[i].reshape(1, -1)).astype(F32)

        z = _in_proj(h, row(norm_mix_g[i]), w_cat)

        wb, a_re, a_im, wc_re, wc_im = _s5_weights(
            lam_re[i], lam_im[i], log_dt[i], b_re[i], b_im[i], c_re[i], c_im[i])
        y_s5 = _s5(z, zw // d_s5, wb, a_re, a_im, wc_re, wc_im, row(d_skip[i]),
                   w_glu[i].astype(BF16), row(b_glu[i]), row(s5_out_g[i]))

        y_rw = _rwkv(z, mu_cat, row(w0[i]),
                     _pad_rows(w2[i], lora_w).astype(BF16), row(a0[i]),
                     _pad_rows(a2[i], lora_w).astype(BF16),
                     _pad_rows(g2[i], lora_g).astype(BF16),
                     row(k_k[i]), row(k_a[i]), row(r_k[i]), row(ln_x_w[i]), row(ln_x_b[i]),
                     d_rw, lora_w, lora_g)

        h = _out_ffn(h, y_s5, y_rw, w_out[i].astype(BF16), row(norm_ffn_g[i]),
                     w_up[i].astype(BF16), w_down[i].astype(BF16), row(norm_final_g),
                     final_norm=(i == depth - 1))
    return h
```

```python
import functools
import math

import jax
import jax.numpy as jnp
from jax import lax
from jax.experimental import pallas as pl
from jax.experimental.pallas import tpu as pltpu

F32 = jnp.float32
BF16 = jnp.bfloat16

NORM_EPS = 1e-6
GN_EPS = 64e-5

S5_GROUP = 16
S5_STATE = 64
RWKV_HEAD = 64
LANES = 128
SUBLANES = 8
VMEM_LIMIT = 56 * 1024 * 1024

IN_TM = 512
IN_TN = 512
S5_TC = 128
S5_STRIP = 512
RWKV_LB = 256
RWKV_C = 64
FFN_TM = 256
FFN_CHUNK = 1024

MM = (((1,), (0,)), ((), ()))
NT = (((1,), (1,)), ((), ()))
TN = (((0,), (0,)), ((), ()))


def _dot(a, b, dims=MM):
    return lax.dot_general(a.astype(BF16), b.astype(BF16), dims,
                           preferred_element_type=F32)


def _dot_split_rhs(a_bf16, b, passes):
    acc = None
    rem = b
    for _ in range(passes):
        part = rem.astype(BF16)
        term = jnp.dot(a_bf16, part, preferred_element_type=F32)
        acc = term if acc is None else acc + term
        rem = rem - part.astype(F32)
    return acc


def _in_proj_kernel(x_ref, g_ref, w_ref, z_ref):
    x = x_ref[0]
    ms = jnp.mean(x * x, axis=-1, keepdims=True)
    xn = (x * lax.rsqrt(ms + NORM_EPS) * g_ref[...]).astype(BF16)
    n = w_ref.shape[1]
    for j in range(n // IN_TN):
        sl = slice(j * IN_TN, (j + 1) * IN_TN)
        z_ref[0, :, sl] = jnp.dot(xn, w_ref[:, sl], preferred_element_type=F32).astype(BF16)


def _in_proj(x, g, w):
    bsz, seqlen, d_model = x.shape
    tm = IN_TM
    return pl.pallas_call(
        _in_proj_kernel,
        grid=(bsz, seqlen // tm),
        in_specs=[
            pl.BlockSpec((1, tm, d_model), lambda b, i: (b, i, 0)),
            pl.BlockSpec((1, d_model), lambda b, i: (0, 0)),
            pl.BlockSpec(w.shape, lambda b, i: (0, 0)),
        ],
        out_specs=pl.BlockSpec((1, tm, w.shape[1]), lambda b, i: (b, i, 0)),
        out_shape=jax.ShapeDtypeStruct((bsz, seqlen, w.shape[1]), BF16),
        compiler_params=pltpu.CompilerParams(
            dimension_semantics=("arbitrary", "arbitrary"),
            vmem_limit_bytes=VMEM_LIMIT),
        name="in_proj",
    )(x, g, w)


def _s5_kernel(u_ref, wb_ref, are_ref, aim_ref, wcre_ref, wcim_ref, dskip_ref,
               wglu_ref, bglu_ref, gout_ref, o_ref, st_ref, carry_ref, *, pitch):
    bsz, tc, d_s5 = u_ref.shape
    n_state = are_ref.shape[1]
    n_slabs_half = n_state // LANES
    pad = pitch - tc

    @pl.when(pl.program_id(0) == 0)
    def _():
        carry_ref[...] = jnp.zeros_like(carry_ref)

    zero_pad = jnp.zeros((pad, d_s5), BF16)
    u_bf = jnp.concatenate(
        [piece for b in range(bsz) for piece in (u_ref[b], zero_pad)], axis=0)

    n_tiles = wb_ref.shape[0]
    tile_w = wb_ref.shape[2]
    tiles_per_half = n_tiles // 2
    slabs_per_tile = tile_w // LANES
    for j in range(n_tiles):
        kb = (j % tiles_per_half) * tile_w // (S5_STATE // S5_GROUP * LANES)
        res = jnp.dot(u_bf[:, kb * LANES:(kb + 1) * LANES], wb_ref[j],
                      preferred_element_type=F32)
        for h in range(slabs_per_tile):
            st_ref[j * slabs_per_tile + h] = res[:, h * LANES:(h + 1) * LANES]

    slabs_per_strip = S5_STRIP // LANES
    for s in range(n_slabs_half // slabs_per_strip):
        re_ids = [s * slabs_per_strip + q for q in range(slabs_per_strip)]
        im_ids = [n_slabs_half + i for i in re_ids]
        a_re = [jnp.broadcast_to(are_ref[:, i * LANES:(i + 1) * LANES], (bsz, LANES)) for i in re_ids]
        a_im = [jnp.broadcast_to(aim_ref[:, i * LANES:(i + 1) * LANES], (bsz, LANES)) for i in re_ids]

        def step(t, carry, re_ids=re_ids, im_ids=im_ids, a_re=a_re, a_im=a_im):
            s_re, s_im = carry
            rows = pl.ds(t, bsz, stride=pitch)
            n_re, n_im = [], []
            for q in range(len(re_ids)):
                n_re.append(a_re[q] * s_re[q] - a_im[q] * s_im[q] + st_ref[re_ids[q], rows, :])
                n_im.append(a_re[q] * s_im[q] + a_im[q] * s_re[q] + st_ref[im_ids[q], rows, :])
            for q in range(len(re_ids)):
                st_ref[re_ids[q], rows, :] = n_re[q]
                st_ref[im_ids[q], rows, :] = n_im[q]
            return tuple(n_re), tuple(n_im)

        init = (tuple(carry_ref[i] for i in re_ids), tuple(carry_ref[i] for i in im_ids))
        s_re, s_im = lax.fori_loop(0, tc, step, init, unroll=4)
        for q in range(len(re_ids)):
            carry_ref[re_ids[q]] = s_re[q]
            carry_ref[im_ids[q]] = s_im[q]

    n_out_tiles = wcre_ref.shape[0]
    slabs_per_out = wcre_ref.shape[1] // LANES
    ys = []
    for m in range(n_out_tiles):
        ids = [m * slabs_per_out + q for q in range(slabs_per_out)]
        x_re = jnp.concatenate([st_ref[i] for i in ids], axis=1).astype(BF16)
        x_im = jnp.concatenate([st_ref[n_slabs_half + i] for i in ids], axis=1).astype(BF16)
        ys.append(jnp.dot(x_re, wcre_ref[m], preferred_element_type=F32)
                  + jnp.dot(x_im, wcim_ref[m], preferred_element_type=F32))
    y = jnp.concatenate(ys, axis=-1) + dskip_ref[...] * u_bf.astype(F32)
    g = 0.5 * y * (1.0 + lax.erf(y * (1.0 / math.sqrt(2.0))))
    gate = jax.nn.sigmoid(
        jnp.dot(g.astype(BF16), wglu_ref[...], preferred_element_type=F32) + bglu_ref[...])
    out = g * gate
    ms = jnp.mean(out * out, axis=-1, keepdims=True)
    out = (out * lax.rsqrt(ms + NORM_EPS) * gout_ref[...]).astype(o_ref.dtype)
    for b in range(bsz):
        o_ref[b] = out[b * pitch:b * pitch + tc]


def _s5(z, col_block, wb, a_re, a_im, wc_re, wc_im, d_skip, w_glu, b_glu, g_out):
    bsz, seqlen, _ = z.shape
    d_s5 = d_skip.shape[1]
    tc = S5_TC
    pitch = tc + SUBLANES
    n_state = a_re.shape[1]
    const = lambda shape: pl.BlockSpec(shape, lambda i: (0,) * len(shape))
    return pl.pallas_call(
        functools.partial(_s5_kernel, pitch=pitch),
        grid=(seqlen // tc,),
        in_specs=[
            pl.BlockSpec((bsz, tc, d_s5), lambda i: (0, i, col_block)),
            const(wb.shape), const(a_re.shape), const(a_im.shape),
            const(wc_re.shape), const(wc_im.shape), const(d_skip.shape),
            const(w_glu.shape), const(b_glu.shape), const(g_out.shape),
        ],
        out_specs=pl.BlockSpec((bsz, tc, d_s5), lambda i: (0, i, 0)),
        out_shape=jax.ShapeDtypeStruct((bsz, seqlen, d_s5), BF16),
        scratch_shapes=[
            pltpu.VMEM((2 * n_state // LANES, bsz * pitch, LANES), F32),
            pltpu.VMEM((2 * n_state // LANES, bsz, LANES), F32),
        ],
        compiler_params=pltpu.CompilerParams(
            dimension_semantics=("arbitrary",),
            vmem_limit_bytes=VMEM_LIMIT),
        name="s5",
    )(z, wb, a_re, a_im, wc_re, wc_im, d_skip, w_glu, b_glu, g_out)


_RWKV_BUFS = (("lhs1", 2, BF16), ("rhs1", 4, BF16), ("v_bd", 2, BF16), ("kk_bd", 2, BF16),
              ("kb_end", 4, BF16), ("b_end", 2, BF16), ("q_t", 1, F32))


def _seg_sum(x, ones_bd):
    w = ones_bd.shape[0]
    outs = [_dot(x[:, i * w:(i + 1) * w], ones_bd) for i in range(x.shape[1] // w)]
    return jnp.concatenate(outs, axis=-1)


def _head_ones(hd):
    seg_w = 2 * LANES
    li = lax.broadcasted_iota(jnp.int32, (seg_w, seg_w), 0) // hd
    lj = lax.broadcasted_iota(jnp.int32, (seg_w, seg_w), 1) // hd
    return jnp.where(li == lj, 1.0, 0.0).astype(BF16)


def _rwkv_prepare(z_ref, prm, carry_ref, buf, *, d_rw, lora_w, lora_g):
    (mu_ref, w0_ref, w2_ref, a0_ref, a2_ref, g2_ref, kk_ref, ka_ref, rk_ref) = prm
    lb = z_ref.shape[1]
    c = RWKV_C
    hd = RWKV_HEAD
    n_pairs = d_rw // LANES

    z = z_ref[0].astype(F32)
    row = lax.broadcasted_iota(jnp.int32, z.shape, 0)
    z_prev = jnp.where(row == 0, carry_ref[0:1, :], pltpu.roll(z, 1, 0))
    carry_ref[0:1, :] = z[lb - 1:lb, :]
    zz = z + (z_prev - z) * mu_ref[...]

    o = 3 * d_rw
    r = zz[:, 0:d_rw]
    k = zz[:, d_rw:2 * d_rw]
    v = zz[:, 2 * d_rw:o]
    zw = zz[:, o:o + lora_w]
    za = zz[:, o + lora_w:o + 2 * lora_w]
    zg = zz[:, o + 2 * lora_w:o + 2 * lora_w + lora_g]

    wpre = w0_ref[...] + _dot(jnp.tanh(zw), w2_ref[...])
    logd = -math.exp(-0.5) * jax.nn.sigmoid(wpre)
    a = jax.nn.sigmoid(a0_ref[...] + _dot(za, a2_ref[...]))
    buf["gate"][...] = _dot(jax.nn.sigmoid(zg), g2_ref[...])

    ones_bd = _head_ones(hd)
    kk = k * kk_ref[...]
    kk = kk * lax.rsqrt(jnp.maximum(_seg_sum(kk * kk, ones_bd), 1e-24))
    k2 = k * (1.0 + (a - 1.0) * ka_ref[...])
    buf["bonus"][...] = _seg_sum(r * k2 * rk_ref[...], ones_bd) * v
    bvec = kk * a

    ci = lax.broadcasted_iota(jnp.int32, (c, c), 0)
    cj = lax.broadcasted_iota(jnp.int32, (c, c), 1)
    ltri = jnp.where(cj <= ci, 1.0, 0.0).astype(BF16)
    head0 = lax.broadcasted_iota(jnp.int32, (c, LANES), 1) < hd

    def stack_heads(x):
        return jnp.concatenate([jnp.where(head0, x, 0.0), jnp.where(head0, 0.0, x)], axis=0)

    for ic in range(lb // c):
        rows = slice(ic * c, (ic + 1) * c)
        logd_c = logd[rows]
        cum = _dot_split_rhs(ltri, logd_c, 3)
        cum_end = cum[c - 1:c, :]
        e_inv = jnp.exp(-cum)
        e_end = jnp.exp(cum_end - cum)
        q_c = r[rows] * jnp.exp(cum)
        kk_c = kk[rows] * jnp.exp(cum - logd_c)
        k_hat = k2[rows] * e_inv
        b_hat = bvec[rows] * e_inv
        k_end = k2[rows] * e_end
        b_end = bvec[rows] * e_end
        buf["d_end"][ic] = jnp.broadcast_to(jnp.exp(cum_end), (SUBLANES, d_rw))
        for p in range(n_pairs):
            ls = slice(p * LANES, (p + 1) * LANES)
            i = ic * n_pairs + p
            b_end_bd = stack_heads(b_end[:, ls])
            buf["q_t"][i] = q_c[:, ls]
            buf["kk_bd"][i] = stack_heads(kk_c[:, ls]).astype(BF16)
            buf["v_bd"][i] = stack_heads(v[rows, ls]).astype(BF16)
            buf["b_end"][i] = b_end_bd.astype(BF16)
            buf["lhs1"][i] = jnp.concatenate([q_c[:, ls], kk_c[:, ls]], axis=0).astype(BF16)
            buf["rhs1"][i] = jnp.concatenate(
                [stack_heads(k_hat[:, ls]), stack_heads(b_hat[:, ls])], axis=0).astype(BF16)
            buf["kb_end"][i] = jnp.concatenate(
                [stack_heads(k_end[:, ls]), b_end_bd], axis=0).astype(BF16)


def _rwkv_delta(buf, st_ref, lnw_ref, lnb_ref, o_ref, *, d_rw, n_chunks):
    c = RWKV_C
    hd = RWKV_HEAD
    n_pairs = d_rw // LANES
    chains = [(ic, p) for ic in range(n_chunks) for p in range(n_pairs)]
    idx = lambda k_: k_[0] * n_pairs + k_[1]

    ri = lax.broadcasted_iota(jnp.int32, (c, LANES), 0)
    rl = lax.broadcasted_iota(jnp.int32, (c, LANES), 1)
    head0 = rl < hd
    incl = (rl % hd) <= ri
    strict = (rl % hd) < ri
    pi = lax.broadcasted_iota(jnp.int32, (2 * c, 2 * c), 0)
    pj = lax.broadcasted_iota(jnp.int32, (2 * c, 2 * c), 1)
    level = pi ^ pj
    diag = pi == pj
    eye = jnp.where(diag, 1.0, 0.0)
    n_levels = int(math.log2(c))

    def stack_heads(x):
        return jnp.concatenate([jnp.where(head0, x, 0.0), jnp.where(head0, 0.0, x)], axis=0)

    p1 = {k_: lax.dot_general(buf["lhs1"][idx(k_)], buf["rhs1"][idx(k_)], NT,
                              preferred_element_type=F32) for k_ in chains}
    a_rk = {k_: jnp.where(incl, p1[k_][0:c, 0:LANES], 0.0).astype(BF16) for k_ in chains}
    a_rb = {k_: jnp.where(incl, p1[k_][0:c, LANES:2 * LANES], 0.0).astype(BF16) for k_ in chains}
    a_kk = {k_: stack_heads(jnp.where(strict, p1[k_][c:2 * c, 0:LANES], 0.0)).astype(BF16)
            for k_ in chains}
    n_bd = {k_: stack_heads(jnp.where(strict, p1[k_][c:2 * c, LANES:2 * LANES], 0.0)) for k_ in chains}

    t_inv = {k_: eye - jnp.where(level == 1, n_bd[k_], 0.0) for k_ in chains}
    n_bd = {k_: n_bd[k_].astype(BF16) for k_ in chains}
    zero_bf = jnp.zeros((), BF16)
    for lg in range(1, n_levels):
        t_bf = {k_: t_inv[k_].astype(BF16) for k_ in chains}
        nt = {k_: _dot(jnp.where((level >> lg) == 1, n_bd[k_], zero_bf), t_bf[k_]).astype(BF16)
              for k_ in chains}
        t_inv = {k_: t_inv[k_] - _dot(t_bf[k_], nt[k_]) for k_ in chains}
    t_bf = {k_: t_inv[k_].astype(BF16) for k_ in chains}

    av = {k_: _dot(a_kk[k_], buf["v_bd"][idx(k_)]).astype(BF16) for k_ in chains}
    w12 = {k_: _dot(t_bf[k_], jnp.concatenate([buf["kk_bd"][idx(k_)], av[k_]], axis=1))
           for k_ in chains}
    w1 = {k_: w12[k_][:, 0:LANES].astype(BF16) for k_ in chains}
    vw = {k_: jnp.concatenate([buf["v_bd"][idx(k_)], (-w12[k_][:, LANES:2 * LANES]).astype(BF16)], axis=0)
          for k_ in chains}

    y_intra = {k_: _dot(jnp.concatenate([a_rk[k_], a_rb[k_]], axis=1), vw[k_]) for k_ in chains}
    q_eff = {k_: (buf["q_t"][idx(k_)] - _dot(a_rb[k_], w1[k_])).astype(BF16) for k_ in chains}
    g_mat = {k_: _dot(buf["kb_end"][idx(k_)], vw[k_], TN) for k_ in chains}
    m_mat = {}
    for k_ in chains:
        ls = slice(k_[1] * LANES, (k_[1] + 1) * LANES)
        d_end = jnp.broadcast_to(buf["d_end"][k_[0]][0:1, ls], (2 * c, LANES))
        m_mat[k_] = (jnp.where(diag, d_end, 0.0)
                     - _dot(buf["b_end"][idx(k_)], w1[k_], TN)).astype(BF16)

    state = [st_ref[p] for p in range(n_pairs)]
    y_rows = []
    for ic in range(n_chunks):
        y_c = [_dot(q_eff[(ic, p)], state[p]) + y_intra[(ic, p)] for p in range(n_pairs)]
        state = [_dot(m_mat[(ic, p)], state[p]) + g_mat[(ic, p)] for p in range(n_pairs)]
        y_rows.append(jnp.concatenate(y_c, axis=1))
    for p in range(n_pairs):
        st_ref[p] = state[p]

    y = jnp.concatenate(y_rows, axis=0)
    ones_bd = _head_ones(hd)
    mean = _seg_sum(y, ones_bd) * (1.0 / hd)
    yc = y - mean
    var = _seg_sum(yc * yc, ones_bd) * (1.0 / hd)
    yn = yc * lax.rsqrt(var + GN_EPS) * lnw_ref[...] + lnb_ref[...]
    o_ref[0] = ((yn + buf["bonus"][...]) * buf["gate"][...]).astype(o_ref.dtype)


def _rwkv_kernel(*refs, d_rw, lora_w, lora_g, n_buf_refs):
    z_ref = refs[0]
    prm = refs[1:10]
    lnw_ref, lnb_ref, o_ref, st_ref, carry_ref = refs[10:15]
    names = [n for n, _, _ in _RWKV_BUFS] + ["d_end", "bonus", "gate"]
    buf = {name: {} for name in names} if n_buf_refs == 0 else dict(zip(names, refs[15:15 + n_buf_refs]))
    n_chunks = z_ref.shape[1] // RWKV_C

    @pl.when(pl.program_id(1) == 0)
    def _():
        st_ref[...] = jnp.zeros_like(st_ref)
        carry_ref[...] = jnp.zeros_like(carry_ref)

    _rwkv_prepare(z_ref, prm, carry_ref, buf, d_rw=d_rw, lora_w=lora_w, lora_g=lora_g)
    _rwkv_delta(buf, st_ref, lnw_ref, lnb_ref, o_ref, d_rw=d_rw, n_chunks=n_chunks)


def _rwkv(z, mu, w0, w2p, a0, a2p, g2p, k_k, k_a, r_k, ln_w, ln_b, d_rw, lora_w, lora_g):
    bsz, seqlen, _ = z.shape
    zw = 3 * d_rw + 2 * lora_w + lora_g
    lb = RWKV_LB
    c = RWKV_C
    n_blocks = seqlen // lb
    n_pairs = d_rw // LANES
    n_chains = (lb // c) * n_pairs
    const = lambda shape: pl.BlockSpec(shape, lambda b, j: (0,) * len(shape))
    params = (mu, w0, w2p, a0, a2p, g2p, k_k, k_a, r_k, ln_w, ln_b)
    bufs = [pltpu.VMEM((n_chains, rows * c, LANES), dt) for _, rows, dt in _RWKV_BUFS]
    bufs += [pltpu.VMEM((lb // c, SUBLANES, d_rw), F32),
             pltpu.VMEM((lb, d_rw), F32),
             pltpu.VMEM((lb, d_rw), F32)]
    bufs = []
    return pl.pallas_call(
        functools.partial(_rwkv_kernel, d_rw=d_rw, lora_w=lora_w, lora_g=lora_g,
                          n_buf_refs=len(bufs)),
        grid=(bsz, n_blocks),
        in_specs=[pl.BlockSpec((1, lb, zw), lambda b, j: (b, j, 0))]
                 + [const(p.shape) for p in params],
        out_specs=pl.BlockSpec((1, lb, d_rw), lambda b, j: (b, j, 0)),
        out_shape=jax.ShapeDtypeStruct((bsz, seqlen, d_rw), BF16),
        scratch_shapes=[
            pltpu.VMEM((n_pairs, LANES, LANES), F32),
            pltpu.VMEM((SUBLANES, zw), F32),
        ] + bufs,
        compiler_params=pltpu.CompilerParams(
            dimension_semantics=("arbitrary", "arbitrary"),
            vmem_limit_bytes=VMEM_LIMIT),
        name="rwkv",
    )(z, *params)


def _out_ffn_kernel(x_ref, ys_ref, yr_ref, wo_ref, gf_ref, wu_ref, wd_ref, gl_ref, o_ref, *, final_norm):
    x = x_ref[0]
    mix = jnp.concatenate([ys_ref[0], yr_ref[0]], axis=1)
    h = x + jnp.dot(mix, wo_ref[...], preferred_element_type=F32)
    ms = jnp.mean(h * h, axis=-1, keepdims=True)
    hn = (h * lax.rsqrt(ms + NORM_EPS) * gf_ref[...]).astype(BF16)
    d_ff = wu_ref.shape[1]
    acc = h
    for j in range(d_ff // FFN_CHUNK):
        sl = slice(j * FFN_CHUNK, (j + 1) * FFN_CHUNK)
        up = jnp.dot(hn, wu_ref[:, sl], preferred_element_type=F32)
        up = jnp.square(jnp.maximum(up, 0.0)).astype(BF16)
        acc = acc + jnp.dot(up, wd_ref[sl, :], preferred_element_type=F32)
    if final_norm:
        ms2 = jnp.mean(acc * acc, axis=-1, keepdims=True)
        acc = acc * lax.rsqrt(ms2 + NORM_EPS) * gl_ref[...]
    o_ref[0] = acc


def _out_ffn(x, y_s5, y_rw, w_out, g_ffn, w_up, w_down, g_final, final_norm):
    bsz, seqlen, d_model = x.shape
    tm = FFN_TM
    const = lambda shape: pl.BlockSpec(shape, lambda b, i: (0,) * len(shape))
    act = lambda width: pl.BlockSpec((1, tm, width), lambda b, i: (b, i, 0))
    return pl.pallas_call(
        functools.partial(_out_ffn_kernel, final_norm=final_norm),
        grid=(bsz, seqlen // tm),
        in_specs=[
            act(d_model), act(y_s5.shape[2]), act(y_rw.shape[2]),
            const(w_out.shape), const(g_ffn.shape), const(w_up.shape),
            const(w_down.shape), const(g_final.shape),
        ],
        out_specs=act(d_model),
        out_shape=jax.ShapeDtypeStruct((bsz, seqlen, d_model), x.dtype),
        compiler_params=pltpu.CompilerParams(
            dimension_semantics=("arbitrary", "arbitrary"),
            vmem_limit_bytes=VMEM_LIMIT),
        name="out_ffn",
    )(x, y_s5, y_rw, w_out, g_ffn, w_up, w_down, g_final)


def _pad_cols(w, width):
    return jnp.pad(w, ((0, 0), (0, width - w.shape[1])))


def _pad_rows(w, height):
    return jnp.pad(w, ((0, height - w.shape[0]), (0, 0)))


def _s5_weights(lam_re, lam_im, log_dt, b_re, b_im, c_re, c_im):
    groups, n_p = lam_re.shape
    dt = jnp.exp(log_dt.astype(F32))[:, None]
    lr, li = lam_re.astype(F32), lam_im.astype(F32)
    mag = jnp.exp(lr * dt)
    a_re = mag * jnp.cos(li * dt)
    a_im = mag * jnp.sin(li * dt)
    den = lr * lr + li * li
    q_re = ((a_re - 1.0) * lr + a_im * li) / den
    q_im = (a_im * lr - (a_re - 1.0) * li) / den
    bb_re = q_re[:, :, None] * b_re - q_im[:, :, None] * b_im
    bb_im = q_re[:, :, None] * b_im + q_im[:, :, None] * b_re
    eye = jnp.eye(groups, dtype=F32)
    n_h = b_re.shape[2]

    def b_full(bb):
        return jnp.einsum('gph,gk->ghkp', bb, eye).reshape(groups * n_h, groups * n_p)

    def c_full(cc):
        return jnp.einsum('ghp,gk->gpkh', cc, eye).reshape(groups * n_p, groups * n_h)

    tile_w = 2 * LANES
    groups_per_tile = tile_w // n_p
    groups_per_kblock = LANES // n_h
    n_tiles = groups * n_p // tile_w

    def b_tiles(full):
        tiles = []
        for j in range(n_tiles):
            kb = (j * groups_per_tile) // groups_per_kblock
            tiles.append(full[kb * LANES:(kb + 1) * LANES, j * tile_w:(j + 1) * tile_w])
        return jnp.stack(tiles)

    wb = jnp.concatenate([b_tiles(b_full(bb_re)), b_tiles(b_full(bb_im))], axis=0).astype(BF16)

    kw = LANES // n_h * n_p
    n_out = groups * n_h // LANES

    def c_tiles(full):
        return jnp.stack([full[m * kw:(m + 1) * kw, m * LANES:(m + 1) * LANES] for m in range(n_out)])

    wc_re = c_tiles(c_full(c_re.astype(F32))).astype(BF16)
    wc_im = c_tiles(c_full(-c_im.astype(F32))).astype(BF16)
    return wb, a_re.reshape(1, -1), a_im.reshape(1, -1), wc_re, wc_im


def kernel(x, norm_mix_g, w_in, lam_re, lam_im, log_dt, b_re, b_im, c_re, c_im, d_skip, w_glu, b_glu, s5_out_g, mu_shift, w0, w2, a0, a2, g2, k_k, k_a, r_k, ln_x_w, ln_x_b, w_out, norm_ffn_g, w_up, w_down, norm_final_g):
    depth = w_in.shape[0]
    d_s5 = d_skip.shape[1]
    d_rw = w0.shape[1]
    n_w, n_a, n_g = w2.shape[1], a2.shape[1], g2.shape[1]
    lora_w = -(-max(n_w, n_a) // LANES) * LANES
    lora_g = -(-n_g // LANES) * LANES
    zw = 3 * d_rw + 2 * lora_w + lora_g
    assert zw % d_s5 == 0
    row = lambda p: p.reshape(1, -1).astype(F32)

    def rwkv_cols(m):
        o = 3 * d_rw
        return jnp.concatenate([
            m[:, :o],
            _pad_cols(m[:, o:o + n_w], lora_w),
            _pad_cols(m[:, o + n_w:o + n_w + n_a], lora_w),
            _pad_cols(m[:, o + n_w + n_a:], lora_g)], axis=1)

    h = x
    for i in range(depth):
        wi = w_in[i]
        w_cat = jnp.concatenate([rwkv_cols(wi[:, d_s5:]), wi[:, :d_s5]], axis=1).astype(BF16)
        mu_cat = rwkv_cols(mu_shift[i].reshape(1, -1)).astype(F32)

        z = _in_proj(h, row(norm_mix_g[i]), w_cat)

        wb, a_re, a_im, wc_re, wc_im = _s5_weights(
            lam_re[i], lam_im[i], log_dt[i], b_re[i], b_im[i], c_re[i], c_im[i])
        y_s5 = _s5(z, zw // d_s5, wb, a_re, a_im, wc_re, wc_im, row(d_skip[i]),
                   w_glu[i].astype(BF16), row(b_glu[i]), row(s5_out_g[i]))

        y_rw = _rwkv(z, mu_cat, row(w0[i]),
                     _pad_rows(w2[i], lora_w).astype(BF16), row(a0[i]),
                     _pad_rows(a2[i], lora_w).astype(BF16),
                     _pad_rows(g2[i], lora_g).astype(BF16),
                     row(k_k[i]), row(k_a[i]), row(r_k[i]), row(ln_x_w[i]), row(ln_x_b[i]),
                     d_rw, lora_w, lora_g)

        h = _out_ffn(h, y_s5, y_rw, w_out[i].astype(BF16), row(norm_ffn_g[i]),
                     w_up[i].astype(BF16), w_down[i].astype(BF16), row(norm_final_g),
                     final_norm=(i == depth - 1))
    return h
```

```python
import functools
import math

import numpy as np
import jax
import jax.numpy as jnp
from jax import lax
from jax.experimental import pallas as pl
from jax.experimental.pallas import tpu as pltpu

F32 = jnp.float32
BF16 = jnp.bfloat16

NORM_EPS = 1e-6
GN_EPS = 64e-5

S5_GROUP = 16
S5_STATE = 64
RWKV_HEAD = 64
LANES = 128
SUBLANES = 8
VMEM_LIMIT = 56 * 1024 * 1024

IN_TM = 512
IN_TN = 512
S5_TC = 128
S5_STRIP = 512
RWKV_LB = 256
RWKV_C = 64
FFN_TM = 256
FFN_CHUNK = 1024

MM = (((1,), (0,)), ((), ()))
NT = (((1,), (1,)), ((), ()))
TN = (((0,), (0,)), ((), ()))


def _dot(a, b, dims=MM):
    return lax.dot_general(a.astype(BF16), b.astype(BF16), dims,
                           preferred_element_type=F32)


def _dot_split_rhs(a_bf16, b, passes):
    acc = None
    rem = b
    for _ in range(passes):
        part = rem.astype(BF16)
        term = jnp.dot(a_bf16, part, preferred_element_type=F32)
        acc = term if acc is None else acc + term
        rem = rem - part.astype(F32)
    return acc


def _in_proj_kernel(x_ref, g_ref, w_ref, z_ref):
    x = x_ref[0]
    ms = jnp.mean(x * x, axis=-1, keepdims=True)
    xn = (x * lax.rsqrt(ms + NORM_EPS) * g_ref[...]).astype(BF16)
    n = w_ref.shape[1]
    for j in range(n // IN_TN):
        sl = slice(j * IN_TN, (j + 1) * IN_TN)
        z_ref[0, :, sl] = jnp.dot(xn, w_ref[:, sl], preferred_element_type=F32).astype(BF16)


def _in_proj(x, g, w):
    bsz, seqlen, d_model = x.shape
    tm = IN_TM
    return pl.pallas_call(
        _in_proj_kernel,
        grid=(bsz, seqlen // tm),
        in_specs=[
            pl.BlockSpec((1, tm, d_model), lambda b, i: (b, i, 0)),
            pl.BlockSpec((1, d_model), lambda b, i: (0, 0)),
            pl.BlockSpec(w.shape, lambda b, i: (0, 0)),
        ],
        out_specs=pl.BlockSpec((1, tm, w.shape[1]), lambda b, i: (b, i, 0)),
        out_shape=jax.ShapeDtypeStruct((bsz, seqlen, w.shape[1]), BF16),
        compiler_params=pltpu.CompilerParams(
            dimension_semantics=("arbitrary", "arbitrary"),
            vmem_limit_bytes=VMEM_LIMIT),
        name="in_proj",
    )(x, g, w)


def _s5_kernel(u_ref, wb_ref, lam_ref, wcre_ref, wcim_ref, vec_ref, wglu_ref, o_ref,
               st_ref, carry_ref, *, pitch):
    bsz, tc, d_s5 = u_ref.shape
    n_state = lam_ref.shape[1]
    n_slabs_half = n_state // LANES
    pad = pitch - tc
    d_skip, b_glu, g_out = vec_ref[0:1, :], vec_ref[1:2, :], vec_ref[2:3, :]

    @pl.when(pl.program_id(0) == 0)
    def _():
        carry_ref[...] = jnp.zeros_like(carry_ref)

    zero_pad = jnp.zeros((pad, d_s5), BF16)
    u_bf = jnp.concatenate(
        [piece for b in range(bsz) for piece in (u_ref[b], zero_pad)], axis=0)

    n_tiles = wb_ref.shape[0]
    tile_w = wb_ref.shape[2]
    tiles_per_half = n_tiles // 2
    slabs_per_tile = tile_w // LANES
    for j in range(n_tiles):
        kb = (j % tiles_per_half) * tile_w // (S5_STATE // S5_GROUP * LANES)
        res = jnp.dot(u_bf[:, kb * LANES:(kb + 1) * LANES], wb_ref[j],
                      preferred_element_type=F32)
        for h in range(slabs_per_tile):
            st_ref[j * slabs_per_tile + h] = res[:, h * LANES:(h + 1) * LANES]

    slabs_per_strip = S5_STRIP // LANES
    for s in range(n_slabs_half // slabs_per_strip):
        re_ids = [s * slabs_per_strip + q for q in range(slabs_per_strip)]
        im_ids = [n_slabs_half + i for i in re_ids]
        a_re = [jnp.broadcast_to(lam_ref[0:1, i * LANES:(i + 1) * LANES], (bsz, LANES)) for i in re_ids]
        a_im = [jnp.broadcast_to(lam_ref[1:2, i * LANES:(i + 1) * LANES], (bsz, LANES)) for i in re_ids]

        def step(t, carry, re_ids=re_ids, im_ids=im_ids, a_re=a_re, a_im=a_im):
            s_re, s_im = carry
            rows = pl.ds(t, bsz, stride=pitch)
            n_re, n_im = [], []
            for q in range(len(re_ids)):
                n_re.append(a_re[q] * s_re[q] - a_im[q] * s_im[q] + st_ref[re_ids[q], rows, :])
                n_im.append(a_re[q] * s_im[q] + a_im[q] * s_re[q] + st_ref[im_ids[q], rows, :])
            for q in range(len(re_ids)):
                st_ref[re_ids[q], rows, :] = n_re[q]
                st_ref[im_ids[q], rows, :] = n_im[q]
            return tuple(n_re), tuple(n_im)

        init = (tuple(carry_ref[i] for i in re_ids), tuple(carry_ref[i] for i in im_ids))
        s_re, s_im = lax.fori_loop(0, tc, step, init, unroll=4)
        for q in range(len(re_ids)):
            carry_ref[re_ids[q]] = s_re[q]
            carry_ref[im_ids[q]] = s_im[q]

    n_out_tiles = wcre_ref.shape[0]
    slabs_per_out = wcre_ref.shape[1] // LANES
    ys = []
    for m in range(n_out_tiles):
        ids = [m * slabs_per_out + q for q in range(slabs_per_out)]
        x_re = jnp.concatenate([st_ref[i] for i in ids], axis=1).astype(BF16)
        x_im = jnp.concatenate([st_ref[n_slabs_half + i] for i in ids], axis=1).astype(BF16)
        ys.append(jnp.dot(x_re, wcre_ref[m], preferred_element_type=F32)
                  + jnp.dot(x_im, wcim_ref[m], preferred_element_type=F32))
    y = jnp.concatenate(ys, axis=-1) + d_skip * u_bf.astype(F32)
    g = 0.5 * y * (1.0 + lax.erf(y * (1.0 / math.sqrt(2.0))))
    gate = jax.nn.sigmoid(
        jnp.dot(g.astype(BF16), wglu_ref[...], preferred_element_type=F32) + b_glu)
    out = g * gate
    ms = jnp.mean(out * out, axis=-1, keepdims=True)
    out = (out * lax.rsqrt(ms + NORM_EPS) * g_out).astype(o_ref.dtype)
    for b in range(bsz):
        o_ref[b] = out[b * pitch:b * pitch + tc]


def _s5(z, col_block, wb, lam_bar, wc_re, wc_im, vecs, w_glu):
    bsz, seqlen, _ = z.shape
    d_s5 = vecs.shape[1]
    tc = S5_TC
    pitch = tc + SUBLANES
    n_state = lam_bar.shape[1]
    const = lambda shape: pl.BlockSpec(shape, lambda i: (0,) * len(shape))
    return pl.pallas_call(
        functools.partial(_s5_kernel, pitch=pitch),
        grid=(seqlen // tc,),
        in_specs=[
            pl.BlockSpec((bsz, tc, d_s5), lambda i: (0, i, col_block)),
            const(wb.shape), const(lam_bar.shape), const(wc_re.shape), const(wc_im.shape),
            const(vecs.shape), const(w_glu.shape),
        ],
        out_specs=pl.BlockSpec((bsz, tc, d_s5), lambda i: (0, i, 0)),
        out_shape=jax.ShapeDtypeStruct((bsz, seqlen, d_s5), BF16),
        scratch_shapes=[
            pltpu.VMEM((2 * n_state // LANES, bsz * pitch, LANES), F32),
            pltpu.VMEM((2 * n_state // LANES, bsz, LANES), F32),
        ],
        compiler_params=pltpu.CompilerParams(
            dimension_semantics=("arbitrary",),
            vmem_limit_bytes=VMEM_LIMIT),
        name="s5",
    )(z, wb, lam_bar, wc_re, wc_im, vecs, w_glu)


_W0, _A0, _KK, _KA, _RK, _LNW, _LNB = range(7)


def _seg_sum(x, ones_bd):
    w = ones_bd.shape[0]
    outs = [_dot(x[:, i * w:(i + 1) * w], ones_bd) for i in range(x.shape[1] // w)]
    return jnp.concatenate(outs, axis=-1)


def _head_ones(hd):
    seg_w = 2 * LANES
    li = lax.broadcasted_iota(jnp.int32, (seg_w, seg_w), 0) // hd
    lj = lax.broadcasted_iota(jnp.int32, (seg_w, seg_w), 1) // hd
    return jnp.where(li == lj, 1.0, 0.0).astype(BF16)


def _rwkv_prepare(z_ref, mu_ref, vec_ref, w2_ref, a2_ref, g2_ref, carry_ref, *, d_rw, lora_wa, lora_g):
    lb = z_ref.shape[1]
    c = RWKV_C
    hd = RWKV_HEAD
    n_pairs = d_rw // LANES
    vec = lambda i: vec_ref[i:i + 1, :]

    z = z_ref[0].astype(F32)
    row = lax.broadcasted_iota(jnp.int32, z.shape, 0)
    z_prev = jnp.where(row == 0, carry_ref[0:1, :], pltpu.roll(z, 1, 0))
    carry_ref[0:1, :] = z[lb - 1:lb, :]
    zz = z + (z_prev - z) * mu_ref[...]

    o = 3 * d_rw
    r = zz[:, 0:d_rw]
    k = zz[:, d_rw:2 * d_rw]
    v = zz[:, 2 * d_rw:o]
    zwa = zz[:, o:o + lora_wa]
    zg = zz[:, o + lora_wa:o + lora_wa + lora_g]

    wpre = vec(_W0) + _dot(jnp.tanh(zwa), w2_ref[...])
    logd = -math.exp(-0.5) * jax.nn.sigmoid(wpre)
    a = jax.nn.sigmoid(vec(_A0) + _dot(zwa, a2_ref[...]))
    buf = {name: {} for name in ("lhs1", "rhs1", "v_bd", "kk_bd", "kb_end", "b_end", "q_t", "d_end")}
    buf["gate"] = _dot(jax.nn.sigmoid(zg), g2_ref[...])

    ones_bd = _head_ones(hd)
    kk = k * vec(_KK)
    kk = kk * lax.rsqrt(jnp.maximum(_seg_sum(kk * kk, ones_bd), 1e-24))
    k2 = k * (1.0 + (a - 1.0) * vec(_KA))
    buf["bonus"] = _seg_sum(r * k2 * vec(_RK), ones_bd) * v
    bvec = kk * a

    ci = lax.broadcasted_iota(jnp.int32, (c, c), 0)
    cj = lax.broadcasted_iota(jnp.int32, (c, c), 1)
    ltri = jnp.where(cj <= ci, 1.0, 0.0).astype(BF16)
    head0 = lax.broadcasted_iota(jnp.int32, (c, LANES), 1) < hd

    def stack_heads(x):
        return jnp.concatenate([jnp.where(head0, x, 0.0), jnp.where(head0, 0.0, x)], axis=0)

    for ic in range(lb // c):
        rows = slice(ic * c, (ic + 1) * c)
        logd_c = logd[rows]
        cum = _dot_split_rhs(ltri, logd_c, 3)
        cum_end = cum[c - 1:c, :]
        e_inv = jnp.exp(-cum)
        e_end = jnp.exp(cum_end - cum)
        q_c = r[rows] * jnp.exp(cum)
        kk_c = kk[rows] * jnp.exp(cum - logd_c)
        k_hat = k2[rows] * e_inv
        b_hat = bvec[rows] * e_inv
        k_end = k2[rows] * e_end
        b_end = bvec[rows] * e_end
        d_c = jnp.exp(cum_end)
        for p in range(n_pairs):
            ls = slice(p * LANES, (p + 1) * LANES)
            key = (ic, p)
            b_end_bd = stack_heads(b_end[:, ls])
            buf["q_t"][key] = q_c[:, ls]
            buf["d_end"][key] = d_c[:, ls]
            buf["kk_bd"][key] = stack_heads(kk_c[:, ls]).astype(BF16)
            buf["v_bd"][key] = stack_heads(v[rows, ls]).astype(BF16)
            buf["b_end"][key] = b_end_bd.astype(BF16)
            buf["lhs1"][key] = jnp.concatenate([q_c[:, ls], kk_c[:, ls]], axis=0).astype(BF16)
            buf["rhs1"][key] = jnp.concatenate(
                [stack_heads(k_hat[:, ls]), stack_heads(b_hat[:, ls])], axis=0).astype(BF16)
            buf["kb_end"][key] = jnp.concatenate(
                [stack_heads(k_end[:, ls]), b_end_bd], axis=0).astype(BF16)
    return buf


def _rwkv_delta(buf, st_ref, vec_ref, o_ref, *, d_rw, n_chunks):
    c = RWKV_C
    hd = RWKV_HEAD
    n_pairs = d_rw // LANES
    chains = [(ic, p) for ic in range(n_chunks) for p in range(n_pairs)]

    ri = lax.broadcasted_iota(jnp.int32, (c, LANES), 0)
    rl = lax.broadcasted_iota(jnp.int32, (c, LANES), 1)
    head0 = rl < hd
    incl = (rl % hd) <= ri
    strict = (rl % hd) < ri
    pi = lax.broadcasted_iota(jnp.int32, (2 * c, 2 * c), 0)
    pj = lax.broadcasted_iota(jnp.int32, (2 * c, 2 * c), 1)
    level = pi ^ pj
    diag = pi == pj
    eye = jnp.where(diag, 1.0, 0.0)
    n_levels = int(math.log2(c))

    def stack_heads(x):
        return jnp.concatenate([jnp.where(head0, x, 0.0), jnp.where(head0, 0.0, x)], axis=0)

    p1 = {k_: lax.dot_general(buf["lhs1"][k_], buf["rhs1"][k_], NT,
                              preferred_element_type=F32) for k_ in chains}
    a_rk = {k_: jnp.where(incl, p1[k_][0:c, 0:LANES], 0.0).astype(BF16) for k_ in chains}
    a_rb = {k_: jnp.where(incl, p1[k_][0:c, LANES:2 * LANES], 0.0).astype(BF16) for k_ in chains}
    a_kk = {k_: stack_heads(jnp.where(strict, p1[k_][c:2 * c, 0:LANES], 0.0)).astype(BF16)
            for k_ in chains}
    n_bd = {k_: stack_heads(jnp.where(strict, p1[k_][c:2 * c, LANES:2 * LANES], 0.0)) for k_ in chains}

    t_inv = {k_: eye - jnp.where(level == 1, n_bd[k_], 0.0) for k_ in chains}
    n_bd = {k_: n_bd[k_].astype(BF16) for k_ in chains}
    zero_bf = jnp.zeros((), BF16)
    for lg in range(1, n_levels):
        t_bf = {k_: t_inv[k_].astype(BF16) for k_ in chains}
        nt = {k_: _dot(jnp.where((level >> lg) == 1, n_bd[k_], zero_bf), t_bf[k_]).astype(BF16)
              for k_ in chains}
        t_inv = {k_: t_inv[k_] - _dot(t_bf[k_], nt[k_]) for k_ in chains}
    t_bf = {k_: t_inv[k_].astype(BF16) for k_ in chains}

    av = {k_: _dot(a_kk[k_], buf["v_bd"][k_]).astype(BF16) for k_ in chains}
    w12 = {k_: _dot(t_bf[k_], jnp.concatenate([buf["kk_bd"][k_], av[k_]], axis=1))
           for k_ in chains}
    w1 = {k_: w12[k_][:, 0:LANES].astype(BF16) for k_ in chains}
    vw = {k_: jnp.concatenate([buf["v_bd"][k_], (-w12[k_][:, LANES:2 * LANES]).astype(BF16)], axis=0)
          for k_ in chains}

    y_intra = {k_: _dot(jnp.concatenate([a_rk[k_], a_rb[k_]], axis=1), vw[k_]) for k_ in chains}
    q_eff = {k_: (buf["q_t"][k_] - _dot(a_rb[k_], w1[k_])).astype(BF16) for k_ in chains}
    g_mat = {k_: _dot(buf["kb_end"][k_], vw[k_], TN) for k_ in chains}
    m_mat = {k_: (jnp.where(diag, jnp.broadcast_to(buf["d_end"][k_], (2 * c, LANES)), 0.0)
                  - _dot(buf["b_end"][k_], w1[k_], TN)).astype(BF16) for k_ in chains}

    state = [st_ref[p] for p in range(n_pairs)]
    y_rows = []
    for ic in range(n_chunks):
        y_c = [_dot(q_eff[(ic, p)], state[p]) + y_intra[(ic, p)] for p in range(n_pairs)]
        state = [_dot(m_mat[(ic, p)], state[p]) + g_mat[(ic, p)] for p in range(n_pairs)]
        y_rows.append(jnp.concatenate(y_c, axis=1))
    for p in range(n_pairs):
        st_ref[p] = state[p]

    y = jnp.concatenate(y_rows, axis=0)
    ones_bd = _head_ones(hd)
    mean = _seg_sum(y, ones_bd) * (1.0 / hd)
    yc = y - mean
    var = _seg_sum(yc * yc, ones_bd) * (1.0 / hd)
    yn = yc * lax.rsqrt(var + GN_EPS) * vec_ref[_LNW:_LNW + 1, :] + vec_ref[_LNB:_LNB + 1, :]
    o_ref[0] = ((yn + buf["bonus"]) * buf["gate"]).astype(o_ref.dtype)


def _rwkv_kernel(z_ref, mu_ref, vec_ref, w2_ref, a2_ref, g2_ref, o_ref, st_ref, carry_ref,
                 *, d_rw, lora_wa, lora_g):
    @pl.when(pl.program_id(1) == 0)
    def _():
        st_ref[...] = jnp.zeros_like(st_ref)
        carry_ref[...] = jnp.zeros_like(carry_ref)

    buf = _rwkv_prepare(z_ref, mu_ref, vec_ref, w2_ref, a2_ref, g2_ref, carry_ref,
                        d_rw=d_rw, lora_wa=lora_wa, lora_g=lora_g)
    _rwkv_delta(buf, st_ref, vec_ref, o_ref, d_rw=d_rw, n_chunks=z_ref.shape[1] // RWKV_C)


def _rwkv(z, zw, mu, vecs, w2p, a2p, g2p, d_rw, lora_wa, lora_g):
    bsz, seqlen, _ = z.shape
    lb = RWKV_LB
    const = lambda shape: pl.BlockSpec(shape, lambda b, j: (0,) * len(shape))
    params = (mu, vecs, w2p, a2p, g2p)
    return pl.pallas_call(
        functools.partial(_rwkv_kernel, d_rw=d_rw, lora_wa=lora_wa, lora_g=lora_g),
        grid=(bsz, seqlen // lb),
        in_specs=[pl.BlockSpec((1, lb, zw), lambda b, j: (b, j, 0))]
                 + [const(p.shape) for p in params],
        out_specs=pl.BlockSpec((1, lb, d_rw), lambda b, j: (b, j, 0)),
        out_shape=jax.ShapeDtypeStruct((bsz, seqlen, d_rw), BF16),
        scratch_shapes=[
            pltpu.VMEM((d_rw // LANES, LANES, LANES), F32),
            pltpu.VMEM((SUBLANES, zw), F32),
        ],
        compiler_params=pltpu.CompilerParams(
            dimension_semantics=("arbitrary", "arbitrary"),
            vmem_limit_bytes=VMEM_LIMIT),
        name="rwkv",
    )(z, *params)


def _out_ffn_kernel(x_ref, ys_ref, yr_ref, wo_ref, g_ref, wu_ref, wd_ref, o_ref, *, final_norm):
    x = x_ref[0]
    mix = jnp.concatenate([ys_ref[0], yr_ref[0]], axis=1)
    h = x + jnp.dot(mix, wo_ref[...], preferred_element_type=F32)
    ms = jnp.mean(h * h, axis=-1, keepdims=True)
    hn = (h * lax.rsqrt(ms + NORM_EPS) * g_ref[0:1, :]).astype(BF16)
    d_ff = wu_ref.shape[1]
    acc = h
    for j in range(d_ff // FFN_CHUNK):
        sl = slice(j * FFN_CHUNK, (j + 1) * FFN_CHUNK)
        up = jnp.dot(hn, wu_ref[:, sl], preferred_element_type=F32)
        up = jnp.square(jnp.maximum(up, 0.0)).astype(BF16)
        acc = acc + jnp.dot(up, wd_ref[sl, :], preferred_element_type=F32)
    if final_norm:
        ms2 = jnp.mean(acc * acc, axis=-1, keepdims=True)
        acc = acc * lax.rsqrt(ms2 + NORM_EPS) * g_ref[1:2, :]
    o_ref[0] = acc


def _out_ffn(x, y_s5, y_rw, w_out, gains, w_up, w_down, final_norm):
    bsz, seqlen, d_model = x.shape
    tm = FFN_TM
    const = lambda shape: pl.BlockSpec(shape, lambda b, i: (0,) * len(shape))
    act = lambda width: pl.BlockSpec((1, tm, width), lambda b, i: (b, i, 0))
    return pl.pallas_call(
        functools.partial(_out_ffn_kernel, final_norm=final_norm),
        grid=(bsz, seqlen // tm),
        in_specs=[
            act(d_model), act(y_s5.shape[2]), act(y_rw.shape[2]),
            const(w_out.shape), const(gains.shape), const(w_up.shape), const(w_down.shape),
        ],
        out_specs=act(d_model),
        out_shape=jax.ShapeDtypeStruct((bsz, seqlen, d_model), x.dtype),
        compiler_params=pltpu.CompilerParams(
            dimension_semantics=("arbitrary", "arbitrary"),
            vmem_limit_bytes=VMEM_LIMIT),
        name="out_ffn",
    )(x, y_s5, y_rw, w_out, gains, w_up, w_down)


def _round_up(n, m):
    return -(-n // m) * m


def _s5_weights(lam_re, lam_im, log_dt, b_re, b_im, c_re, c_im):
    groups, n_p = lam_re.shape
    n_h = b_re.shape[2]
    dt = jnp.exp(log_dt.astype(F32))[:, None]
    lr, li = lam_re.astype(F32), lam_im.astype(F32)
    mag = jnp.exp(lr * dt)
    a_re = mag * jnp.cos(li * dt)
    a_im = mag * jnp.sin(li * dt)
    den = lr * lr + li * li
    q_re = ((a_re - 1.0) * lr + a_im * li) / den
    q_im = (a_im * lr - (a_re - 1.0) * li) / den
    bb_re = q_re[:, :, None] * b_re - q_im[:, :, None] * b_im
    bb_im = q_re[:, :, None] * b_im + q_im[:, :, None] * b_re

    tile_groups = 2 * LANES // n_p
    k_groups = LANES // n_h
    n_tiles = groups // tile_groups
    place = np.zeros((n_tiles, k_groups, tile_groups), np.float32)
    for j in range(n_tiles):
        for g in range(tile_groups):
            place[j, (j * tile_groups + g) % k_groups, g] = 1.0

    def b_tiles(bb):
        t = jnp.einsum('jgph,jlg->jlhgp', bb.reshape(n_tiles, tile_groups, n_p, n_h), place)
        return t.reshape(n_tiles, LANES, 2 * LANES)

    wb = jnp.concatenate([b_tiles(bb_re), b_tiles(bb_im)], axis=0).astype(BF16)

    n_out = groups // k_groups
    eye = np.eye(k_groups, dtype=np.float32)

    def c_tiles(cc):
        t = jnp.einsum('mghp,gk->mgpkh', cc.astype(F32).reshape(n_out, k_groups, n_h, n_p), eye)
        return t.reshape(n_out, k_groups * n_p, LANES).astype(BF16)

    lam_bar = jnp.stack([a_re.reshape(-1), a_im.reshape(-1)])
    return wb, lam_bar, c_tiles(c_re), c_tiles(-c_im)


def kernel(x, norm_mix_g, w_in, lam_re, lam_im, log_dt, b_re, b_im, c_re, c_im, d_skip, w_glu, b_glu, s5_out_g, mu_shift, w0, w2, a0, a2, g2, k_k, k_a, r_k, ln_x_w, ln_x_b, w_out, norm_ffn_g, w_up, w_down, norm_final_g):
    depth, d_model, _ = w_in.shape
    d_s5 = d_skip.shape[1]
    d_rw = w0.shape[1]
    n_w, n_a, n_g = w2.shape[1], a2.shape[1], g2.shape[1]
    lora_wa = _round_up(n_w + n_a, LANES)
    lora_g = _round_up(n_g, LANES)
    d_rw_in = 3 * d_rw + n_w + n_a + n_g
    zw = _round_up(3 * d_rw + lora_wa + lora_g, d_s5)
    assert n_w + n_a == lora_wa and zw >= d_rw_in

    h = x
    for i in range(depth):
        w_cat = jnp.concatenate(
            [w_in[i][:, d_s5:], jnp.zeros((d_model, zw - d_rw_in), w_in.dtype), w_in[i][:, :d_s5]],
            axis=1).astype(BF16)
        mu = jnp.pad(mu_shift[i].astype(F32), (0, zw - d_rw_in)).reshape(1, zw)
        w2p = jnp.pad(w2[i], ((0, lora_wa - n_w), (0, 0))).astype(BF16)
        a2p = jnp.pad(a2[i], ((n_w, lora_wa - n_w - n_a), (0, 0))).astype(BF16)
        g2p = jnp.pad(g2[i], ((0, lora_g - n_g), (0, 0))).astype(BF16)
        rw_vecs = jnp.stack([w0[i], a0[i], k_k[i], k_a[i], r_k[i].reshape(-1), ln_x_w[i], ln_x_b[i],
                             jnp.zeros_like(w0[i])]).astype(F32)
        s5_vecs = jnp.stack([d_skip[i], b_glu[i], s5_out_g[i]]).astype(F32)
        gains = jnp.stack([norm_ffn_g[i], norm_final_g]).astype(F32)

        z = _in_proj(h, norm_mix_g[i].reshape(1, -1).astype(F32), w_cat)

        wb, lam_bar, wc_re, wc_im = _s5_weights(
            lam_re[i], lam_im[i], log_dt[i], b_re[i], b_im[i], c_re[i], c_im[i])
        y_s5 = _s5(z, zw // d_s5, wb, lam_bar, wc_re, wc_im, s5_vecs, w_glu[i].astype(BF16))
        y_rw = _rwkv(z, zw, mu, rw_vecs, w2p, a2p, g2p, d_rw, lora_wa, lora_g)

        h = _out_ffn(h, y_s5, y_rw, w_out[i].astype(BF16), gains,
                     w_up[i].astype(BF16), w_down[i].astype(BF16), final_norm=(i == depth - 1))
    return h
```

```python
import functools
import math

import numpy as np
import jax
import jax.numpy as jnp
from jax import lax
from jax.experimental import pallas as pl
from jax.experimental.pallas import tpu as pltpu

F32 = jnp.float32
BF16 = jnp.bfloat16

NORM_EPS = 1e-6
GN_EPS = 64e-5

S5_GROUP = 16
S5_STATE = 64
RWKV_HEAD = 64
LANES = 128
SUBLANES = 8
VMEM_LIMIT = 56 * 1024 * 1024

IN_TM = 512
IN_TN = 512
S5_TC = 128
S5_STRIP = 512
RWKV_LB = 512
RWKV_SUB = 256
RWKV_C = 64
FFN_TM = 256
FFN_CHUNK = 1024

MM = (((1,), (0,)), ((), ()))
NT = (((1,), (1,)), ((), ()))
TN = (((0,), (0,)), ((), ()))


def _dot(a, b, dims=MM):
    return lax.dot_general(a.astype(BF16), b.astype(BF16), dims,
                           preferred_element_type=F32)


def _dot_split_rhs(a_bf16, b, passes):
    acc = None
    rem = b
    for _ in range(passes):
        part = rem.astype(BF16)
        term = jnp.dot(a_bf16, part, preferred_element_type=F32)
        acc = term if acc is None else acc + term
        rem = rem - part.astype(F32)
    return acc


def _in_proj_kernel(x_ref, g_ref, w_ref, z_ref):
    x = x_ref[0]
    ms = jnp.mean(x * x, axis=-1, keepdims=True)
    xn = (x * lax.rsqrt(ms + NORM_EPS) * g_ref[...]).astype(BF16)
    n = w_ref.shape[1]
    for j in range(n // IN_TN):
        sl = slice(j * IN_TN, (j + 1) * IN_TN)
        z_ref[0, :, sl] = jnp.dot(xn, w_ref[:, sl], preferred_element_type=F32).astype(BF16)


def _in_proj(x, g, w):
    bsz, seqlen, d_model = x.shape
    tm = IN_TM
    return pl.pallas_call(
        _in_proj_kernel,
        grid=(bsz, seqlen // tm),
        in_specs=[
            pl.BlockSpec((1, tm, d_model), lambda b, i: (b, i, 0)),
            pl.BlockSpec((1, d_model), lambda b, i: (0, 0)),
            pl.BlockSpec(w.shape, lambda b, i: (0, 0)),
        ],
        out_specs=pl.BlockSpec((1, tm, w.shape[1]), lambda b, i: (b, i, 0)),
        out_shape=jax.ShapeDtypeStruct((bsz, seqlen, w.shape[1]), BF16),
        compiler_params=pltpu.CompilerParams(
            dimension_semantics=("arbitrary", "arbitrary"),
            vmem_limit_bytes=VMEM_LIMIT),
        name="in_proj",
    )(x, g, w)


def _s5_kernel(u_ref, wb_ref, lam_ref, wcre_ref, wcim_ref, vec_ref, wglu_ref, o_ref,
               st_ref, carry_ref, *, pitch):
    bsz, tc, d_s5 = u_ref.shape
    n_state = lam_ref.shape[1]
    n_slabs_half = n_state // LANES
    pad = pitch - tc
    d_skip, b_glu, g_out = vec_ref[0:1, :], vec_ref[1:2, :], vec_ref[2:3, :]

    @pl.when(pl.program_id(0) == 0)
    def _():
        carry_ref[...] = jnp.zeros_like(carry_ref)

    zero_pad = jnp.zeros((pad, d_s5), BF16)
    u_bf = jnp.concatenate(
        [piece for b in range(bsz) for piece in (u_ref[b], zero_pad)], axis=0)

    n_tiles = wb_ref.shape[0]
    tile_w = wb_ref.shape[2]
    tiles_per_half = n_tiles // 2
    slabs_per_tile = tile_w // LANES
    for j in range(n_tiles):
        kb = (j % tiles_per_half) * tile_w // (S5_STATE // S5_GROUP * LANES)
        res = jnp.dot(u_bf[:, kb * LANES:(kb + 1) * LANES], wb_ref[j],
                      preferred_element_type=F32)
        for h in range(slabs_per_tile):
            st_ref[j * slabs_per_tile + h] = res[:, h * LANES:(h + 1) * LANES]

    slabs_per_strip = S5_STRIP // LANES
    for s in range(n_slabs_half // slabs_per_strip):
        re_ids = [s * slabs_per_strip + q for q in range(slabs_per_strip)]
        im_ids = [n_slabs_half + i for i in re_ids]
        a_re = [jnp.broadcast_to(lam_ref[0:1, i * LANES:(i + 1) * LANES], (bsz, LANES)) for i in re_ids]
        a_im = [jnp.broadcast_to(lam_ref[1:2, i * LANES:(i + 1) * LANES], (bsz, LANES)) for i in re_ids]

        def step(t, carry, re_ids=re_ids, im_ids=im_ids, a_re=a_re, a_im=a_im):
            s_re, s_im = carry
            rows = pl.ds(t, bsz, stride=pitch)
            n_re, n_im = [], []
            for q in range(len(re_ids)):
                n_re.append(a_re[q] * s_re[q] - a_im[q] * s_im[q] + st_ref[re_ids[q], rows, :])
                n_im.append(a_re[q] * s_im[q] + a_im[q] * s_re[q] + st_ref[im_ids[q], rows, :])
            for q in range(len(re_ids)):
                st_ref[re_ids[q], rows, :] = n_re[q]
                st_ref[im_ids[q], rows, :] = n_im[q]
            return tuple(n_re), tuple(n_im)

        init = (tuple(carry_ref[i] for i in re_ids), tuple(carry_ref[i] for i in im_ids))
        s_re, s_im = lax.fori_loop(0, tc, step, init, unroll=4)
        for q in range(len(re_ids)):
            carry_ref[re_ids[q]] = s_re[q]
            carry_ref[im_ids[q]] = s_im[q]

    n_out_tiles = wcre_ref.shape[0]
    slabs_per_out = wcre_ref.shape[1] // LANES
    ys = []
    for m in range(n_out_tiles):
        ids = [m * slabs_per_out + q for q in range(slabs_per_out)]
        x_re = jnp.concatenate([st_ref[i] for i in ids], axis=1).astype(BF16)
        x_im = jnp.concatenate([st_ref[n_slabs_half + i] for i in ids], axis=1).astype(BF16)
        ys.append(jnp.dot(x_re, wcre_ref[m], preferred_element_type=F32)
                  + jnp.dot(x_im, wcim_ref[m], preferred_element_type=F32))
    y = jnp.concatenate(ys, axis=-1) + d_skip * u_bf.astype(F32)
    g = 0.5 * y * (1.0 + lax.erf(y * (1.0 / math.sqrt(2.0))))
    gate = jax.nn.sigmoid(
        jnp.dot(g.astype(BF16), wglu_ref[...], preferred_element_type=F32) + b_glu)
    out = g * gate
    ms = jnp.mean(out * out, axis=-1, keepdims=True)
    out = (out * lax.rsqrt(ms + NORM_EPS) * g_out).astype(o_ref.dtype)
    for b in range(bsz):
        o_ref[b] = out[b * pitch:b * pitch + tc]


def _s5(z, col_block, wb, lam_bar, wc_re, wc_im, vecs, w_glu):
    bsz, seqlen, _ = z.shape
    d_s5 = vecs.shape[1]
    tc = S5_TC
    pitch = tc + SUBLANES
    n_state = lam_bar.shape[1]
    const = lambda shape: pl.BlockSpec(shape, lambda i: (0,) * len(shape))
    return pl.pallas_call(
        functools.partial(_s5_kernel, pitch=pitch),
        grid=(seqlen // tc,),
        in_specs=[
            pl.BlockSpec((bsz, tc, d_s5), lambda i: (0, i, col_block)),
            const(wb.shape), const(lam_bar.shape), const(wc_re.shape), const(wc_im.shape),
            const(vecs.shape), const(w_glu.shape),
        ],
        out_specs=pl.BlockSpec((bsz, tc, d_s5), lambda i: (0, i, 0)),
        out_shape=jax.ShapeDtypeStruct((bsz, seqlen, d_s5), BF16),
        scratch_shapes=[
            pltpu.VMEM((2 * n_state // LANES, bsz * pitch, LANES), F32),
            pltpu.VMEM((2 * n_state // LANES, bsz, LANES), F32),
        ],
        compiler_params=pltpu.CompilerParams(
            dimension_semantics=("arbitrary",),
            vmem_limit_bytes=VMEM_LIMIT),
        name="s5",
    )(z, wb, lam_bar, wc_re, wc_im, vecs, w_glu)


_W0, _A0, _KK, _KA, _RK, _LNW, _LNB = range(7)


def _seg_sum(x, ones_bd):
    w = ones_bd.shape[0]
    outs = [_dot(x[:, i * w:(i + 1) * w], ones_bd) for i in range(x.shape[1] // w)]
    return jnp.concatenate(outs, axis=-1)


def _head_ones(hd):
    seg_w = 2 * LANES
    li = lax.broadcasted_iota(jnp.int32, (seg_w, seg_w), 0) // hd
    lj = lax.broadcasted_iota(jnp.int32, (seg_w, seg_w), 1) // hd
    return jnp.where(li == lj, 1.0, 0.0).astype(BF16)


def _rwkv_prepare(z, prev_row, buf, mu_ref, vec_ref, w2_ref, a2_ref, g2_ref, *, d_rw, lora_wa, lora_g):
    lb = z.shape[0]
    c = RWKV_C
    hd = RWKV_HEAD
    n_pairs = d_rw // LANES
    vec = lambda i: vec_ref[i:i + 1, :]

    row = lax.broadcasted_iota(jnp.int32, z.shape, 0)
    z_prev = jnp.where(row == 0, prev_row, pltpu.roll(z, 1, 0))
    zz = z + (z_prev - z) * mu_ref[...]
    yield

    o = 3 * d_rw
    r = zz[:, 0:d_rw]
    k = zz[:, d_rw:2 * d_rw]
    v = zz[:, 2 * d_rw:o]
    zwa = zz[:, o:o + lora_wa]
    zg = zz[:, o + lora_wa:o + lora_wa + lora_g]

    wpre = vec(_W0) + _dot(jnp.tanh(zwa), w2_ref[...])
    logd = -math.exp(-0.5) * jax.nn.sigmoid(wpre)
    a = jax.nn.sigmoid(vec(_A0) + _dot(zwa, a2_ref[...]))
    yield
    for name in ("lhs1", "rhs1", "v_bd", "kk_bd", "kb_end", "b_end", "q_t", "d_end"):
        buf[name] = {}
    buf["gate"] = _dot(jax.nn.sigmoid(zg), g2_ref[...])
    yield

    ones_bd = _head_ones(hd)
    kk = k * vec(_KK)
    kk = kk * lax.rsqrt(jnp.maximum(_seg_sum(kk * kk, ones_bd), 1e-24))
    yield
    k2 = k * (1.0 + (a - 1.0) * vec(_KA))
    buf["bonus"] = _seg_sum(r * k2 * vec(_RK), ones_bd) * v
    bvec = kk * a
    yield

    ci = lax.broadcasted_iota(jnp.int32, (c, c), 0)
    cj = lax.broadcasted_iota(jnp.int32, (c, c), 1)
    ltri = jnp.where(cj <= ci, 1.0, 0.0).astype(BF16)
    head0 = lax.broadcasted_iota(jnp.int32, (c, LANES), 1) < hd

    def stack_heads(x):
        return jnp.concatenate([jnp.where(head0, x, 0.0), jnp.where(head0, 0.0, x)], axis=0)

    for ic in range(lb // c):
        rows = slice(ic * c, (ic + 1) * c)
        logd_c = logd[rows]
        cum = _dot_split_rhs(ltri, logd_c, 3)
        cum_end = cum[c - 1:c, :]
        e_inv = jnp.exp(-cum)
        e_end = jnp.exp(cum_end - cum)
        q_c = r[rows] * jnp.exp(cum)
        kk_c = kk[rows] * jnp.exp(cum - logd_c)
        k_hat = k2[rows] * e_inv
        b_hat = bvec[rows] * e_inv
        k_end = k2[rows] * e_end
        b_end = bvec[rows] * e_end
        d_c = jnp.exp(cum_end)
        for p in range(n_pairs):
            ls = slice(p * LANES, (p + 1) * LANES)
            key = (ic, p)
            b_end_bd = stack_heads(b_end[:, ls])
            buf["q_t"][key] = q_c[:, ls]
            buf["d_end"][key] = d_c[:, ls]
            buf["kk_bd"][key] = stack_heads(kk_c[:, ls]).astype(BF16)
            buf["v_bd"][key] = stack_heads(v[rows, ls]).astype(BF16)
            buf["b_end"][key] = b_end_bd.astype(BF16)
            buf["lhs1"][key] = jnp.concatenate([q_c[:, ls], kk_c[:, ls]], axis=0).astype(BF16)
            buf["rhs1"][key] = jnp.concatenate(
                [stack_heads(k_hat[:, ls]), stack_heads(b_hat[:, ls])], axis=0).astype(BF16)
            buf["kb_end"][key] = jnp.concatenate(
                [stack_heads(k_end[:, ls]), b_end_bd], axis=0).astype(BF16)
            yield


def _rwkv_delta(buf, chunk_ids, state, y_rows, *, d_rw):
    c = RWKV_C
    hd = RWKV_HEAD
    n_pairs = d_rw // LANES
    chains = [(ic, p) for ic in chunk_ids for p in range(n_pairs)]

    ri = lax.broadcasted_iota(jnp.int32, (c, LANES), 0)
    rl = lax.broadcasted_iota(jnp.int32, (c, LANES), 1)
    head0 = rl < hd
    incl = (rl % hd) <= ri
    strict = (rl % hd) < ri
    pi = lax.broadcasted_iota(jnp.int32, (2 * c, 2 * c), 0)
    pj = lax.broadcasted_iota(jnp.int32, (2 * c, 2 * c), 1)
    level = pi ^ pj
    diag = pi == pj
    eye = jnp.where(diag, 1.0, 0.0)
    n_levels = int(math.log2(c))

    def stack_heads(x):
        return jnp.concatenate([jnp.where(head0, x, 0.0), jnp.where(head0, 0.0, x)], axis=0)

    p1 = {k_: lax.dot_general(buf["lhs1"][k_], buf["rhs1"][k_], NT,
                              preferred_element_type=F32) for k_ in chains}
    a_rk = {k_: jnp.where(incl, p1[k_][0:c, 0:LANES], 0.0).astype(BF16) for k_ in chains}
    a_rb = {k_: jnp.where(incl, p1[k_][0:c, LANES:2 * LANES], 0.0).astype(BF16) for k_ in chains}
    a_kk = {k_: stack_heads(jnp.where(strict, p1[k_][c:2 * c, 0:LANES], 0.0)).astype(BF16)
            for k_ in chains}
    n_bd = {k_: stack_heads(jnp.where(strict, p1[k_][c:2 * c, LANES:2 * LANES], 0.0)) for k_ in chains}

    t_inv = {k_: eye - jnp.where(level == 1, n_bd[k_], 0.0) for k_ in chains}
    yield
    for lg in range(1, n_levels):
        m = 1 << lg
        starts = list(range(m, 2 * c, 2 * m))
        take = (lambda x: x) if m < SUBLANES else (
            lambda x: jnp.concatenate([x[s:s + m] for s in starts], axis=0))
        t_bf = {k_: t_inv[k_].astype(BF16) for k_ in chains}
        nt = {k_: _dot(take(jnp.where((level >> lg) == 1, n_bd[k_], 0.0)), t_bf[k_]) for k_ in chains}
        yield
        if m < SUBLANES:
            t_inv = {k_: t_inv[k_] - _dot(t_bf[k_], nt[k_]) for k_ in chains}
        else:
            zero = jnp.zeros((m, LANES), F32)
            for k_ in chains:
                nt_full = jnp.concatenate(
                    [piece for i in range(len(starts)) for piece in (zero, nt[k_][i * m:(i + 1) * m])], axis=0)
                t_r = take(t_inv[k_])
                t_r = t_r - _dot(t_r, nt_full)
                t_inv[k_] = jnp.concatenate(
                    [piece for i, s in enumerate(starts)
                     for piece in (t_inv[k_][s - m:s], t_r[i * m:(i + 1) * m])], axis=0)
        yield
    t_bf = {k_: t_inv[k_].astype(BF16) for k_ in chains}

    av = {k_: _dot(a_kk[k_], buf["v_bd"][k_]).astype(BF16) for k_ in chains}
    yield
    w12 = {k_: _dot(t_bf[k_], jnp.concatenate([buf["kk_bd"][k_], av[k_]], axis=1))
           for k_ in chains}
    w1 = {k_: w12[k_][:, 0:LANES].astype(BF16) for k_ in chains}
    vw = {k_: jnp.concatenate([buf["v_bd"][k_], (-w12[k_][:, LANES:2 * LANES]).astype(BF16)], axis=0)
          for k_ in chains}
    yield

    y_intra = {k_: _dot(jnp.concatenate([a_rk[k_], a_rb[k_]], axis=1), vw[k_]) for k_ in chains}
    yield
    q_eff = {k_: (buf["q_t"][k_] - _dot(a_rb[k_], w1[k_])).astype(BF16) for k_ in chains}
    yield
    g_mat = {k_: _dot(buf["kb_end"][k_], vw[k_], TN) for k_ in chains}
    yield
    m_mat = {k_: (jnp.where(diag, jnp.broadcast_to(buf["d_end"][k_], (2 * c, LANES)), 0.0)
                  - _dot(buf["b_end"][k_], w1[k_], TN)).astype(BF16) for k_ in chains}
    yield

    for ic in chunk_ids:
        y_c = [_dot(q_eff[(ic, p)], state[p]) + y_intra[(ic, p)] for p in range(n_pairs)]
        state[:] = [_dot(m_mat[(ic, p)], state[p]) + g_mat[(ic, p)] for p in range(n_pairs)]
        y_rows.append(jnp.concatenate(y_c, axis=1))
        yield


def _rwkv_finish(buf, y_rows, vec_ref, o_ref, rows):
    hd = RWKV_HEAD
    y = jnp.concatenate(y_rows, axis=0)
    ones_bd = _head_ones(hd)
    mean = _seg_sum(y, ones_bd) * (1.0 / hd)
    yc = y - mean
    var = _seg_sum(yc * yc, ones_bd) * (1.0 / hd)
    yn = yc * lax.rsqrt(var + GN_EPS) * vec_ref[_LNW:_LNW + 1, :] + vec_ref[_LNB:_LNB + 1, :]
    o_ref[0, rows, :] = ((yn + buf["bonus"]) * buf["gate"]).astype(o_ref.dtype)


def _run_interleaved(*gens):
    live = list(gens)
    while live:
        for gen in list(live):
            if next(gen, StopIteration) is StopIteration:
                live.remove(gen)


def _rwkv_kernel(z_ref, mu_ref, vec_ref, w2_ref, a2_ref, g2_ref, o_ref, st_ref, carry_ref,
                 *, d_rw, lora_wa, lora_g):
    @pl.when(pl.program_id(1) == 0)
    def _():
        st_ref[...] = jnp.zeros_like(st_ref)
        carry_ref[...] = jnp.zeros_like(carry_ref)

    n_pairs = d_rw // LANES
    sub = RWKV_SUB
    n_sub = z_ref.shape[1] // sub
    chunks = list(range(sub // RWKV_C))
    state = [st_ref[p] for p in range(n_pairs)]

    prev_row = carry_ref[0:1, :]
    bufs, prepares = [], []
    for i in range(n_sub):
        z = z_ref[0, i * sub:(i + 1) * sub, :].astype(F32)
        bufs.append({})
        prepares.append(_rwkv_prepare(z, prev_row, bufs[i], mu_ref, vec_ref, w2_ref, a2_ref, g2_ref,
                                      d_rw=d_rw, lora_wa=lora_wa, lora_g=lora_g))
        prev_row = z[sub - 1:sub, :]
    carry_ref[0:1, :] = prev_row

    _run_interleaved(prepares[0])
    for i in range(n_sub):
        y_rows = []
        delta = _rwkv_delta(bufs[i], chunks, state, y_rows, d_rw=d_rw)
        _run_interleaved(*([delta] + prepares[i + 1:i + 2]))
        _rwkv_finish(bufs[i], y_rows, vec_ref, o_ref, slice(i * sub, (i + 1) * sub))
    for p in range(n_pairs):
        st_ref[p] = state[p]


def _rwkv(z, zw, mu, vecs, w2p, a2p, g2p, d_rw, lora_wa, lora_g):
    bsz, seqlen, _ = z.shape
    lb = RWKV_LB
    const = lambda shape: pl.BlockSpec(shape, lambda b, j: (0,) * len(shape))
    params = (mu, vecs, w2p, a2p, g2p)
    return pl.pallas_call(
        functools.partial(_rwkv_kernel, d_rw=d_rw, lora_wa=lora_wa, lora_g=lora_g),
        grid=(bsz, seqlen // lb),
        in_specs=[pl.BlockSpec((1, lb, zw), lambda b, j: (b, j, 0))]
                 + [const(p.shape) for p in params],
        out_specs=pl.BlockSpec((1, lb, d_rw), lambda b, j: (b, j, 0)),
        out_shape=jax.ShapeDtypeStruct((bsz, seqlen, d_rw), BF16),
        scratch_shapes=[
            pltpu.VMEM((d_rw // LANES, LANES, LANES), F32),
            pltpu.VMEM((SUBLANES, zw), F32),
        ],
        compiler_params=pltpu.CompilerParams(
            dimension_semantics=("arbitrary", "arbitrary"),
            vmem_limit_bytes=VMEM_LIMIT),
        name="rwkv",
    )(z, *params)


def _out_ffn_kernel(x_ref, ys_ref, yr_ref, wo_ref, g_ref, wu_ref, wd_ref, o_ref, *, final_norm):
    x = x_ref[0]
    mix = jnp.concatenate([ys_ref[0], yr_ref[0]], axis=1)
    h = x + jnp.dot(mix, wo_ref[...], preferred_element_type=F32)
    ms = jnp.mean(h * h, axis=-1, keepdims=True)
    hn = (h * lax.rsqrt(ms + NORM_EPS) * g_ref[0:1, :]).astype(BF16)
    d_ff = wu_ref.shape[1]
    acc = h
    for j in range(d_ff // FFN_CHUNK):
        sl = slice(j * FFN_CHUNK, (j + 1) * FFN_CHUNK)
        up = jnp.dot(hn, wu_ref[:, sl], preferred_element_type=F32)
        up = jnp.square(jnp.maximum(up, 0.0)).astype(BF16)
        acc = acc + jnp.dot(up, wd_ref[sl, :], preferred_element_type=F32)
    if final_norm:
        ms2 = jnp.mean(acc * acc, axis=-1, keepdims=True)
        acc = acc * lax.rsqrt(ms2 + NORM_EPS) * g_ref[1:2, :]
    o_ref[0] = acc


def _out_ffn(x, y_s5, y_rw, w_out, gains, w_up, w_down, final_norm):
    bsz, seqlen, d_model = x.shape
    tm = FFN_TM
    const = lambda shape: pl.BlockSpec(shape, lambda b, i: (0,) * len(shape))
    act = lambda width: pl.BlockSpec((1, tm, width), lambda b, i: (b, i, 0))
    return pl.pallas_call(
        functools.partial(_out_ffn_kernel, final_norm=final_norm),
        grid=(bsz, seqlen // tm),
        in_specs=[
            act(d_model), act(y_s5.shape[2]), act(y_rw.shape[2]),
            const(w_out.shape), const(gains.shape), const(w_up.shape), const(w_down.shape),
        ],
        out_specs=act(d_model),
        out_shape=jax.ShapeDtypeStruct((bsz, seqlen, d_model), x.dtype),
        compiler_params=pltpu.CompilerParams(
            dimension_semantics=("arbitrary", "arbitrary"),
            vmem_limit_bytes=VMEM_LIMIT),
        name="out_ffn",
    )(x, y_s5, y_rw, w_out, gains, w_up, w_down)


def _round_up(n, m):
    return -(-n // m) * m


def _s5_weights(lam_re, lam_im, log_dt, b_re, b_im, c_re, c_im):
    groups, n_p = lam_re.shape
    n_h = b_re.shape[2]
    dt = jnp.exp(log_dt.astype(F32))[:, None]
    lr, li = lam_re.astype(F32), lam_im.astype(F32)
    mag = jnp.exp(lr * dt)
    a_re = mag * jnp.cos(li * dt)
    a_im = mag * jnp.sin(li * dt)
    den = lr * lr + li * li
    q_re = ((a_re - 1.0) * lr + a_im * li) / den
    q_im = (a_im * lr - (a_re - 1.0) * li) / den
    bb_re = q_re[:, :, None] * b_re - q_im[:, :, None] * b_im
    bb_im = q_re[:, :, None] * b_im + q_im[:, :, None] * b_re

    tile_groups = 2 * LANES // n_p
    k_groups = LANES // n_h
    n_tiles = groups // tile_groups
    place = np.zeros((n_tiles, k_groups, tile_groups), np.float32)
    for j in range(n_tiles):
        for g in range(tile_groups):
            place[j, (j * tile_groups + g) % k_groups, g] = 1.0

    def b_tiles(bb):
        t = jnp.einsum('jgph,jlg->jlhgp', bb.reshape(n_tiles, tile_groups, n_p, n_h), place)
        return t.reshape(n_tiles, LANES, 2 * LANES)

    wb = jnp.concatenate([b_tiles(bb_re), b_tiles(bb_im)], axis=0).astype(BF16)

    n_out = groups // k_groups
    eye = np.eye(k_groups, dtype=np.float32)

    def c_tiles(cc):
        t = jnp.einsum('mghp,gk->mgpkh', cc.astype(F32).reshape(n_out, k_groups, n_h, n_p), eye)
        return t.reshape(n_out, k_groups * n_p, LANES).astype(BF16)

    lam_bar = jnp.stack([a_re.reshape(-1), a_im.reshape(-1)])
    return wb, lam_bar, c_tiles(c_re), c_tiles(-c_im)


def kernel(x, norm_mix_g, w_in, lam_re, lam_im, log_dt, b_re, b_im, c_re, c_im, d_skip, w_glu, b_glu, s5_out_g, mu_shift, w0, w2, a0, a2, g2, k_k, k_a, r_k, ln_x_w, ln_x_b, w_out, norm_ffn_g, w_up, w_down, norm_final_g):
    depth, d_model, _ = w_in.shape
    d_s5 = d_skip.shape[1]
    d_rw = w0.shape[1]
    n_w, n_a, n_g = w2.shape[1], a2.shape[1], g2.shape[1]
    lora_wa = _round_up(n_w + n_a, LANES)
    lora_g = _round_up(n_g, LANES)
    d_rw_in = 3 * d_rw + n_w + n_a + n_g
    zw = _round_up(3 * d_rw + lora_wa + lora_g, d_s5)
    assert n_w + n_a == lora_wa and zw >= d_rw_in

    h = x
    for i in range(depth):
        w_cat = jnp.concatenate(
            [w_in[i][:, d_s5:], jnp.zeros((d_model, zw - d_rw_in), w_in.dtype), w_in[i][:, :d_s5]],
            axis=1).astype(BF16)
        mu = jnp.pad(mu_shift[i].astype(F32), (0, zw - d_rw_in)).reshape(1, zw)
        w2p = jnp.pad(w2[i], ((0, lora_wa - n_w), (0, 0))).astype(BF16)
        a2p = jnp.pad(a2[i], ((n_w, lora_wa - n_w - n_a), (0, 0))).astype(BF16)
        g2p = jnp.pad(g2[i], ((0, lora_g - n_g), (0, 0))).astype(BF16)
        rw_vecs = jnp.stack([w0[i], a0[i], k_k[i], k_a[i], r_k[i].reshape(-1), ln_x_w[i], ln_x_b[i],
                             jnp.zeros_like(w0[i])]).astype(F32)
        s5_vecs = jnp.stack([d_skip[i], b_glu[i], s5_out_g[i]]).astype(F32)
        gains = jnp.stack([norm_ffn_g[i], norm_final_g]).astype(F32)

        z = _in_proj(h, norm_mix_g[i].reshape(1, -1).astype(F32), w_cat)

        wb, lam_bar, wc_re, wc_im = _s5_weights(
            lam_re[i], lam_im[i], log_dt[i], b_re[i], b_im[i], c_re[i], c_im[i])
        y_s5 = _s5(z, zw // d_s5, wb, lam_bar, wc_re, wc_im, s5_vecs, w_glu[i].astype(BF16))
        y_rw = _rwkv(z, zw, mu, rw_vecs, w2p, a2p, g2p, d_rw, lora_wa, lora_g)

        h = _out_ffn(h, y_s5, y_rw, w_out[i].astype(BF16), gains,
                     w_up[i].astype(BF16), w_down[i].astype(BF16), final_norm=(i == depth - 1))
    return h
```

```python
import functools
import math

import numpy as np
import jax
import jax.numpy as jnp
from jax import lax
from jax.experimental import pallas as pl
from jax.experimental.pallas import tpu as pltpu

F32 = jnp.float32
BF16 = jnp.bfloat16

NORM_EPS = 1e-6
GN_EPS = 64e-5

S5_GROUP = 16
S5_STATE = 64
RWKV_HEAD = 64
LANES = 128
SUBLANES = 8
VMEM_LIMIT = 56 * 1024 * 1024

IN_TM = 512
IN_TN = 512
S5_TC = 128
S5_STRIP = 512
RWKV_LB = 512
RWKV_SUB = 256
RWKV_C = 64
FFN_TM = 256
FFN_CHUNK = 1024

MM = (((1,), (0,)), ((), ()))
NT = (((1,), (1,)), ((), ()))
TN = (((0,), (0,)), ((), ()))


def _dot(a, b, dims=MM):
    return lax.dot_general(a.astype(BF16), b.astype(BF16), dims,
                           preferred_element_type=F32)


def _dot_split_rhs(a_bf16, b, passes):
    acc = None
    rem = b
    for _ in range(passes):
        part = rem.astype(BF16)
        term = jnp.dot(a_bf16, part, preferred_element_type=F32)
        acc = term if acc is None else acc + term
        rem = rem - part.astype(F32)
    return acc


def _in_proj_kernel(x_ref, g_ref, w_ref, z_ref):
    x = x_ref[0]
    ms = jnp.mean(x * x, axis=-1, keepdims=True)
    xn = (x * lax.rsqrt(ms + NORM_EPS) * g_ref[...]).astype(BF16)
    n = w_ref.shape[1]
    for j in range(n // IN_TN):
        sl = slice(j * IN_TN, (j + 1) * IN_TN)
        z_ref[0, :, sl] = jnp.dot(xn, w_ref[:, sl], preferred_element_type=F32).astype(BF16)


def _in_proj(x, g, w):
    bsz, seqlen, d_model = x.shape
    tm = IN_TM
    return pl.pallas_call(
        _in_proj_kernel,
        grid=(bsz, seqlen // tm),
        in_specs=[
            pl.BlockSpec((1, tm, d_model), lambda b, i: (b, i, 0)),
            pl.BlockSpec((1, d_model), lambda b, i: (0, 0)),
            pl.BlockSpec(w.shape, lambda b, i: (0, 0)),
        ],
        out_specs=pl.BlockSpec((1, tm, w.shape[1]), lambda b, i: (b, i, 0)),
        out_shape=jax.ShapeDtypeStruct((bsz, seqlen, w.shape[1]), BF16),
        compiler_params=pltpu.CompilerParams(
            dimension_semantics=("arbitrary", "arbitrary"),
            vmem_limit_bytes=VMEM_LIMIT),
        name="in_proj",
    )(x, g, w)


def _s5_kernel(u_ref, wb_ref, lam_ref, wcre_ref, wcim_ref, vec_ref, wglu_ref, o_ref,
               st_ref, carry_ref, io_ref):
    bsz, tc, d_s5 = u_ref.shape
    n_state = lam_ref.shape[1]
    n_slabs_half = n_state // LANES
    n_io = d_s5 // LANES
    d_skip, b_glu, g_out = vec_ref[0:1, :], vec_ref[1:2, :], vec_ref[2:3, :]

    @pl.when(pl.program_id(0) == 0)
    def _():
        carry_ref[...] = jnp.zeros_like(carry_ref)

    for b in range(bsz):
        u_b = u_ref[b].astype(F32)
        for q in range(n_io):
            io_ref[q, pl.ds(b, tc, stride=bsz), :] = u_b[:, q * LANES:(q + 1) * LANES]
    u_f32 = jnp.concatenate([io_ref[q] for q in range(n_io)], axis=1)
    u_bf = u_f32.astype(BF16)

    n_tiles = wb_ref.shape[0]
    tile_w = wb_ref.shape[2]
    tiles_per_half = n_tiles // 2
    slabs_per_tile = tile_w // LANES
    for j in range(n_tiles):
        kb = (j % tiles_per_half) * tile_w // (S5_STATE // S5_GROUP * LANES)
        res = jnp.dot(u_bf[:, kb * LANES:(kb + 1) * LANES], wb_ref[j],
                      preferred_element_type=F32)
        for h in range(slabs_per_tile):
            st_ref[j * slabs_per_tile + h] = res[:, h * LANES:(h + 1) * LANES]

    slabs_per_strip = S5_STRIP // LANES
    for s in range(n_slabs_half // slabs_per_strip):
        re_ids = [s * slabs_per_strip + q for q in range(slabs_per_strip)]
        im_ids = [n_slabs_half + i for i in re_ids]
        a_re = [jnp.broadcast_to(lam_ref[0:1, i * LANES:(i + 1) * LANES], (bsz, LANES)) for i in re_ids]
        a_im = [jnp.broadcast_to(lam_ref[1:2, i * LANES:(i + 1) * LANES], (bsz, LANES)) for i in re_ids]

        def step(t, carry, re_ids=re_ids, im_ids=im_ids, a_re=a_re, a_im=a_im):
            s_re, s_im = carry
            rows = pl.ds(pl.multiple_of(t * bsz, bsz), bsz)
            n_re, n_im = [], []
            for q in range(len(re_ids)):
                n_re.append(a_re[q] * s_re[q] - a_im[q] * s_im[q] + st_ref[re_ids[q], rows, :])
                n_im.append(a_re[q] * s_im[q] + a_im[q] * s_re[q] + st_ref[im_ids[q], rows, :])
            for q in range(len(re_ids)):
                st_ref[re_ids[q], rows, :] = n_re[q]
                st_ref[im_ids[q], rows, :] = n_im[q]
            return tuple(n_re), tuple(n_im)

        init = (tuple(carry_ref[i] for i in re_ids), tuple(carry_ref[i] for i in im_ids))
        s_re, s_im = lax.fori_loop(0, tc, step, init, unroll=4)
        for q in range(len(re_ids)):
            carry_ref[re_ids[q]] = s_re[q]
            carry_ref[im_ids[q]] = s_im[q]

    n_out_tiles = wcre_ref.shape[0]
    slabs_per_out = wcre_ref.shape[1] // LANES
    ys = []
    for m in range(n_out_tiles):
        ids = [m * slabs_per_out + q for q in range(slabs_per_out)]
        x_re = jnp.concatenate([st_ref[i] for i in ids], axis=1).astype(BF16)
        x_im = jnp.concatenate([st_ref[n_slabs_half + i] for i in ids], axis=1).astype(BF16)
        ys.append(jnp.dot(x_re, wcre_ref[m], preferred_element_type=F32)
                  + jnp.dot(x_im, wcim_ref[m], preferred_element_type=F32))
    y = jnp.concatenate(ys, axis=-1) + d_skip * u_f32
    g = 0.5 * y * (1.0 + lax.erf(y * (1.0 / math.sqrt(2.0))))
    gate = jax.nn.sigmoid(
        jnp.dot(g.astype(BF16), wglu_ref[...], preferred_element_type=F32) + b_glu)
    out = g * gate
    ms = jnp.mean(out * out, axis=-1, keepdims=True)
    out = out * lax.rsqrt(ms + NORM_EPS) * g_out
    for q in range(n_io):
        io_ref[q] = out[:, q * LANES:(q + 1) * LANES]
    for b in range(bsz):
        o_ref[b] = jnp.concatenate(
            [io_ref[q, pl.ds(b, tc, stride=bsz), :] for q in range(n_io)], axis=1).astype(o_ref.dtype)


def _s5(z, col_block, wb, lam_bar, wc_re, wc_im, vecs, w_glu):
    bsz, seqlen, _ = z.shape
    d_s5 = vecs.shape[1]
    tc = S5_TC
    n_state = lam_bar.shape[1]
    const = lambda shape: pl.BlockSpec(shape, lambda i: (0,) * len(shape))
    return pl.pallas_call(
        _s5_kernel,
        grid=(seqlen // tc,),
        in_specs=[
            pl.BlockSpec((bsz, tc, d_s5), lambda i: (0, i, col_block)),
            const(wb.shape), const(lam_bar.shape), const(wc_re.shape), const(wc_im.shape),
            const(vecs.shape), const(w_glu.shape),
        ],
        out_specs=pl.BlockSpec((bsz, tc, d_s5), lambda i: (0, i, 0)),
        out_shape=jax.ShapeDtypeStruct((bsz, seqlen, d_s5), BF16),
        scratch_shapes=[
            pltpu.VMEM((2 * n_state // LANES, bsz * tc, LANES), F32),
            pltpu.VMEM((2 * n_state // LANES, bsz, LANES), F32),
            pltpu.VMEM((d_s5 // LANES, bsz * tc, LANES), F32),
        ],
        compiler_params=pltpu.CompilerParams(
            dimension_semantics=("arbitrary",),
            vmem_limit_bytes=VMEM_LIMIT),
        name="s5",
    )(z, wb, lam_bar, wc_re, wc_im, vecs, w_glu)


_W0, _A0, _KK, _KA, _RK, _LNW, _LNB = range(7)


def _seg_sum(x, ones_bd):
    w = ones_bd.shape[0]
    outs = [_dot(x[:, i * w:(i + 1) * w], ones_bd) for i in range(x.shape[1] // w)]
    return jnp.concatenate(outs, axis=-1)


def _head_ones(hd):
    seg_w = 2 * LANES
    li = lax.broadcasted_iota(jnp.int32, (seg_w, seg_w), 0) // hd
    lj = lax.broadcasted_iota(jnp.int32, (seg_w, seg_w), 1) // hd
    return jnp.where(li == lj, 1.0, 0.0).astype(BF16)


def _rwkv_prepare(z, prev_row, buf, mu_ref, vec_ref, w2_ref, a2_ref, g2_ref, *, d_rw, lora_wa, lora_g):
    lb = z.shape[0]
    c = RWKV_C
    hd = RWKV_HEAD
    n_pairs = d_rw // LANES
    vec = lambda i: vec_ref[i:i + 1, :]

    row = lax.broadcasted_iota(jnp.int32, z.shape, 0)
    z_prev = jnp.where(row == 0, prev_row, pltpu.roll(z, 1, 0))
    zz = z + (z_prev - z) * mu_ref[...]
    yield

    o = 3 * d_rw
    r = zz[:, 0:d_rw]
    k = zz[:, d_rw:2 * d_rw]
    v = zz[:, 2 * d_rw:o]
    zwa = zz[:, o:o + lora_wa]
    zg = zz[:, o + lora_wa:o + lora_wa + lora_g]

    wpre = vec(_W0) + _dot(jnp.tanh(zwa), w2_ref[...])
    logd = -math.exp(-0.5) * jax.nn.sigmoid(wpre)
    a = jax.nn.sigmoid(vec(_A0) + _dot(zwa, a2_ref[...]))
    yield
    for name in ("lhs1", "rhs1", "v_bd", "kk_bd", "kb_end", "b_end", "q_t", "d_end"):
        buf[name] = {}
    buf["gate"] = _dot(jax.nn.sigmoid(zg), g2_ref[...])
    yield

    ones_bd = _head_ones(hd)
    kk = k * vec(_KK)
    kk = kk * lax.rsqrt(jnp.maximum(_seg_sum(kk * kk, ones_bd), 1e-24))
    yield
    k2 = k * (1.0 + (a - 1.0) * vec(_KA))
    buf["bonus"] = _seg_sum(r * k2 * vec(_RK), ones_bd) * v
    bvec = kk * a
    yield

    ci = lax.broadcasted_iota(jnp.int32, (c, c), 0)
    cj = lax.broadcasted_iota(jnp.int32, (c, c), 1)
    ltri = jnp.where(cj <= ci, 1.0, 0.0).astype(BF16)
    head0 = lax.broadcasted_iota(jnp.int32, (c, LANES), 1) < hd

    def stack_heads(x):
        return jnp.concatenate([jnp.where(head0, x, 0.0), jnp.where(head0, 0.0, x)], axis=0)

    for ic in range(lb // c):
        rows = slice(ic * c, (ic + 1) * c)
        logd_c = logd[rows]
        cum = _dot_split_rhs(ltri, logd_c, 3)
        cum_end = cum[c - 1:c, :]
        e_inv = jnp.exp(-cum)
        e_end = jnp.exp(cum_end - cum)
        q_c = r[rows] * jnp.exp(cum)
        kk_c = kk[rows] * jnp.exp(cum - logd_c)
        k_hat = k2[rows] * e_inv
        b_hat = bvec[rows] * e_inv
        k_end = k2[rows] * e_end
        b_end = bvec[rows] * e_end
        d_c = jnp.exp(cum_end)
        for p in range(n_pairs):
            ls = slice(p * LANES, (p + 1) * LANES)
            key = (ic, p)
            b_end_bd = stack_heads(b_end[:, ls])
            buf["q_t"][key] = q_c[:, ls]
            buf["d_end"][key] = d_c[:, ls]
            buf["kk_bd"][key] = stack_heads(kk_c[:, ls]).astype(BF16)
            buf["v_bd"][key] = stack_heads(v[rows, ls]).astype(BF16)
            buf["b_end"][key] = b_end_bd.astype(BF16)
            buf["lhs1"][key] = jnp.concatenate([q_c[:, ls], kk_c[:, ls]], axis=0).astype(BF16)
            buf["rhs1"][key] = jnp.concatenate(
                [stack_heads(k_hat[:, ls]), stack_heads(b_hat[:, ls])], axis=0).astype(BF16)
            buf["kb_end"][key] = jnp.concatenate(
                [stack_heads(k_end[:, ls]), b_end_bd], axis=0).astype(BF16)
            yield


def _rwkv_delta(buf, chunk_ids, state, y_rows, *, d_rw):
    c = RWKV_C
    hd = RWKV_HEAD
    n_pairs = d_rw // LANES
    chains = [(ic, p) for ic in chunk_ids for p in range(n_pairs)]

    ri = lax.broadcasted_iota(jnp.int32, (c, LANES), 0)
    rl = lax.broadcasted_iota(jnp.int32, (c, LANES), 1)
    head0 = rl < hd
    incl = (rl % hd) <= ri
    strict = (rl % hd) < ri
    pi = lax.broadcasted_iota(jnp.int32, (2 * c, 2 * c), 0)
    pj = lax.broadcasted_iota(jnp.int32, (2 * c, 2 * c), 1)
    level = pi ^ pj
    diag = pi == pj
    eye = jnp.where(diag, 1.0, 0.0)
    n_levels = int(math.log2(c))

    def stack_heads(x):
        return jnp.concatenate([jnp.where(head0, x, 0.0), jnp.where(head0, 0.0, x)], axis=0)

    p1 = {k_: lax.dot_general(buf["lhs1"][k_], buf["rhs1"][k_], NT,
                              preferred_element_type=F32) for k_ in chains}
    a_rk = {k_: jnp.where(incl, p1[k_][0:c, 0:LANES], 0.0).astype(BF16) for k_ in chains}
    a_rb = {k_: jnp.where(incl, p1[k_][0:c, LANES:2 * LANES], 0.0).astype(BF16) for k_ in chains}
    a_kk = {k_: stack_heads(jnp.where(strict, p1[k_][c:2 * c, 0:LANES], 0.0)).astype(BF16)
            for k_ in chains}
    n_bd = {k_: stack_heads(jnp.where(strict, p1[k_][c:2 * c, LANES:2 * LANES], 0.0)) for k_ in chains}

    t_inv = {k_: eye - jnp.where(level == 1, n_bd[k_], 0.0) for k_ in chains}
    yield
    for lg in range(1, n_levels):
        m = 1 << lg
        starts = list(range(m, 2 * c, 2 * m))
        take = (lambda x: x) if m < SUBLANES else (
            lambda x: jnp.concatenate([x[s:s + m] for s in starts], axis=0))
        t_bf = {k_: t_inv[k_].astype(BF16) for k_ in chains}
        nt = {k_: _dot(take(jnp.where((level >> lg) == 1, n_bd[k_], 0.0)), t_bf[k_]) for k_ in chains}
        yield
        if m < SUBLANES:
            t_inv = {k_: t_inv[k_] - _dot(t_bf[k_], nt[k_]) for k_ in chains}
        else:
            zero = jnp.zeros((m, LANES), F32)
            for k_ in chains:
                nt_full = jnp.concatenate(
                    [piece for i in range(len(starts)) for piece in (zero, nt[k_][i * m:(i + 1) * m])], axis=0)
                t_r = take(t_inv[k_])
                t_r = t_r - _dot(t_r, nt_full)
                t_inv[k_] = jnp.concatenate(
                    [piece for i, s in enumerate(starts)
                     for piece in (t_inv[k_][s - m:s], t_r[i * m:(i + 1) * m])], axis=0)
        yield
    t_bf = {k_: t_inv[k_].astype(BF16) for k_ in chains}

    av = {k_: _dot(a_kk[k_], buf["v_bd"][k_]).astype(BF16) for k_ in chains}
    yield
    w12 = {k_: _dot(t_bf[k_], jnp.concatenate([buf["kk_bd"][k_], av[k_]], axis=1))
           for k_ in chains}
    w1 = {k_: w12[k_][:, 0:LANES].astype(BF16) for k_ in chains}
    vw = {k_: jnp.concatenate([buf["v_bd"][k_], (-w12[k_][:, LANES:2 * LANES]).astype(BF16)], axis=0)
          for k_ in chains}
    yield

    y_intra = {k_: _dot(jnp.concatenate([a_rk[k_], a_rb[k_]], axis=1), vw[k_]) for k_ in chains}
    yield
    q_eff = {k_: (buf["q_t"][k_] - _dot(a_rb[k_], w1[k_])).astype(BF16) for k_ in chains}
    yield
    g_mat = {k_: _dot(buf["kb_end"][k_], vw[k_], TN) for k_ in chains}
    yield
    m_mat = {k_: (jnp.where(diag, jnp.broadcast_to(buf["d_end"][k_], (2 * c, LANES)), 0.0)
                  - _dot(buf["b_end"][k_], w1[k_], TN)).astype(BF16) for k_ in chains}
    yield

    for ic in chunk_ids:
        y_c = [_dot(q_eff[(ic, p)], state[p]) + y_intra[(ic, p)] for p in range(n_pairs)]
        state[:] = [_dot(m_mat[(ic, p)], state[p]) + g_mat[(ic, p)] for p in range(n_pairs)]
        y_rows.append(jnp.concatenate(y_c, axis=1))
        yield


def _rwkv_finish(buf, y_rows, vec_ref, o_ref, rows):
    hd = RWKV_HEAD
    y = jnp.concatenate(y_rows, axis=0)
    ones_bd = _head_ones(hd)
    mean = _seg_sum(y, ones_bd) * (1.0 / hd)
    yc = y - mean
    var = _seg_sum(yc * yc, ones_bd) * (1.0 / hd)
    yn = yc * lax.rsqrt(var + GN_EPS) * vec_ref[_LNW:_LNW + 1, :] + vec_ref[_LNB:_LNB + 1, :]
    o_ref[0, rows, :] = ((yn + buf["bonus"]) * buf["gate"]).astype(o_ref.dtype)


def _run_interleaved(*gens):
    live = list(gens)
    while live:
        for gen in list(live):
            if next(gen, StopIteration) is StopIteration:
                live.remove(gen)


def _rwkv_kernel(z_ref, mu_ref, vec_ref, w2_ref, a2_ref, g2_ref, o_ref, st_ref, carry_ref,
                 *, d_rw, lora_wa, lora_g):
    @pl.when(pl.program_id(1) == 0)
    def _():
        st_ref[...] = jnp.zeros_like(st_ref)
        carry_ref[...] = jnp.zeros_like(carry_ref)

    n_pairs = d_rw // LANES
    sub = RWKV_SUB
    n_sub = z_ref.shape[1] // sub
    chunks = list(range(sub // RWKV_C))
    state = [st_ref[p] for p in range(n_pairs)]

    prev_row = carry_ref[0:1, :]
    bufs, prepares = [], []
    for i in range(n_sub):
        z = z_ref[0, i * sub:(i + 1) * sub, :].astype(F32)
        bufs.append({})
        prepares.append(_rwkv_prepare(z, prev_row, bufs[i], mu_ref, vec_ref, w2_ref, a2_ref, g2_ref,
                                      d_rw=d_rw, lora_wa=lora_wa, lora_g=lora_g))
        prev_row = z[sub - 1:sub, :]
    carry_ref[0:1, :] = prev_row

    _run_interleaved(prepares[0])
    for i in range(n_sub):
        y_rows = []
        delta = _rwkv_delta(bufs[i], chunks, state, y_rows, d_rw=d_rw)
        _run_interleaved(*([delta] + prepares[i + 1:i + 2]))
        _rwkv_finish(bufs[i], y_rows, vec_ref, o_ref, slice(i * sub, (i + 1) * sub))
    for p in range(n_pairs):
        st_ref[p] = state[p]


def _rwkv(z, zw, mu, vecs, w2p, a2p, g2p, d_rw, lora_wa, lora_g):
    bsz, seqlen, _ = z.shape
    lb = RWKV_LB
    const = lambda shape: pl.BlockSpec(shape, lambda b, j: (0,) * len(shape))
    params = (mu, vecs, w2p, a2p, g2p)
    return pl.pallas_call(
        functools.partial(_rwkv_kernel, d_rw=d_rw, lora_wa=lora_wa, lora_g=lora_g),
        grid=(bsz, seqlen // lb),
        in_specs=[pl.BlockSpec((1, lb, zw), lambda b, j: (b, j, 0))]
                 + [const(p.shape) for p in params],
        out_specs=pl.BlockSpec((1, lb, d_rw), lambda b, j: (b, j, 0)),
        out_shape=jax.ShapeDtypeStruct((bsz, seqlen, d_rw), BF16),
        scratch_shapes=[
            pltpu.VMEM((d_rw // LANES, LANES, LANES), F32),
            pltpu.VMEM((SUBLANES, zw), F32),
        ],
        compiler_params=pltpu.CompilerParams(
            dimension_semantics=("arbitrary", "arbitrary"),
            vmem_limit_bytes=VMEM_LIMIT),
        name="rwkv",
    )(z, *params)


def _out_ffn_kernel(x_ref, ys_ref, yr_ref, wo_ref, g_ref, wu_ref, wd_ref, o_ref, *, final_norm):
    x = x_ref[0]
    mix = jnp.concatenate([ys_ref[0], yr_ref[0]], axis=1)
    h = x + jnp.dot(mix, wo_ref[...], preferred_element_type=F32)
    ms = jnp.mean(h * h, axis=-1, keepdims=True)
    hn = (h * lax.rsqrt(ms + NORM_EPS) * g_ref[0:1, :]).astype(BF16)
    d_ff = wu_ref.shape[1]
    acc = h
    for j in range(d_ff // FFN_CHUNK):
        sl = slice(j * FFN_CHUNK, (j + 1) * FFN_CHUNK)
        up = jnp.dot(hn, wu_ref[:, sl], preferred_element_type=F32)
        up = jnp.square(jnp.maximum(up, 0.0)).astype(BF16)
        acc = acc + jnp.dot(up, wd_ref[sl, :], preferred_element_type=F32)
    if final_norm:
        ms2 = jnp.mean(acc * acc, axis=-1, keepdims=True)
        acc = acc * lax.rsqrt(ms2 + NORM_EPS) * g_ref[1:2, :]
    o_ref[0] = acc


def _out_ffn(x, y_s5, y_rw, w_out, gains, w_up, w_down, final_norm):
    bsz, seqlen, d_model = x.shape
    tm = FFN_TM
    const = lambda shape: pl.BlockSpec(shape, lambda b, i: (0,) * len(shape))
    act = lambda width: pl.BlockSpec((1, tm, width), lambda b, i: (b, i, 0))
    return pl.pallas_call(
        functools.partial(_out_ffn_kernel, final_norm=final_norm),
        grid=(bsz, seqlen // tm),
        in_specs=[
            act(d_model), act(y_s5.shape[2]), act(y_rw.shape[2]),
            const(w_out.shape), const(gains.shape), const(w_up.shape), const(w_down.shape),
        ],
        out_specs=act(d_model),
        out_shape=jax.ShapeDtypeStruct((bsz, seqlen, d_model), x.dtype),
        compiler_params=pltpu.CompilerParams(
            dimension_semantics=("arbitrary", "arbitrary"),
            vmem_limit_bytes=VMEM_LIMIT),
        name="out_ffn",
    )(x, y_s5, y_rw, w_out, gains, w_up, w_down)


def _round_up(n, m):
    return -(-n // m) * m


def _s5_weights(lam_re, lam_im, log_dt, b_re, b_im, c_re, c_im):
    groups, n_p = lam_re.shape
    n_h = b_re.shape[2]
    dt = jnp.exp(log_dt.astype(F32))[:, None]
    lr, li = lam_re.astype(F32), lam_im.astype(F32)
    mag = jnp.exp(lr * dt)
    a_re = mag * jnp.cos(li * dt)
    a_im = mag * jnp.sin(li * dt)
    den = lr * lr + li * li
    q_re = ((a_re - 1.0) * lr + a_im * li) / den
    q_im = (a_im * lr - (a_re - 1.0) * li) / den
    bb_re = q_re[:, :, None] * b_re - q_im[:, :, None] * b_im
    bb_im = q_re[:, :, None] * b_im + q_im[:, :, None] * b_re

    tile_groups = 2 * LANES // n_p
    k_groups = LANES // n_h
    n_tiles = groups // tile_groups
    place = np.zeros((n_tiles, k_groups, tile_groups), np.float32)
    for j in range(n_tiles):
        for g in range(tile_groups):
            place[j, (j * tile_groups + g) % k_groups, g] = 1.0

    def b_tiles(bb):
        t = jnp.einsum('jgph,jlg->jlhgp', bb.reshape(n_tiles, tile_groups, n_p, n_h), place)
        return t.reshape(n_tiles, LANES, 2 * LANES)

    wb = jnp.concatenate([b_tiles(bb_re), b_tiles(bb_im)], axis=0).astype(BF16)

    n_out = groups // k_groups
    eye = np.eye(k_groups, dtype=np.float32)

    def c_tiles(cc):
        t = jnp.einsum('mghp,gk->mgpkh', cc.astype(F32).reshape(n_out, k_groups, n_h, n_p), eye)
        return t.reshape(n_out, k_groups * n_p, LANES).astype(BF16)

    lam_bar = jnp.stack([a_re.reshape(-1), a_im.reshape(-1)])
    return wb, lam_bar, c_tiles(c_re), c_tiles(-c_im)


def kernel(x, norm_mix_g, w_in, lam_re, lam_im, log_dt, b_re, b_im, c_re, c_im, d_skip, w_glu, b_glu, s5_out_g, mu_shift, w0, w2, a0, a2, g2, k_k, k_a, r_k, ln_x_w, ln_x_b, w_out, norm_ffn_g, w_up, w_down, norm_final_g):
    depth, d_model, _ = w_in.shape
    d_s5 = d_skip.shape[1]
    d_rw = w0.shape[1]
    n_w, n_a, n_g = w2.shape[1], a2.shape[1], g2.shape[1]
    lora_wa = _round_up(n_w + n_a, LANES)
    lora_g = _round_up(n_g, LANES)
    d_rw_in = 3 * d_rw + n_w + n_a + n_g
    zw = _round_up(3 * d_rw + lora_wa + lora_g, d_s5)
    assert n_w + n_a == lora_wa and zw >= d_rw_in

    h = x
    for i in range(depth):
        w_cat = jnp.concatenate(
            [w_in[i][:, d_s5:], jnp.zeros((d_model, zw - d_rw_in), w_in.dtype), w_in[i][:, :d_s5]],
            axis=1).astype(BF16)
        mu = jnp.pad(mu_shift[i].astype(F32), (0, zw - d_rw_in)).reshape(1, zw)
        w2p = jnp.pad(w2[i], ((0, lora_wa - n_w), (0, 0))).astype(BF16)
        a2p = jnp.pad(a2[i], ((n_w, lora_wa - n_w - n_a), (0, 0))).astype(BF16)
        g2p = jnp.pad(g2[i], ((0, lora_g - n_g), (0, 0))).astype(BF16)
        rw_vecs = jnp.stack([w0[i], a0[i], k_k[i], k_a[i], r_k[i].reshape(-1), ln_x_w[i], ln_x_b[i],
                             jnp.zeros_like(w0[i])]).astype(F32)
        s5_vecs = jnp.stack([d_skip[i], b_glu[i], s5_out_g[i]]).astype(F32)
        gains = jnp.stack([norm_ffn_g[i], norm_final_g]).astype(F32)

        z = _in_proj(h, norm_mix_g[i].reshape(1, -1).astype(F32), w_cat)

        wb, lam_bar, wc_re, wc_im = _s5_weights(
            lam_re[i], lam_im[i], log_dt[i], b_re[i], b_im[i], c_re[i], c_im[i])
        y_s5 = _s5(z, zw // d_s5, wb, lam_bar, wc_re, wc_im, s5_vecs, w_glu[i].astype(BF16))
        y_rw = _rwkv(z, zw, mu, rw_vecs, w2p, a2p, g2p, d_rw, lora_wa, lora_g)

        h = _out_ffn(h, y_s5, y_rw, w_out[i].astype(BF16), gains,
                     w_up[i].astype(BF16), w_down[i].astype(BF16), final_norm=(i == depth - 1))
    return h
```

```python
import functools
import math

import numpy as np
import jax
import jax.numpy as jnp
from jax import lax
from jax.experimental import pallas as pl
from jax.experimental.pallas import tpu as pltpu

F32 = jnp.float32
BF16 = jnp.bfloat16

NORM_EPS = 1e-6
GN_EPS = 64e-5

S5_GROUP = 16
S5_STATE = 64
RWKV_HEAD = 64
LANES = 128
SUBLANES = 8
VMEM_LIMIT = 56 * 1024 * 1024

IN_TM = 512
IN_TN = 512
S5_TC = 128
S5_STRIP = 512
RWKV_LB = 512
RWKV_SUB = 256
RWKV_C = 64
FFN_TM = 256
FFN_CHUNK = 1024

MM = (((1,), (0,)), ((), ()))
NT = (((1,), (1,)), ((), ()))
TN = (((0,), (0,)), ((), ()))


def _dot(a, b, dims=MM):
    return lax.dot_general(a.astype(BF16), b.astype(BF16), dims,
                           preferred_element_type=F32)


def _dot_split_rhs(a_bf16, b, passes):
    acc = None
    rem = b
    for _ in range(passes):
        part = rem.astype(BF16)
        term = jnp.dot(a_bf16, part, preferred_element_type=F32)
        acc = term if acc is None else acc + term
        rem = rem - part.astype(F32)
    return acc


def _in_proj_kernel(x_ref, g_ref, w_ref, z_ref):
    x = x_ref[0]
    ms = jnp.mean(x * x, axis=-1, keepdims=True)
    xn = (x * lax.rsqrt(ms + NORM_EPS) * g_ref[...]).astype(BF16)
    n = w_ref.shape[1]
    for j in range(n // IN_TN):
        sl = slice(j * IN_TN, (j + 1) * IN_TN)
        z_ref[0, :, sl] = jnp.dot(xn, w_ref[:, sl], preferred_element_type=F32).astype(BF16)


def _in_proj(x, g, w):
    bsz, seqlen, d_model = x.shape
    tm = IN_TM
    return pl.pallas_call(
        _in_proj_kernel,
        grid=(bsz, seqlen // tm),
        in_specs=[
            pl.BlockSpec((1, tm, d_model), lambda b, i: (b, i, 0)),
            pl.BlockSpec((1, d_model), lambda b, i: (0, 0)),
            pl.BlockSpec(w.shape, lambda b, i: (0, 0)),
        ],
        out_specs=pl.BlockSpec((1, tm, w.shape[1]), lambda b, i: (b, i, 0)),
        out_shape=jax.ShapeDtypeStruct((bsz, seqlen, w.shape[1]), BF16),
        compiler_params=pltpu.CompilerParams(
            dimension_semantics=("arbitrary", "arbitrary"),
            vmem_limit_bytes=VMEM_LIMIT),
        name="in_proj",
    )(x, g, w)


def _s5_kernel(u_ref, wb_ref, lam_ref, wcre_ref, wcim_ref, vec_ref, wglu_ref, o_ref,
               st_ref, carry_ref, io_ref):
    bsz, tc, d_s5 = u_ref.shape
    n_state = lam_ref.shape[1]
    n_slabs_half = n_state // LANES
    n_io = d_s5 // LANES
    d_skip, b_glu, g_out = vec_ref[0:1, :], vec_ref[1:2, :], vec_ref[2:3, :]

    @pl.when(pl.program_id(0) == 0)
    def _():
        carry_ref[...] = jnp.zeros_like(carry_ref)

    for b in range(bsz):
        u_b = u_ref[b].astype(F32)
        for q in range(n_io):
            io_ref[q, pl.ds(b, tc, stride=bsz), :] = u_b[:, q * LANES:(q + 1) * LANES]
    u_f32 = jnp.concatenate([io_ref[q] for q in range(n_io)], axis=1)
    u_bf = u_f32.astype(BF16)

    n_tiles = wb_ref.shape[0]
    tile_w = wb_ref.shape[2]
    tiles_per_half = n_tiles // 2
    slabs_per_tile = tile_w // LANES
    slabs_per_strip = S5_STRIP // LANES
    tiles_per_strip = slabs_per_strip // slabs_per_tile
    n_strips = n_slabs_half // slabs_per_strip
    assert wcre_ref.shape[0] == n_strips and wcre_ref.shape[1] == S5_STRIP

    def b_tile(j):
        kb = (j % tiles_per_half) * tile_w // (S5_STATE // S5_GROUP * LANES)
        res = jnp.dot(u_bf[:, kb * LANES:(kb + 1) * LANES], wb_ref[j],
                      preferred_element_type=F32)
        for h in range(slabs_per_tile):
            st_ref[j * slabs_per_tile + h] = res[:, h * LANES:(h + 1) * LANES]

    def strip_tiles(s):
        re_tiles = [s * tiles_per_strip + q for q in range(tiles_per_strip)]
        return re_tiles + [tiles_per_half + j for j in re_tiles]

    ys = [None] * n_strips

    def c_part(m, part):
        base = m * slabs_per_strip + (n_slabs_half if part else 0)
        xs = jnp.concatenate([st_ref[base + q] for q in range(slabs_per_strip)], axis=1).astype(BF16)
        res = jnp.dot(xs, (wcim_ref if part else wcre_ref)[m], preferred_element_type=F32)
        ys[m] = res if ys[m] is None else ys[m] + res

    def scan(s, n_yields):
        re_ids = [s * slabs_per_strip + q for q in range(slabs_per_strip)]
        im_ids = [n_slabs_half + i for i in re_ids]
        a_re = [jnp.broadcast_to(lam_ref[0:1, i * LANES:(i + 1) * LANES], (bsz, LANES)) for i in re_ids]
        a_im = [jnp.broadcast_to(lam_ref[1:2, i * LANES:(i + 1) * LANES], (bsz, LANES)) for i in re_ids]
        s_re = [carry_ref[i] for i in re_ids]
        s_im = [carry_ref[i] for i in im_ids]
        every = max(tc // max(n_yields, 1), 1)
        for t in range(tc):
            rows = slice(t * bsz, (t + 1) * bsz)
            n_re = [a_re[q] * s_re[q] - a_im[q] * s_im[q] + st_ref[re_ids[q], rows, :]
                    for q in range(slabs_per_strip)]
            n_im = [a_re[q] * s_im[q] + a_im[q] * s_re[q] + st_ref[im_ids[q], rows, :]
                    for q in range(slabs_per_strip)]
            for q in range(slabs_per_strip):
                st_ref[re_ids[q], rows, :] = n_re[q]
                st_ref[im_ids[q], rows, :] = n_im[q]
            s_re, s_im = n_re, n_im
            if t % every == every - 1:
                yield
        for q in range(slabs_per_strip):
            carry_ref[re_ids[q]] = s_re[q]
            carry_ref[im_ids[q]] = s_im[q]

    def mxu_pieces(fns):
        for fn in fns:
            fn()
            yield

    for j in strip_tiles(0):
        b_tile(j)
    for s in range(n_strips):
        fns = []
        if s + 1 < n_strips:
            fns += [functools.partial(b_tile, j) for j in strip_tiles(s + 1)]
        if s >= 1:
            fns += [functools.partial(c_part, s - 1, part) for part in (0, 1)]
        _run_interleaved(scan(s, len(fns)), mxu_pieces(fns))
    for part in (0, 1):
        c_part(n_strips - 1, part)

    y = jnp.concatenate(ys, axis=-1) + d_skip * u_f32
    g = 0.5 * y * (1.0 + lax.erf(y * (1.0 / math.sqrt(2.0))))
    gate = jax.nn.sigmoid(
        jnp.dot(g.astype(BF16), wglu_ref[...], preferred_element_type=F32) + b_glu)
    out = g * gate
    ms = jnp.mean(out * out, axis=-1, keepdims=True)
    out = out * lax.rsqrt(ms + NORM_EPS) * g_out
    for q in range(n_io):
        io_ref[q] = out[:, q * LANES:(q + 1) * LANES]
    for b in range(bsz):
        o_ref[b] = jnp.concatenate(
            [io_ref[q, pl.ds(b, tc, stride=bsz), :] for q in range(n_io)], axis=1).astype(o_ref.dtype)


def _s5(z, col_block, wb, lam_bar, wc_re, wc_im, vecs, w_glu):
    bsz, seqlen, _ = z.shape
    d_s5 = vecs.shape[1]
    tc = S5_TC
    n_state = lam_bar.shape[1]
    const = lambda shape: pl.BlockSpec(shape, lambda i: (0,) * len(shape))
    return pl.pallas_call(
        _s5_kernel,
        grid=(seqlen // tc,),
        in_specs=[
            pl.BlockSpec((bsz, tc, d_s5), lambda i: (0, i, col_block)),
            const(wb.shape), const(lam_bar.shape), const(wc_re.shape), const(wc_im.shape),
            const(vecs.shape), const(w_glu.shape),
        ],
        out_specs=pl.BlockSpec((bsz, tc, d_s5), lambda i: (0, i, 0)),
        out_shape=jax.ShapeDtypeStruct((bsz, seqlen, d_s5), BF16),
        scratch_shapes=[
            pltpu.VMEM((2 * n_state // LANES, bsz * tc, LANES), F32),
            pltpu.VMEM((2 * n_state // LANES, bsz, LANES), F32),
            pltpu.VMEM((d_s5 // LANES, bsz * tc, LANES), F32),
        ],
        compiler_params=pltpu.CompilerParams(
            dimension_semantics=("arbitrary",),
            vmem_limit_bytes=VMEM_LIMIT),
        name="s5",
    )(z, wb, lam_bar, wc_re, wc_im, vecs, w_glu)


_W0, _A0, _KK, _KA, _RK, _LNW, _LNB = range(7)


def _seg_sum(x, ones_bd):
    w = ones_bd.shape[0]
    outs = [_dot(x[:, i * w:(i + 1) * w], ones_bd) for i in range(x.shape[1] // w)]
    return jnp.concatenate(outs, axis=-1)


def _head_ones(hd):
    seg_w = 2 * LANES
    li = lax.broadcasted_iota(jnp.int32, (seg_w, seg_w), 0) // hd
    lj = lax.broadcasted_iota(jnp.int32, (seg_w, seg_w), 1) // hd
    return jnp.where(li == lj, 1.0, 0.0).astype(BF16)


def _rwkv_prepare(z, prev_row, buf, mu_ref, vec_ref, w2_ref, a2_ref, g2_ref, *, d_rw, lora_wa, lora_g):
    lb = z.shape[0]
    c = RWKV_C
    hd = RWKV_HEAD
    n_pairs = d_rw // LANES
    vec = lambda i: vec_ref[i:i + 1, :]

    row = lax.broadcasted_iota(jnp.int32, z.shape, 0)
    z_prev = jnp.where(row == 0, prev_row, pltpu.roll(z, 1, 0))
    zz = z + (z_prev - z) * mu_ref[...]
    yield

    o = 3 * d_rw
    r = zz[:, 0:d_rw]
    k = zz[:, d_rw:2 * d_rw]
    v = zz[:, 2 * d_rw:o]
    zwa = zz[:, o:o + lora_wa]
    zg = zz[:, o + lora_wa:o + lora_wa + lora_g]

    wpre = vec(_W0) + _dot(jnp.tanh(zwa), w2_ref[...])
    logd = -math.exp(-0.5) * jax.nn.sigmoid(wpre)
    a = jax.nn.sigmoid(vec(_A0) + _dot(zwa, a2_ref[...]))
    yield
    for name in ("lhs1", "rhs1", "v_bd", "kk_bd", "kb_end", "b_end", "q_t", "d_end"):
        buf[name] = {}
    buf["gate"] = _dot(jax.nn.sigmoid(zg), g2_ref[...])
    yield

    ones_bd = _head_ones(hd)
    kk = k * vec(_KK)
    kk = kk * lax.rsqrt(jnp.maximum(_seg_sum(kk * kk, ones_bd), 1e-24))
    yield
    k2 = k * (1.0 + (a - 1.0) * vec(_KA))
    buf["bonus"] = _seg_sum(r * k2 * vec(_RK), ones_bd) * v
    bvec = kk * a
    yield

    ci = lax.broadcasted_iota(jnp.int32, (c, c), 0)
    cj = lax.broadcasted_iota(jnp.int32, (c, c), 1)
    ltri = jnp.where(cj <= ci, 1.0, 0.0).astype(BF16)
    head0 = lax.broadcasted_iota(jnp.int32, (c, LANES), 1) < hd

    def stack_heads(x):
        return jnp.concatenate([jnp.where(head0, x, 0.0), jnp.where(head0, 0.0, x)], axis=0)

    for ic in range(lb // c):
        rows = slice(ic * c, (ic + 1) * c)
        logd_c = logd[rows]
        cum = _dot_split_rhs(ltri, logd_c, 3)
        cum_end = cum[c - 1:c, :]
        e_inv = jnp.exp(-cum)
        e_end = jnp.exp(cum_end - cum)
        q_c = r[rows] * jnp.exp(cum)
        kk_c = kk[rows] * jnp.exp(cum - logd_c)
        k_hat = k2[rows] * e_inv
        b_hat = bvec[rows] * e_inv
        k_end = k2[rows] * e_end
        b_end = bvec[rows] * e_end
        d_c = jnp.exp(cum_end)
        for p in range(n_pairs):
            ls = slice(p * LANES, (p + 1) * LANES)
            key = (ic, p)
            b_end_bd = stack_heads(b_end[:, ls])
            buf["q_t"][key] = q_c[:, ls]
            buf["d_end"][key] = d_c[:, ls]
            buf["kk_bd"][key] = stack_heads(kk_c[:, ls]).astype(BF16)
            buf["v_bd"][key] = stack_heads(v[rows, ls]).astype(BF16)
            buf["b_end"][key] = b_end_bd.astype(BF16)
            buf["lhs1"][key] = jnp.concatenate([q_c[:, ls], kk_c[:, ls]], axis=0).astype(BF16)
            buf["rhs1"][key] = jnp.concatenate(
                [stack_heads(k_hat[:, ls]), stack_heads(b_hat[:, ls])], axis=0).astype(BF16)
            buf["kb_end"][key] = jnp.concatenate(
                [stack_heads(k_end[:, ls]), b_end_bd], axis=0).astype(BF16)
            yield


def _rwkv_delta(buf, chunk_ids, state, y_rows, *, d_rw):
    c = RWKV_C
    hd = RWKV_HEAD
    n_pairs = d_rw // LANES
    chains = [(ic, p) for ic in chunk_ids for p in range(n_pairs)]

    ri = lax.broadcasted_iota(jnp.int32, (c, LANES), 0)
    rl = lax.broadcasted_iota(jnp.int32, (c, LANES), 1)
    head0 = rl < hd
    incl = (rl % hd) <= ri
    strict = (rl % hd) < ri
    pi = lax.broadcasted_iota(jnp.int32, (2 * c, 2 * c), 0)
    pj = lax.broadcasted_iota(jnp.int32, (2 * c, 2 * c), 1)
    level = pi ^ pj
    diag = pi == pj
    eye = jnp.where(diag, 1.0, 0.0)
    n_levels = int(math.log2(c))

    def stack_heads(x):
        return jnp.concatenate([jnp.where(head0, x, 0.0), jnp.where(head0, 0.0, x)], axis=0)

    p1 = {k_: lax.dot_general(buf["lhs1"][k_], buf["rhs1"][k_], NT,
                              preferred_element_type=F32) for k_ in chains}
    a_rk = {k_: jnp.where(incl, p1[k_][0:c, 0:LANES], 0.0).astype(BF16) for k_ in chains}
    a_rb = {k_: jnp.where(incl, p1[k_][0:c, LANES:2 * LANES], 0.0).astype(BF16) for k_ in chains}
    a_kk = {k_: stack_heads(jnp.where(strict, p1[k_][c:2 * c, 0:LANES], 0.0)).astype(BF16)
            for k_ in chains}
    n_bd = {k_: stack_heads(jnp.where(strict, p1[k_][c:2 * c, LANES:2 * LANES], 0.0)) for k_ in chains}

    t_inv = {k_: eye - jnp.where(level == 1, n_bd[k_], 0.0) for k_ in chains}
    yield
    for lg in range(1, n_levels):
        m = 1 << lg
        starts = list(range(m, 2 * c, 2 * m))
        take = (lambda x: x) if m < SUBLANES else (
            lambda x: jnp.concatenate([x[s:s + m] for s in starts], axis=0))
        t_bf = {k_: t_inv[k_].astype(BF16) for k_ in chains}
        nt = {k_: _dot(take(jnp.where((level >> lg) == 1, n_bd[k_], 0.0)), t_bf[k_]) for k_ in chains}
        yield
        if m < SUBLANES:
            t_inv = {k_: t_inv[k_] - _dot(t_bf[k_], nt[k_]) for k_ in chains}
        else:
            zero = jnp.zeros((m, LANES), F32)
            for k_ in chains:
                nt_full = jnp.concatenate(
                    [piece for i in range(len(starts)) for piece in (zero, nt[k_][i * m:(i + 1) * m])], axis=0)
                t_r = take(t_inv[k_])
                t_r = t_r - _dot(t_r, nt_full)
                t_inv[k_] = jnp.concatenate(
                    [piece for i, s in enumerate(starts)
                     for piece in (t_inv[k_][s - m:s], t_r[i * m:(i + 1) * m])], axis=0)
        yield
    t_bf = {k_: t_inv[k_].astype(BF16) for k_ in chains}

    av = {k_: _dot(a_kk[k_], buf["v_bd"][k_]).astype(BF16) for k_ in chains}
    yield
    w12 = {k_: _dot(t_bf[k_], jnp.concatenate([buf["kk_bd"][k_], av[k_]], axis=1))
           for k_ in chains}
    w1 = {k_: w12[k_][:, 0:LANES].astype(BF16) for k_ in chains}
    vw = {k_: jnp.concatenate([buf["v_bd"][k_], (-w12[k_][:, LANES:2 * LANES]).astype(BF16)], axis=0)
          for k_ in chains}
    yield

    y_intra = {k_: _dot(jnp.concatenate([a_rk[k_], a_rb[k_]], axis=1), vw[k_]) for k_ in chains}
    yield
    q_eff = {k_: (buf["q_t"][k_] - _dot(a_rb[k_], w1[k_])).astype(BF16) for k_ in chains}
    yield
    g_mat = {k_: _dot(buf["kb_end"][k_], vw[k_], TN) for k_ in chains}
    yield
    m_mat = {k_: (jnp.where(diag, jnp.broadcast_to(buf["d_end"][k_], (2 * c, LANES)), 0.0)
                  - _dot(buf["b_end"][k_], w1[k_], TN)).astype(BF16) for k_ in chains}
    yield

    for ic in chunk_ids:
        y_c = [_dot(q_eff[(ic, p)], state[p]) + y_intra[(ic, p)] for p in range(n_pairs)]
        state[:] = [_dot(m_mat[(ic, p)], state[p]) + g_mat[(ic, p)] for p in range(n_pairs)]
        y_rows.append(jnp.concatenate(y_c, axis=1))
        yield


def _rwkv_finish(buf, y_rows, vec_ref, o_ref, rows):
    hd = RWKV_HEAD
    y = jnp.concatenate(y_rows, axis=0)
    ones_bd = _head_ones(hd)
    mean = _seg_sum(y, ones_bd) * (1.0 / hd)
    yc = y - mean
    var = _seg_sum(yc * yc, ones_bd) * (1.0 / hd)
    yn = yc * lax.rsqrt(var + GN_EPS) * vec_ref[_LNW:_LNW + 1, :] + vec_ref[_LNB:_LNB + 1, :]
    o_ref[0, rows, :] = ((yn + buf["bonus"]) * buf["gate"]).astype(o_ref.dtype)


def _run_interleaved(*gens):
    live = list(gens)
    while live:
        for gen in list(live):
            if next(gen, StopIteration) is StopIteration:
                live.remove(gen)


def _rwkv_kernel(z_ref, mu_ref, vec_ref, w2_ref, a2_ref, g2_ref, o_ref, st_ref, carry_ref,
                 *, d_rw, lora_wa, lora_g):
    @pl.when(pl.program_id(1) == 0)
    def _():
        st_ref[...] = jnp.zeros_like(st_ref)
        carry_ref[...] = jnp.zeros_like(carry_ref)

    n_pairs = d_rw // LANES
    sub = RWKV_SUB
    n_sub = z_ref.shape[1] // sub
    chunks = list(range(sub // RWKV_C))
    state = [st_ref[p] for p in range(n_pairs)]

    prev_row = carry_ref[0:1, :]
    bufs, prepares = [], []
    for i in range(n_sub):
        z = z_ref[0, i * sub:(i + 1) * sub, :].astype(F32)
        bufs.append({})
        prepares.append(_rwkv_prepare(z, prev_row, bufs[i], mu_ref, vec_ref, w2_ref, a2_ref, g2_ref,
                                      d_rw=d_rw, lora_wa=lora_wa, lora_g=lora_g))
        prev_row = z[sub - 1:sub, :]
    carry_ref[0:1, :] = prev_row

    _run_interleaved(prepares[0])
    for i in range(n_sub):
        y_rows = []
        delta = _rwkv_delta(bufs[i], chunks, state, y_rows, d_rw=d_rw)
        _run_interleaved(*([delta] + prepares[i + 1:i + 2]))
        _rwkv_finish(bufs[i], y_rows, vec_ref, o_ref, slice(i * sub, (i + 1) * sub))
    for p in range(n_pairs):
        st_ref[p] = state[p]


def _rwkv(z, zw, mu, vecs, w2p, a2p, g2p, d_rw, lora_wa, lora_g):
    bsz, seqlen, _ = z.shape
    lb = RWKV_LB
    const = lambda shape: pl.BlockSpec(shape, lambda b, j: (0,) * len(shape))
    params = (mu, vecs, w2p, a2p, g2p)
    return pl.pallas_call(
        functools.partial(_rwkv_kernel, d_rw=d_rw, lora_wa=lora_wa, lora_g=lora_g),
        grid=(bsz, seqlen // lb),
        in_specs=[pl.BlockSpec((1, lb, zw), lambda b, j: (b, j, 0))]
                 + [const(p.shape) for p in params],
        out_specs=pl.BlockSpec((1, lb, d_rw), lambda b, j: (b, j, 0)),
        out_shape=jax.ShapeDtypeStruct((bsz, seqlen, d_rw), BF16),
        scratch_shapes=[
            pltpu.VMEM((d_rw // LANES, LANES, LANES), F32),
            pltpu.VMEM((SUBLANES, zw), F32),
        ],
        compiler_params=pltpu.CompilerParams(
            dimension_semantics=("arbitrary", "arbitrary"),
            vmem_limit_bytes=VMEM_LIMIT),
        name="rwkv",
    )(z, *params)


def _out_ffn_kernel(x_ref, ys_ref, yr_ref, wo_ref, g_ref, wu_ref, wd_ref, o_ref, *, final_norm):
    x = x_ref[0]
    mix = jnp.concatenate([ys_ref[0], yr_ref[0]], axis=1)
    h = x + jnp.dot(mix, wo_ref[...], preferred_element_type=F32)
    ms = jnp.mean(h * h, axis=-1, keepdims=True)
    hn = (h * lax.rsqrt(ms + NORM_EPS) * g_ref[0:1, :]).astype(BF16)
    d_ff = wu_ref.shape[1]
    acc = h
    for j in range(d_ff // FFN_CHUNK):
        sl = slice(j * FFN_CHUNK, (j + 1) * FFN_CHUNK)
        up = jnp.dot(hn, wu_ref[:, sl], preferred_element_type=F32)
        up = jnp.square(jnp.maximum(up, 0.0)).astype(BF16)
        acc = acc + jnp.dot(up, wd_ref[sl, :], preferred_element_type=F32)
    if final_norm:
        ms2 = jnp.mean(acc * acc, axis=-1, keepdims=True)
        acc = acc * lax.rsqrt(ms2 + NORM_EPS) * g_ref[1:2, :]
    o_ref[0] = acc


def _out_ffn(x, y_s5, y_rw, w_out, gains, w_up, w_down, final_norm):
    bsz, seqlen, d_model = x.shape
    tm = FFN_TM
    const = lambda shape: pl.BlockSpec(shape, lambda b, i: (0,) * len(shape))
    act = lambda width: pl.BlockSpec((1, tm, width), lambda b, i: (b, i, 0))
    return pl.pallas_call(
        functools.partial(_out_ffn_kernel, final_norm=final_norm),
        grid=(bsz, seqlen // tm),
        in_specs=[
            act(d_model), act(y_s5.shape[2]), act(y_rw.shape[2]),
            const(w_out.shape), const(gains.shape), const(w_up.shape), const(w_down.shape),
        ],
        out_specs=act(d_model),
        out_shape=jax.ShapeDtypeStruct((bsz, seqlen, d_model), x.dtype),
        compiler_params=pltpu.CompilerParams(
            dimension_semantics=("arbitrary", "arbitrary"),
            vmem_limit_bytes=VMEM_LIMIT),
        name="out_ffn",
    )(x, y_s5, y_rw, w_out, gains, w_up, w_down)


def _round_up(n, m):
    return -(-n // m) * m


def _s5_weights(lam_re, lam_im, log_dt, b_re, b_im, c_re, c_im):
    groups, n_p = lam_re.shape
    n_h = b_re.shape[2]
    dt = jnp.exp(log_dt.astype(F32))[:, None]
    lr, li = lam_re.astype(F32), lam_im.astype(F32)
    mag = jnp.exp(lr * dt)
    a_re = mag * jnp.cos(li * dt)
    a_im = mag * jnp.sin(li * dt)
    den = lr * lr + li * li
    q_re = ((a_re - 1.0) * lr + a_im * li) / den
    q_im = (a_im * lr - (a_re - 1.0) * li) / den
    bb_re = q_re[:, :, None] * b_re - q_im[:, :, None] * b_im
    bb_im = q_re[:, :, None] * b_im + q_im[:, :, None] * b_re

    tile_groups = 2 * LANES // n_p
    k_groups = LANES // n_h
    n_tiles = groups // tile_groups
    place = np.zeros((n_tiles, k_groups, tile_groups), np.float32)
    for j in range(n_tiles):
        for g in range(tile_groups):
            place[j, (j * tile_groups + g) % k_groups, g] = 1.0

    def b_tiles(bb):
        t = jnp.einsum('jgph,jlg->jlhgp', bb.reshape(n_tiles, tile_groups, n_p, n_h), place)
        return t.reshape(n_tiles, LANES, 2 * LANES)

    wb = jnp.concatenate([b_tiles(bb_re), b_tiles(bb_im)], axis=0).astype(BF16)

    n_out = groups // k_groups
    eye = np.eye(k_groups, dtype=np.float32)

    def c_tiles(cc):
        t = jnp.einsum('mghp,gk->mgpkh', cc.astype(F32).reshape(n_out, k_groups, n_h, n_p), eye)
        return t.reshape(n_out, k_groups * n_p, LANES).astype(BF16)

    lam_bar = jnp.stack([a_re.reshape(-1), a_im.reshape(-1)])
    return wb, lam_bar, c_tiles(c_re), c_tiles(-c_im)


def kernel(x, norm_mix_g, w_in, lam_re, lam_im, log_dt, b_re, b_im, c_re, c_im, d_skip, w_glu, b_glu, s5_out_g, mu_shift, w0, w2, a0, a2, g2, k_k, k_a, r_k, ln_x_w, ln_x_b, w_out, norm_ffn_g, w_up, w_down, norm_final_g):
    depth, d_model, _ = w_in.shape
    d_s5 = d_skip.shape[1]
    d_rw = w0.shape[1]
    n_w, n_a, n_g = w2.shape[1], a2.shape[1], g2.shape[1]
    lora_wa = _round_up(n_w + n_a, LANES)
    lora_g = _round_up(n_g, LANES)
    d_rw_in = 3 * d_rw + n_w + n_a + n_g
    zw = _round_up(3 * d_rw + lora_wa + lora_g, d_s5)
    assert n_w + n_a == lora_wa and zw >= d_rw_in

    h = x
    for i in range(depth):
        w_cat = jnp.concatenate(
            [w_in[i][:, d_s5:], jnp.zeros((d_model, zw - d_rw_in), w_in.dtype), w_in[i][:, :d_s5]],
            axis=1).astype(BF16)
        mu = jnp.pad(mu_shift[i].astype(F32), (0, zw - d_rw_in)).reshape(1, zw)
        w2p = jnp.pad(w2[i], ((0, lora_wa - n_w), (0, 0))).astype(BF16)
        a2p = jnp.pad(a2[i], ((n_w, lora_wa - n_w - n_a), (0, 0))).astype(BF16)
        g2p = jnp.pad(g2[i], ((0, lora_g - n_g), (0, 0))).astype(BF16)
        rw_vecs = jnp.stack([w0[i], a0[i], k_k[i], k_a[i], r_k[i].reshape(-1), ln_x_w[i], ln_x_b[i],
                             jnp.zeros_like(w0[i])]).astype(F32)
        s5_vecs = jnp.stack([d_skip[i], b_glu[i], s5_out_g[i]]).astype(F32)
        gains = jnp.stack([norm_ffn_g[i], norm_final_g]).astype(F32)

        z = _in_proj(h, norm_mix_g[i].reshape(1, -1).astype(F32), w_cat)

        wb, lam_bar, wc_re, wc_im = _s5_weights(
            lam_re[i], lam_im[i], log_dt[i], b_re[i], b_im[i], c_re[i], c_im[i])
        y_s5 = _s5(z, zw // d_s5, wb, lam_bar, wc_re, wc_im, s5_vecs, w_glu[i].astype(BF16))
        y_rw = _rwkv(z, zw, mu, rw_vecs, w2p, a2p, g2p, d_rw, lora_wa, lora_g)

        h = _out_ffn(h, y_s5, y_rw, w_out[i].astype(BF16), gains,
                     w_up[i].astype(BF16), w_down[i].astype(BF16), final_norm=(i == depth - 1))
    return h
```

```python
import functools
import math

import numpy as np
import jax
import jax.numpy as jnp
from jax import lax
from jax.experimental import pallas as pl
from jax.experimental.pallas import tpu as pltpu

F32 = jnp.float32
BF16 = jnp.bfloat16

NORM_EPS = 1e-6
GN_EPS = 64e-5

S5_GROUP = 16
S5_STATE = 64
RWKV_HEAD = 64
LANES = 128
SUBLANES = 8
VMEM_LIMIT = 56 * 1024 * 1024

IN_TM = 512
IN_TN = 512
S5_TC = 128
S5_STRIP = 512
RWKV_LB = 512
RWKV_SUB = 256
RWKV_C = 64
FFN_TM = 512
FFN_CHUNK = 1024

MM = (((1,), (0,)), ((), ()))
NT = (((1,), (1,)), ((), ()))
TN = (((0,), (0,)), ((), ()))


def _dot(a, b, dims=MM):
    return lax.dot_general(a.astype(BF16), b.astype(BF16), dims,
                           preferred_element_type=F32)


def _dot_split_rhs(a_bf16, b, passes):
    acc = None
    rem = b
    for _ in range(passes):
        part = rem.astype(BF16)
        term = jnp.dot(a_bf16, part, preferred_element_type=F32)
        acc = term if acc is None else acc + term
        rem = rem - part.astype(F32)
    return acc


def _in_proj_kernel(x_ref, g_ref, *refs):
    w_refs, z_ref = refs[:-1], refs[-1]
    x = x_ref[0]
    ms = jnp.mean(x * x, axis=-1, keepdims=True)
    xn = (x * lax.rsqrt(ms + NORM_EPS) * g_ref[...]).astype(BF16)
    col = 0
    for w_ref in w_refs:
        n = w_ref.shape[1]
        for j0 in range(0, n, IN_TN):
            j1 = min(j0 + IN_TN, n)
            z_ref[0, :, col + j0:col + j1] = jnp.dot(
                xn, w_ref[:, j0:j1], preferred_element_type=F32).astype(BF16)
        col += n


def _in_proj(x, g, weights):
    bsz, seqlen, d_model = x.shape
    tm = IN_TM
    width = sum(w.shape[1] for w in weights)
    return pl.pallas_call(
        _in_proj_kernel,
        grid=(bsz, seqlen // tm),
        in_specs=[
            pl.BlockSpec((1, tm, d_model), lambda b, i: (b, i, 0)),
            pl.BlockSpec((1, d_model), lambda b, i: (0, 0)),
        ] + [pl.BlockSpec(w.shape, lambda b, i: (0, 0)) for w in weights],
        out_specs=pl.BlockSpec((1, tm, width), lambda b, i: (b, i, 0)),
        out_shape=jax.ShapeDtypeStruct((bsz, seqlen, width), BF16),
        compiler_params=pltpu.CompilerParams(
            dimension_semantics=("arbitrary", "arbitrary"),
            vmem_limit_bytes=VMEM_LIMIT),
        name="in_proj",
    )(x, g, *weights)


def _s5_kernel(u_ref, wb_ref, lam_ref, wcre_ref, wcim_ref, vec_ref, wglu_ref, o_ref,
               st_ref, carry_ref, io_ref):
    bsz, tc, d_s5 = u_ref.shape
    n_state = lam_ref.shape[1]
    n_slabs_half = n_state // LANES
    n_io = d_s5 // LANES
    d_skip, b_glu, g_out = vec_ref[0:1, :], vec_ref[1:2, :], vec_ref[2:3, :]

    @pl.when(pl.program_id(0) == 0)
    def _():
        carry_ref[...] = jnp.zeros_like(carry_ref)

    for b in range(bsz):
        u_b = u_ref[b].astype(F32)
        for q in range(n_io):
            io_ref[q, pl.ds(b, tc, stride=bsz), :] = u_b[:, q * LANES:(q + 1) * LANES]
    u_f32 = jnp.concatenate([io_ref[q] for q in range(n_io)], axis=1)
    u_bf = u_f32.astype(BF16)

    n_tiles = wb_ref.shape[0]
    tile_w = wb_ref.shape[2]
    tiles_per_half = n_tiles // 2
    slabs_per_tile = tile_w // LANES
    slabs_per_strip = S5_STRIP // LANES
    tiles_per_strip = slabs_per_strip // slabs_per_tile
    n_strips = n_slabs_half // slabs_per_strip
    assert wcre_ref.shape[0] == n_strips and wcre_ref.shape[1] == S5_STRIP

    def b_tile(j):
        kb = (j % tiles_per_half) * tile_w // (S5_STATE // S5_GROUP * LANES)
        res = jnp.dot(u_bf[:, kb * LANES:(kb + 1) * LANES], wb_ref[j],
                      preferred_element_type=F32)
        for h in range(slabs_per_tile):
            st_ref[j * slabs_per_tile + h] = res[:, h * LANES:(h + 1) * LANES]

    def strip_tiles(s):
        re_tiles = [s * tiles_per_strip + q for q in range(tiles_per_strip)]
        return re_tiles + [tiles_per_half + j for j in re_tiles]

    ys = [None] * n_strips

    def c_part(m, part):
        base = m * slabs_per_strip + (n_slabs_half if part else 0)
        xs = jnp.concatenate([st_ref[base + q] for q in range(slabs_per_strip)], axis=1).astype(BF16)
        res = jnp.dot(xs, (wcim_ref if part else wcre_ref)[m], preferred_element_type=F32)
        ys[m] = res if ys[m] is None else ys[m] + res

    def scan(s, n_yields):
        re_ids = [s * slabs_per_strip + q for q in range(slabs_per_strip)]
        im_ids = [n_slabs_half + i for i in re_ids]
        a_re = [jnp.broadcast_to(lam_ref[0:1, i * LANES:(i + 1) * LANES], (bsz, LANES)) for i in re_ids]
        a_im = [jnp.broadcast_to(lam_ref[1:2, i * LANES:(i + 1) * LANES], (bsz, LANES)) for i in re_ids]
        s_re = [carry_ref[i] for i in re_ids]
        s_im = [carry_ref[i] for i in im_ids]
        every = max(tc // max(n_yields, 1), 1)
        for t in range(tc):
            rows = slice(t * bsz, (t + 1) * bsz)
            n_re = [a_re[q] * s_re[q] - a_im[q] * s_im[q] + st_ref[re_ids[q], rows, :]
                    for q in range(slabs_per_strip)]
            n_im = [a_re[q] * s_im[q] + a_im[q] * s_re[q] + st_ref[im_ids[q], rows, :]
                    for q in range(slabs_per_strip)]
            for q in range(slabs_per_strip):
                st_ref[re_ids[q], rows, :] = n_re[q]
                st_ref[im_ids[q], rows, :] = n_im[q]
            s_re, s_im = n_re, n_im
            if t % every == every - 1:
                yield
        for q in range(slabs_per_strip):
            carry_ref[re_ids[q]] = s_re[q]
            carry_ref[im_ids[q]] = s_im[q]

    def mxu_pieces(fns):
        for fn in fns:
            fn()
            yield

    for j in strip_tiles(0):
        b_tile(j)
    for s in range(n_strips):
        fns = []
        if s + 1 < n_strips:
            fns += [functools.partial(b_tile, j) for j in strip_tiles(s + 1)]
        if s >= 1:
            fns += [functools.partial(c_part, s - 1, part) for part in (0, 1)]
        _run_interleaved(scan(s, len(fns)), mxu_pieces(fns))
    for part in (0, 1):
        c_part(n_strips - 1, part)

    y = jnp.concatenate(ys, axis=-1) + d_skip * u_f32
    g = 0.5 * y * (1.0 + lax.erf(y * (1.0 / math.sqrt(2.0))))
    gate = jax.nn.sigmoid(
        jnp.dot(g.astype(BF16), wglu_ref[...], preferred_element_type=F32) + b_glu)
    out = g * gate
    ms = jnp.mean(out * out, axis=-1, keepdims=True)
    out = out * lax.rsqrt(ms + NORM_EPS) * g_out
    for q in range(n_io):
        io_ref[q] = out[:, q * LANES:(q + 1) * LANES]
    for b in range(bsz):
        o_ref[b] = jnp.concatenate(
            [io_ref[q, pl.ds(b, tc, stride=bsz), :] for q in range(n_io)], axis=1).astype(o_ref.dtype)


def _s5(z, col_block, wb, lam_bar, wc_re, wc_im, vecs, w_glu):
    bsz, seqlen, _ = z.shape
    d_s5 = vecs.shape[1]
    tc = S5_TC
    n_state = lam_bar.shape[1]
    const = lambda shape: pl.BlockSpec(shape, lambda i: (0,) * len(shape))
    return pl.pallas_call(
        _s5_kernel,
        grid=(seqlen // tc,),
        in_specs=[
            pl.BlockSpec((bsz, tc, d_s5), lambda i: (0, i, col_block)),
            const(wb.shape), const(lam_bar.shape), const(wc_re.shape), const(wc_im.shape),
            const(vecs.shape), const(w_glu.shape),
        ],
        out_specs=pl.BlockSpec((bsz, tc, d_s5), lambda i: (0, i, 0)),
        out_shape=jax.ShapeDtypeStruct((bsz, seqlen, d_s5), BF16),
        scratch_shapes=[
            pltpu.VMEM((2 * n_state // LANES, bsz * tc, LANES), F32),
            pltpu.VMEM((2 * n_state // LANES, bsz, LANES), F32),
            pltpu.VMEM((d_s5 // LANES, bsz * tc, LANES), F32),
        ],
        compiler_params=pltpu.CompilerParams(
            dimension_semantics=("arbitrary",),
            vmem_limit_bytes=VMEM_LIMIT),
        name="s5",
    )(z, wb, lam_bar, wc_re, wc_im, vecs, w_glu)


_W0, _A0, _KK, _KA, _RK, _LNW, _LNB = range(7)


def _seg_sum(x, ones_bd):
    w = ones_bd.shape[0]
    outs = [_dot(x[:, i * w:(i + 1) * w], ones_bd) for i in range(x.shape[1] // w)]
    return jnp.concatenate(outs, axis=-1)


def _head_ones(hd):
    seg_w = 2 * LANES
    li = lax.broadcasted_iota(jnp.int32, (seg_w, seg_w), 0) // hd
    lj = lax.broadcasted_iota(jnp.int32, (seg_w, seg_w), 1) // hd
    return jnp.where(li == lj, 1.0, 0.0).astype(BF16)


def _rwkv_prepare(z, prev_row, buf, mu_ref, vec_ref, w2_ref, a2_ref, g2_ref, *, d_rw, lora_wa, lora_g):
    lb = z.shape[0]
    c = RWKV_C
    hd = RWKV_HEAD
    n_pairs = d_rw // LANES
    vec = lambda i: vec_ref[i:i + 1, :]

    row = lax.broadcasted_iota(jnp.int32, z.shape, 0)
    z_prev = jnp.where(row == 0, prev_row, pltpu.roll(z, 1, 0))
    zz = z + (z_prev - z) * mu_ref[...]
    yield

    o = 3 * d_rw
    r = zz[:, 0:d_rw]
    k = zz[:, d_rw:2 * d_rw]
    v = zz[:, 2 * d_rw:o]
    zwa = zz[:, o:o + lora_wa]
    zg = zz[:, o + lora_wa:o + lora_wa + lora_g]

    wpre = vec(_W0) + _dot(jnp.tanh(zwa), w2_ref[...])
    logd = -math.exp(-0.5) * jax.nn.sigmoid(wpre)
    a = jax.nn.sigmoid(vec(_A0) + _dot(zwa, a2_ref[...]))
    yield
    for name in ("lhs1", "rhs1", "v_bd", "kk_bd", "kb_end", "b_end", "q_t", "d_end"):
        buf[name] = {}
    buf["gate"] = _dot(jax.nn.sigmoid(zg), g2_ref[...])
    yield

    ones_bd = _head_ones(hd)
    kk = k * vec(_KK)
    kk = kk * lax.rsqrt(jnp.maximum(_seg_sum(kk * kk, ones_bd), 1e-24))
    yield
    k2 = k * (1.0 + (a - 1.0) * vec(_KA))
    buf["bonus"] = _seg_sum(r * k2 * vec(_RK), ones_bd) * v
    bvec = kk * a
    yield

    ci = lax.broadcasted_iota(jnp.int32, (c, c), 0)
    cj = lax.broadcasted_iota(jnp.int32, (c, c), 1)
    ltri = jnp.where(cj <= ci, 1.0, 0.0).astype(BF16)
    head0 = lax.broadcasted_iota(jnp.int32, (c, LANES), 1) < hd

    def stack_heads(x):
        return jnp.concatenate([jnp.where(head0, x, 0.0), jnp.where(head0, 0.0, x)], axis=0)

    for ic in range(lb // c):
        rows = slice(ic * c, (ic + 1) * c)
        logd_c = logd[rows]
        cum = _dot_split_rhs(ltri, logd_c, 3)
        cum_end = cum[c - 1:c, :]
        e_inv = jnp.exp(-cum)
        e_end = jnp.exp(cum_end - cum)
        q_c = r[rows] * jnp.exp(cum)
        kk_c = kk[rows] * jnp.exp(cum - logd_c)
        k_hat = k2[rows] * e_inv
        b_hat = bvec[rows] * e_inv
        k_end = k2[rows] * e_end
        b_end = bvec[rows] * e_end
        d_c = jnp.exp(cum_end)
        for p in range(n_pairs):
            ls = slice(p * LANES, (p + 1) * LANES)
            key = (ic, p)
            b_end_bd = stack_heads(b_end[:, ls])
            buf["q_t"][key] = q_c[:, ls]
            buf["d_end"][key] = d_c[:, ls]
            buf["kk_bd"][key] = stack_heads(kk_c[:, ls]).astype(BF16)
            buf["v_bd"][key] = stack_heads(v[rows, ls]).astype(BF16)
            buf["b_end"][key] = b_end_bd.astype(BF16)
            buf["lhs1"][key] = jnp.concatenate([q_c[:, ls], kk_c[:, ls]], axis=0).astype(BF16)
            buf["rhs1"][key] = jnp.concatenate(
                [stack_heads(k_hat[:, ls]), stack_heads(b_hat[:, ls])], axis=0).astype(BF16)
            buf["kb_end"][key] = jnp.concatenate(
                [stack_heads(k_end[:, ls]), b_end_bd], axis=0).astype(BF16)
            yield


def _rwkv_delta(buf, chunk_ids, state, y_rows, *, d_rw):
    c = RWKV_C
    hd = RWKV_HEAD
    n_pairs = d_rw // LANES
    chains = [(ic, p) for ic in chunk_ids for p in range(n_pairs)]

    ri = lax.broadcasted_iota(jnp.int32, (c, LANES), 0)
    rl = lax.broadcasted_iota(jnp.int32, (c, LANES), 1)
    head0 = rl < hd
    incl = (rl % hd) <= ri
    strict = (rl % hd) < ri
    pi = lax.broadcasted_iota(jnp.int32, (2 * c, 2 * c), 0)
    pj = lax.broadcasted_iota(jnp.int32, (2 * c, 2 * c), 1)
    level = pi ^ pj
    diag = pi == pj
    eye = jnp.where(diag, 1.0, 0.0)
    n_levels = int(math.log2(c))

    def stack_heads(x):
        return jnp.concatenate([jnp.where(head0, x, 0.0), jnp.where(head0, 0.0, x)], axis=0)

    p1 = {k_: lax.dot_general(buf["lhs1"][k_], buf["rhs1"][k_], NT,
                              preferred_element_type=F32) for k_ in chains}
    a_rk = {k_: jnp.where(incl, p1[k_][0:c, 0:LANES], 0.0).astype(BF16) for k_ in chains}
    a_rb = {k_: jnp.where(incl, p1[k_][0:c, LANES:2 * LANES], 0.0).astype(BF16) for k_ in chains}
    a_kk = {k_: stack_heads(jnp.where(strict, p1[k_][c:2 * c, 0:LANES], 0.0)).astype(BF16)
            for k_ in chains}
    n_bd = {k_: stack_heads(jnp.where(strict, p1[k_][c:2 * c, LANES:2 * LANES], 0.0)) for k_ in chains}

    t_inv = {k_: eye - jnp.where(level == 1, n_bd[k_], 0.0) for k_ in chains}
    yield
    for lg in range(1, n_levels):
        m = 1 << lg
        starts = list(range(m, 2 * c, 2 * m))
        take = (lambda x: x) if m < SUBLANES else (
            lambda x: jnp.concatenate([x[s:s + m] for s in starts], axis=0))
        t_bf = {k_: t_inv[k_].astype(BF16) for k_ in chains}
        nt = {k_: _dot(take(jnp.where((level >> lg) == 1, n_bd[k_], 0.0)), t_bf[k_]) for k_ in chains}
        yield
        if m < SUBLANES:
            t_inv = {k_: t_inv[k_] - _dot(t_bf[k_], nt[k_]) for k_ in chains}
        else:
            zero = jnp.zeros((m, LANES), F32)
            for k_ in chains:
                nt_full = jnp.concatenate(
                    [piece for i in range(len(starts)) for piece in (zero, nt[k_][i * m:(i + 1) * m])], axis=0)
                t_r = take(t_inv[k_])
                t_r = t_r - _dot(t_r, nt_full)
                t_inv[k_] = jnp.concatenate(
                    [piece for i, s in enumerate(starts)
                     for piece in (t_inv[k_][s - m:s], t_r[i * m:(i + 1) * m])], axis=0)
        yield
    t_bf = {k_: t_inv[k_].astype(BF16) for k_ in chains}

    av = {k_: _dot(a_kk[k_], buf["v_bd"][k_]).astype(BF16) for k_ in chains}
    yield
    w12 = {k_: _dot(t_bf[k_], jnp.concatenate([buf["kk_bd"][k_], av[k_]], axis=1))
           for k_ in chains}
    w1 = {k_: w12[k_][:, 0:LANES].astype(BF16) for k_ in chains}
    vw = {k_: jnp.concatenate([buf["v_bd"][k_], (-w12[k_][:, LANES:2 * LANES]).astype(BF16)], axis=0)
          for k_ in chains}
    yield

    y_intra = {k_: _dot(jnp.concatenate([a_rk[k_], a_rb[k_]], axis=1), vw[k_]) for k_ in chains}
    yield
    q_eff = {k_: (buf["q_t"][k_] - _dot(a_rb[k_], w1[k_])).astype(BF16) for k_ in chains}
    yield
    g_mat = {k_: _dot(buf["kb_end"][k_], vw[k_], TN) for k_ in chains}
    yield
    m_mat = {k_: (jnp.where(diag, jnp.broadcast_to(buf["d_end"][k_], (2 * c, LANES)), 0.0)
                  - _dot(buf["b_end"][k_], w1[k_], TN)).astype(BF16) for k_ in chains}
    yield

    for ic in chunk_ids:
        y_c = [_dot(q_eff[(ic, p)], state[p]) + y_intra[(ic, p)] for p in range(n_pairs)]
        state[:] = [_dot(m_mat[(ic, p)], state[p]) + g_mat[(ic, p)] for p in range(n_pairs)]
        y_rows.append(jnp.concatenate(y_c, axis=1))
        yield


def _rwkv_finish(buf, y_rows, vec_ref, o_ref, rows):
    hd = RWKV_HEAD
    y = jnp.concatenate(y_rows, axis=0)
    ones_bd = _head_ones(hd)
    mean = _seg_sum(y, ones_bd) * (1.0 / hd)
    yc = y - mean
    var = _seg_sum(yc * yc, ones_bd) * (1.0 / hd)
    yn = yc * lax.rsqrt(var + GN_EPS) * vec_ref[_LNW:_LNW + 1, :] + vec_ref[_LNB:_LNB + 1, :]
    o_ref[0, rows, :] = ((yn + buf["bonus"]) * buf["gate"]).astype(o_ref.dtype)


def _run_interleaved(*gens):
    live = list(gens)
    while live:
        for gen in list(live):
            if next(gen, StopIteration) is StopIteration:
                live.remove(gen)


def _rwkv_kernel(z_ref, mu_ref, vec_ref, w2_ref, a2_ref, g2_ref, o_ref, st_ref, carry_ref,
                 *, d_rw, lora_wa, lora_g):
    @pl.when(pl.program_id(1) == 0)
    def _():
        st_ref[...] = jnp.zeros_like(st_ref)
        carry_ref[...] = jnp.zeros_like(carry_ref)

    n_pairs = d_rw // LANES
    sub = RWKV_SUB
    n_sub = z_ref.shape[1] // sub
    chunks = list(range(sub // RWKV_C))
    state = [st_ref[p] for p in range(n_pairs)]

    prev_row = carry_ref[0:1, :]
    bufs, prepares = [], []
    for i in range(n_sub):
        z = z_ref[0, i * sub:(i + 1) * sub, :].astype(F32)
        bufs.append({})
        prepares.append(_rwkv_prepare(z, prev_row, bufs[i], mu_ref, vec_ref, w2_ref, a2_ref, g2_ref,
                                      d_rw=d_rw, lora_wa=lora_wa, lora_g=lora_g))
        prev_row = z[sub - 1:sub, :]
    carry_ref[0:1, :] = prev_row

    _run_interleaved(prepares[0])
    for i in range(n_sub):
        y_rows = []
        delta = _rwkv_delta(bufs[i], chunks, state, y_rows, d_rw=d_rw)
        _run_interleaved(*([delta] + prepares[i + 1:i + 2]))
        _rwkv_finish(bufs[i], y_rows, vec_ref, o_ref, slice(i * sub, (i + 1) * sub))
    for p in range(n_pairs):
        st_ref[p] = state[p]


def _rwkv(z, zw, mu, vecs, w2p, a2p, g2p, d_rw, lora_wa, lora_g):
    bsz, seqlen, _ = z.shape
    lb = RWKV_LB
    const = lambda shape: pl.BlockSpec(shape, lambda b, j: (0,) * len(shape))
    params = (mu, vecs, w2p, a2p, g2p)
    return pl.pallas_call(
        functools.partial(_rwkv_kernel, d_rw=d_rw, lora_wa=lora_wa, lora_g=lora_g),
        grid=(bsz, seqlen // lb),
        in_specs=[pl.BlockSpec((1, lb, zw), lambda b, j: (b, j, 0))]
                 + [const(p.shape) for p in params],
        out_specs=pl.BlockSpec((1, lb, d_rw), lambda b, j: (b, j, 0)),
        out_shape=jax.ShapeDtypeStruct((bsz, seqlen, d_rw), BF16),
        scratch_shapes=[
            pltpu.VMEM((d_rw // LANES, LANES, LANES), F32),
            pltpu.VMEM((SUBLANES, zw), F32),
        ],
        compiler_params=pltpu.CompilerParams(
            dimension_semantics=("arbitrary", "arbitrary"),
            vmem_limit_bytes=VMEM_LIMIT),
        name="rwkv",
    )(z, *params)


def _out_ffn_kernel(x_ref, ys_ref, yr_ref, wo_ref, g_ref, wu_ref, wd_ref, o_ref, *, final_norm):
    x = x_ref[0]
    mix = jnp.concatenate([ys_ref[0], yr_ref[0]], axis=1)
    h = x + jnp.dot(mix, wo_ref[...], preferred_element_type=F32)
    ms = jnp.mean(h * h, axis=-1, keepdims=True)
    hn = (h * lax.rsqrt(ms + NORM_EPS) * g_ref[0:1, :]).astype(BF16)
    d_ff = wu_ref.shape[1]
    acc = h
    for j in range(d_ff // FFN_CHUNK):
        sl = slice(j * FFN_CHUNK, (j + 1) * FFN_CHUNK)
        up = jnp.dot(hn, wu_ref[:, sl], preferred_element_type=F32)
        up = jnp.square(jnp.maximum(up, 0.0)).astype(BF16)
        acc = acc + jnp.dot(up, wd_ref[sl, :], preferred_element_type=F32)
    if final_norm:
        ms2 = jnp.mean(acc * acc, axis=-1, keepdims=True)
        acc = acc * lax.rsqrt(ms2 + NORM_EPS) * g_ref[1:2, :]
    o_ref[0] = acc


def _out_ffn(x, y_s5, y_rw, w_out, gains, w_up, w_down, final_norm):
    bsz, seqlen, d_model = x.shape
    tm = FFN_TM
    const = lambda shape: pl.BlockSpec(shape, lambda b, i: (0,) * len(shape),
                                       pipeline_mode=pl.Buffered(1))
    act = lambda width: pl.BlockSpec((1, tm, width), lambda b, i: (b, i, 0))
    return pl.pallas_call(
        functools.partial(_out_ffn_kernel, final_norm=final_norm),
        grid=(bsz, seqlen // tm),
        in_specs=[
            act(d_model), act(y_s5.shape[2]), act(y_rw.shape[2]),
            const(w_out.shape), const(gains.shape), const(w_up.shape), const(w_down.shape),
        ],
        out_specs=act(d_model),
        out_shape=jax.ShapeDtypeStruct((bsz, seqlen, d_model), x.dtype),
        compiler_params=pltpu.CompilerParams(
            dimension_semantics=("arbitrary", "arbitrary"),
            vmem_limit_bytes=VMEM_LIMIT),
        name="out_ffn",
    )(x, y_s5, y_rw, w_out, gains, w_up, w_down)


def _round_up(n, m):
    return -(-n // m) * m


def _s5_weights(lam_re, lam_im, log_dt, b_re, b_im, c_re, c_im):
    groups, n_p = lam_re.shape
    n_h = b_re.shape[2]
    dt = jnp.exp(log_dt.astype(F32))[:, None]
    lr, li = lam_re.astype(F32), lam_im.astype(F32)
    mag = jnp.exp(lr * dt)
    a_re = mag * jnp.cos(li * dt)
    a_im = mag * jnp.sin(li * dt)
    den = lr * lr + li * li
    q_re = ((a_re - 1.0) * lr + a_im * li) / den
    q_im = (a_im * lr - (a_re - 1.0) * li) / den
    bb_re = q_re[:, :, None] * b_re - q_im[:, :, None] * b_im
    bb_im = q_re[:, :, None] * b_im + q_im[:, :, None] * b_re

    tile_groups = 2 * LANES // n_p
    k_groups = LANES // n_h
    n_tiles = groups // tile_groups
    place = np.zeros((n_tiles, k_groups, tile_groups), np.float32)
    for j in range(n_tiles):
        for g in range(tile_groups):
            place[j, (j * tile_groups + g) % k_groups, g] = 1.0

    def b_tiles(bb):
        t = jnp.einsum('jgph,jlg->jlhgp', bb.reshape(n_tiles, tile_groups, n_p, n_h), place)
        return t.reshape(n_tiles, LANES, 2 * LANES)

    wb = jnp.concatenate([b_tiles(bb_re), b_tiles(bb_im)], axis=0).astype(BF16)

    n_out = groups // k_groups
    eye = np.eye(k_groups, dtype=np.float32)

    def c_tiles(cc):
        t = jnp.einsum('mghp,gk->mgpkh', cc.astype(F32).reshape(n_out, k_groups, n_h, n_p), eye)
        return t.reshape(n_out, k_groups * n_p, LANES).astype(BF16)

    lam_bar = jnp.stack([a_re.reshape(-1), a_im.reshape(-1)])
    return wb, lam_bar, c_tiles(c_re), c_tiles(-c_im)


def kernel(x, norm_mix_g, w_in, lam_re, lam_im, log_dt, b_re, b_im, c_re, c_im, d_skip, w_glu, b_glu, s5_out_g, mu_shift, w0, w2, a0, a2, g2, k_k, k_a, r_k, ln_x_w, ln_x_b, w_out, norm_ffn_g, w_up, w_down, norm_final_g):
    depth, d_model, _ = w_in.shape
    d_s5 = d_skip.shape[1]
    d_rw = w0.shape[1]
    n_w, n_a, n_g = w2.shape[1], a2.shape[1], g2.shape[1]
    lora_wa = _round_up(n_w + n_a, LANES)
    lora_g = _round_up(n_g, LANES)
    d_rw_in = 3 * d_rw + n_w + n_a + n_g
    zw = _round_up(3 * d_rw + lora_wa + lora_g, d_s5)
    assert n_w + n_a == lora_wa and zw >= d_rw_in

    h = x
    for i in range(depth):
        main = d_rw_in // LANES * LANES
        w_pieces = (w_in[i][:, d_s5:d_s5 + main].astype(BF16),
                    jnp.pad(w_in[i][:, d_s5 + main:], ((0, 0), (0, zw - d_rw_in))).astype(BF16),
                    w_in[i][:, :d_s5].astype(BF16))
        mu = jnp.pad(mu_shift[i].astype(F32), (0, zw - d_rw_in)).reshape(1, zw)
        w2p = jnp.pad(w2[i], ((0, lora_wa - n_w), (0, 0))).astype(BF16)
        a2p = jnp.pad(a2[i], ((n_w, lora_wa - n_w - n_a), (0, 0))).astype(BF16)
        g2p = jnp.pad(g2[i], ((0, lora_g - n_g), (0, 0))).astype(BF16)
        rw_vecs = jnp.stack([w0[i], a0[i], k_k[i], k_a[i], r_k[i].reshape(-1), ln_x_w[i], ln_x_b[i],
                             jnp.zeros_like(w0[i])]).astype(F32)
        s5_vecs = jnp.stack([d_skip[i], b_glu[i], s5_out_g[i]]).astype(F32)
        gains = jnp.stack([norm_ffn_g[i], norm_final_g]).astype(F32)

        z = _in_proj(h, norm_mix_g[i].reshape(1, -1).astype(F32), w_pieces)

        wb, lam_bar, wc_re, wc_im = _s5_weights(
            lam_re[i], lam_im[i], log_dt[i], b_re[i], b_im[i], c_re[i], c_im[i])
        y_s5 = _s5(z, zw // d_s5, wb, lam_bar, wc_re, wc_im, s5_vecs, w_glu[i].astype(BF16))
        y_rw = _rwkv(z, zw, mu, rw_vecs, w2p, a2p, g2p, d_rw, lora_wa, lora_g)

        h = _out_ffn(h, y_s5, y_rw, w_out[i].astype(BF16), gains,
                     w_up[i].astype(BF16), w_down[i].astype(BF16), final_norm=(i == depth - 1))
    return h
```

```python
import functools
import math

import numpy as np
import jax
import jax.numpy as jnp
from jax import lax
from jax.experimental import pallas as pl
from jax.experimental.pallas import tpu as pltpu

F32 = jnp.float32
BF16 = jnp.bfloat16

NORM_EPS = 1e-6
GN_EPS = 64e-5

S5_GROUP = 16
S5_STATE = 64
RWKV_HEAD = 64
LANES = 128
SUBLANES = 8
VMEM_LIMIT = 56 * 1024 * 1024

IN_TM = 1024
IN_TN = 512
S5_TC = 128
S5_STRIP = 512
RWKV_LB = 512
RWKV_SUB = 256
RWKV_C = 64
FFN_TM = 1024
FFN_CHUNK = 1024

MM = (((1,), (0,)), ((), ()))
NT = (((1,), (1,)), ((), ()))
TN = (((0,), (0,)), ((), ()))


def _dot(a, b, dims=MM):
    return lax.dot_general(a.astype(BF16), b.astype(BF16), dims,
                           preferred_element_type=F32)


def _dot_split_rhs(a_bf16, b, passes):
    acc = None
    rem = b
    for _ in range(passes):
        part = rem.astype(BF16)
        term = jnp.dot(a_bf16, part, preferred_element_type=F32)
        acc = term if acc is None else acc + term
        rem = rem - part.astype(F32)
    return acc


def _in_proj_kernel(x_ref, g_ref, *refs):
    w_refs, z_ref = refs[:-1], refs[-1]
    x = x_ref[0]
    ms = jnp.mean(x * x, axis=-1, keepdims=True)
    xn = (x * lax.rsqrt(ms + NORM_EPS) * g_ref[...]).astype(BF16)
    col = 0
    for w_ref in w_refs:
        n = w_ref.shape[1]
        for j0 in range(0, n, IN_TN):
            j1 = min(j0 + IN_TN, n)
            z_ref[0, :, col + j0:col + j1] = jnp.dot(
                xn, w_ref[:, j0:j1], preferred_element_type=F32).astype(BF16)
        col += n


def _in_proj(x, g, weights):
    bsz, seqlen, d_model = x.shape
    tm = IN_TM
    width = sum(w.shape[1] for w in weights)
    return pl.pallas_call(
        _in_proj_kernel,
        grid=(bsz, seqlen // tm),
        in_specs=[
            pl.BlockSpec((1, tm, d_model), lambda b, i: (b, i, 0)),
            pl.BlockSpec((1, d_model), lambda b, i: (0, 0)),
        ] + [pl.BlockSpec(w.shape, lambda b, i: (0, 0)) for w in weights],
        out_specs=pl.BlockSpec((1, tm, width), lambda b, i: (b, i, 0)),
        out_shape=jax.ShapeDtypeStruct((bsz, seqlen, width), BF16),
        compiler_params=pltpu.CompilerParams(
            dimension_semantics=("arbitrary", "arbitrary"),
            vmem_limit_bytes=VMEM_LIMIT),
        name="in_proj",
    )(x, g, *weights)


def _s5_kernel(u_ref, wb_ref, lam_ref, wcre_ref, wcim_ref, vec_ref, wglu_ref, o_ref,
               st_ref, carry_ref, io_ref):
    bsz, tc, d_s5 = u_ref.shape
    n_state = lam_ref.shape[1]
    n_slabs_half = n_state // LANES
    n_io = d_s5 // LANES
    d_skip, b_glu, g_out = vec_ref[0:1, :], vec_ref[1:2, :], vec_ref[2:3, :]

    @pl.when(pl.program_id(0) == 0)
    def _():
        carry_ref[...] = jnp.zeros_like(carry_ref)

    for b in range(bsz):
        u_b = u_ref[b].astype(F32)
        for q in range(n_io):
            io_ref[q, pl.ds(b, tc, stride=bsz), :] = u_b[:, q * LANES:(q + 1) * LANES]
    u_f32 = jnp.concatenate([io_ref[q] for q in range(n_io)], axis=1)
    u_bf = u_f32.astype(BF16)

    n_tiles = wb_ref.shape[0]
    tile_w = wb_ref.shape[2]
    tiles_per_half = n_tiles // 2
    slabs_per_tile = tile_w // LANES
    slabs_per_strip = S5_STRIP // LANES
    tiles_per_strip = slabs_per_strip // slabs_per_tile
    n_strips = n_slabs_half // slabs_per_strip
    assert wcre_ref.shape[0] == n_strips and wcre_ref.shape[1] == S5_STRIP

    def b_tile(j):
        kb = (j % tiles_per_half) * tile_w // (S5_STATE // S5_GROUP * LANES)
        res = jnp.dot(u_bf[:, kb * LANES:(kb + 1) * LANES], wb_ref[j],
                      preferred_element_type=F32)
        for h in range(slabs_per_tile):
            st_ref[j * slabs_per_tile + h] = res[:, h * LANES:(h + 1) * LANES]

    def strip_tiles(s):
        re_tiles = [s * tiles_per_strip + q for q in range(tiles_per_strip)]
        return re_tiles + [tiles_per_half + j for j in re_tiles]

    ys = [None] * n_strips

    def c_part(m, part):
        base = m * slabs_per_strip + (n_slabs_half if part else 0)
        xs = jnp.concatenate([st_ref[base + q] for q in range(slabs_per_strip)], axis=1).astype(BF16)
        res = jnp.dot(xs, (wcim_ref if part else wcre_ref)[m], preferred_element_type=F32)
        ys[m] = res if ys[m] is None else ys[m] + res

    def scan(s, n_yields):
        re_ids = [s * slabs_per_strip + q for q in range(slabs_per_strip)]
        im_ids = [n_slabs_half + i for i in re_ids]
        a_re = [jnp.broadcast_to(lam_ref[0:1, i * LANES:(i + 1) * LANES], (bsz, LANES)) for i in re_ids]
        a_im = [jnp.broadcast_to(lam_ref[1:2, i * LANES:(i + 1) * LANES], (bsz, LANES)) for i in re_ids]
        s_re = [carry_ref[i] for i in re_ids]
        s_im = [carry_ref[i] for i in im_ids]
        every = max(tc // max(n_yields, 1), 1)
        for t in range(tc):
            rows = slice(t * bsz, (t + 1) * bsz)
            n_re = [a_re[q] * s_re[q] - a_im[q] * s_im[q] + st_ref[re_ids[q], rows, :]
                    for q in range(slabs_per_strip)]
            n_im = [a_re[q] * s_im[q] + a_im[q] * s_re[q] + st_ref[im_ids[q], rows, :]
                    for q in range(slabs_per_strip)]
            for q in range(slabs_per_strip):
                st_ref[re_ids[q], rows, :] = n_re[q]
                st_ref[im_ids[q], rows, :] = n_im[q]
            s_re, s_im = n_re, n_im
            if t % every == every - 1:
                yield
        for q in range(slabs_per_strip):
            carry_ref[re_ids[q]] = s_re[q]
            carry_ref[im_ids[q]] = s_im[q]

    def mxu_pieces(fns):
        for fn in fns:
            fn()
            yield

    for j in strip_tiles(0):
        b_tile(j)
    for s in range(n_strips):
        fns = []
        if s + 1 < n_strips:
            fns += [functools.partial(b_tile, j) for j in strip_tiles(s + 1)]
        if s >= 1:
            fns += [functools.partial(c_part, s - 1, part) for part in (0, 1)]
        _run_interleaved(scan(s, len(fns)), mxu_pieces(fns))
    for part in (0, 1):
        c_part(n_strips - 1, part)

    y = jnp.concatenate(ys, axis=-1) + d_skip * u_f32
    g = 0.5 * y * (1.0 + lax.erf(y * (1.0 / math.sqrt(2.0))))
    gate = jax.nn.sigmoid(
        jnp.dot(g.astype(BF16), wglu_ref[...], preferred_element_type=F32) + b_glu)
    out = g * gate
    ms = jnp.mean(out * out, axis=-1, keepdims=True)
    out = out * lax.rsqrt(ms + NORM_EPS) * g_out
    for q in range(n_io):
        io_ref[q] = out[:, q * LANES:(q + 1) * LANES]
    for b in range(bsz):
        o_ref[b] = jnp.concatenate(
            [io_ref[q, pl.ds(b, tc, stride=bsz), :] for q in range(n_io)], axis=1).astype(o_ref.dtype)


def _s5(z, col_block, wb, lam_bar, wc_re, wc_im, vecs, w_glu):
    bsz, seqlen, _ = z.shape
    d_s5 = vecs.shape[1]
    tc = S5_TC
    n_state = lam_bar.shape[1]
    const = lambda shape: pl.BlockSpec(shape, lambda i: (0,) * len(shape))
    return pl.pallas_call(
        _s5_kernel,
        grid=(seqlen // tc,),
        in_specs=[
            pl.BlockSpec((bsz, tc, d_s5), lambda i: (0, i, col_block)),
            const(wb.shape), const(lam_bar.shape), const(wc_re.shape), const(wc_im.shape),
            const(vecs.shape), const(w_glu.shape),
        ],
        out_specs=pl.BlockSpec((bsz, tc, d_s5), lambda i: (0, i, 0)),
        out_shape=jax.ShapeDtypeStruct((bsz, seqlen, d_s5), BF16),
        scratch_shapes=[
            pltpu.VMEM((2 * n_state // LANES, bsz * tc, LANES), F32),
            pltpu.VMEM((2 * n_state // LANES, bsz, LANES), F32),
            pltpu.VMEM((d_s5 // LANES, bsz * tc, LANES), F32),
        ],
        compiler_params=pltpu.CompilerParams(
            dimension_semantics=("arbitrary",),
            vmem_limit_bytes=VMEM_LIMIT),
        name="s5",
    )(z, wb, lam_bar, wc_re, wc_im, vecs, w_glu)


_W0, _A0, _KK, _KA, _RK, _LNW, _LNB = range(7)


def _seg_sum(x, ones_bd):
    w = ones_bd.shape[0]
    outs = [_dot(x[:, i * w:(i + 1) * w], ones_bd) for i in range(x.shape[1] // w)]
    return jnp.concatenate(outs, axis=-1)


def _head_ones(hd):
    seg_w = 2 * LANES
    li = lax.broadcasted_iota(jnp.int32, (seg_w, seg_w), 0) // hd
    lj = lax.broadcasted_iota(jnp.int32, (seg_w, seg_w), 1) // hd
    return jnp.where(li == lj, 1.0, 0.0).astype(BF16)


def _rwkv_prepare(z, prev_row, buf, mu_ref, vec_ref, w2_ref, a2_ref, g2_ref, *, d_rw, lora_wa, lora_g):
    lb = z.shape[0]
    c = RWKV_C
    hd = RWKV_HEAD
    n_pairs = d_rw // LANES
    vec = lambda i: vec_ref[i:i + 1, :]

    row = lax.broadcasted_iota(jnp.int32, z.shape, 0)
    z_prev = jnp.where(row == 0, prev_row, pltpu.roll(z, 1, 0))
    zz = z + (z_prev - z) * mu_ref[...]
    yield

    o = 3 * d_rw
    r = zz[:, 0:d_rw]
    k = zz[:, d_rw:2 * d_rw]
    v = zz[:, 2 * d_rw:o]
    zwa = zz[:, o:o + lora_wa]
    zg = zz[:, o + lora_wa:o + lora_wa + lora_g]

    wpre = vec(_W0) + _dot(jnp.tanh(zwa), w2_ref[...])
    logd = -math.exp(-0.5) * jax.nn.sigmoid(wpre)
    a = jax.nn.sigmoid(vec(_A0) + _dot(zwa, a2_ref[...]))
    yield
    for name in ("lhs1", "rhs1", "v_bd", "kk_bd", "kb_end", "b_end", "q_t", "d_end"):
        buf[name] = {}
    buf["gate"] = _dot(jax.nn.sigmoid(zg), g2_ref[...])
    yield

    ones_bd = _head_ones(hd)
    kk = k * vec(_KK)
    kk = kk * lax.rsqrt(jnp.maximum(_seg_sum(kk * kk, ones_bd), 1e-24))
    yield
    k2 = k * (1.0 + (a - 1.0) * vec(_KA))
    buf["bonus"] = _seg_sum(r * k2 * vec(_RK), ones_bd) * v
    bvec = kk * a
    yield

    ci = lax.broadcasted_iota(jnp.int32, (c, c), 0)
    cj = lax.broadcasted_iota(jnp.int32, (c, c), 1)
    ltri = jnp.where(cj <= ci, 1.0, 0.0).astype(BF16)
    head0 = lax.broadcasted_iota(jnp.int32, (c, LANES), 1) < hd

    def stack_heads(x):
        return jnp.concatenate([jnp.where(head0, x, 0.0), jnp.where(head0, 0.0, x)], axis=0)

    for ic in range(lb // c):
        rows = slice(ic * c, (ic + 1) * c)
        logd_c = logd[rows]
        cum = _dot_split_rhs(ltri, logd_c, 3)
        cum_end = cum[c - 1:c, :]
        e_inv = jnp.exp(-cum)
        e_end = jnp.exp(cum_end - cum)
        q_c = r[rows] * jnp.exp(cum)
        kk_c = kk[rows] * jnp.exp(cum - logd_c)
        k_hat = k2[rows] * e_inv
        b_hat = bvec[rows] * e_inv
        k_end = k2[rows] * e_end
        b_end = bvec[rows] * e_end
        d_c = jnp.exp(cum_end)
        for p in range(n_pairs):
            ls = slice(p * LANES, (p + 1) * LANES)
            key = (ic, p)
            b_end_bd = stack_heads(b_end[:, ls])
            buf["q_t"][key] = q_c[:, ls]
            buf["d_end"][key] = d_c[:, ls]
            buf["kk_bd"][key] = stack_heads(kk_c[:, ls]).astype(BF16)
            buf["v_bd"][key] = stack_heads(v[rows, ls]).astype(BF16)
            buf["b_end"][key] = b_end_bd.astype(BF16)
            buf["lhs1"][key] = jnp.concatenate([q_c[:, ls], kk_c[:, ls]], axis=0).astype(BF16)
            buf["rhs1"][key] = jnp.concatenate(
                [stack_heads(k_hat[:, ls]), stack_heads(b_hat[:, ls])], axis=0).astype(BF16)
            buf["kb_end"][key] = jnp.concatenate(
                [stack_heads(k_end[:, ls]), b_end_bd], axis=0).astype(BF16)
            yield


def _rwkv_delta(buf, chunk_ids, state, y_rows, *, d_rw):
    c = RWKV_C
    hd = RWKV_HEAD
    n_pairs = d_rw // LANES
    chains = [(ic, p) for ic in chunk_ids for p in range(n_pairs)]

    ri = lax.broadcasted_iota(jnp.int32, (c, LANES), 0)
    rl = lax.broadcasted_iota(jnp.int32, (c, LANES), 1)
    head0 = rl < hd
    incl = (rl % hd) <= ri
    strict = (rl % hd) < ri
    pi = lax.broadcasted_iota(jnp.int32, (2 * c, 2 * c), 0)
    pj = lax.broadcasted_iota(jnp.int32, (2 * c, 2 * c), 1)
    level = pi ^ pj
    diag = pi == pj
    eye = jnp.where(diag, 1.0, 0.0)
    n_levels = int(math.log2(c))

    def stack_heads(x):
        return jnp.concatenate([jnp.where(head0, x, 0.0), jnp.where(head0, 0.0, x)], axis=0)

    p1 = {k_: lax.dot_general(buf["lhs1"][k_], buf["rhs1"][k_], NT,
                              preferred_element_type=F32) for k_ in chains}
    a_rk = {k_: jnp.where(incl, p1[k_][0:c, 0:LANES], 0.0).astype(BF16) for k_ in chains}
    a_rb = {k_: jnp.where(incl, p1[k_][0:c, LANES:2 * LANES], 0.0).astype(BF16) for k_ in chains}
    a_kk = {k_: stack_heads(jnp.where(strict, p1[k_][c:2 * c, 0:LANES], 0.0)).astype(BF16)
            for k_ in chains}
    n_bd = {k_: stack_heads(jnp.where(strict, p1[k_][c:2 * c, LANES:2 * LANES], 0.0)) for k_ in chains}

    t_inv = {k_: eye - jnp.where(level == 1, n_bd[k_], 0.0) for k_ in chains}
    yield
    for lg in range(1, n_levels):
        m = 1 << lg
        starts = list(range(m, 2 * c, 2 * m))
        take = (lambda x: x) if m < SUBLANES else (
            lambda x: jnp.concatenate([x[s:s + m] for s in starts], axis=0))
        t_bf = {k_: t_inv[k_].astype(BF16) for k_ in chains}
        nt = {k_: _dot(take(jnp.where((level >> lg) == 1, n_bd[k_], 0.0)), t_bf[k_]) for k_ in chains}
        yield
        if m < SUBLANES:
            t_inv = {k_: t_inv[k_] - _dot(t_bf[k_], nt[k_]) for k_ in chains}
        else:
            zero = jnp.zeros((m, LANES), F32)
            for k_ in chains:
                nt_full = jnp.concatenate(
                    [piece for i in range(len(starts)) for piece in (zero, nt[k_][i * m:(i + 1) * m])], axis=0)
                t_r = take(t_inv[k_])
                t_r = t_r - _dot(t_r, nt_full)
                t_inv[k_] = jnp.concatenate(
                    [piece for i, s in enumerate(starts)
                     for piece in (t_inv[k_][s - m:s], t_r[i * m:(i + 1) * m])], axis=0)
        yield
    t_bf = {k_: t_inv[k_].astype(BF16) for k_ in chains}

    av = {k_: _dot(a_kk[k_], buf["v_bd"][k_]).astype(BF16) for k_ in chains}
    yield
    w12 = {k_: _dot(t_bf[k_], jnp.concatenate([buf["kk_bd"][k_], av[k_]], axis=1))
           for k_ in chains}
    w1 = {k_: w12[k_][:, 0:LANES].astype(BF16) for k_ in chains}
    vw = {k_: jnp.concatenate([buf["v_bd"][k_], (-w12[k_][:, LANES:2 * LANES]).astype(BF16)], axis=0)
          for k_ in chains}
    yield

    y_intra = {k_: _dot(jnp.concatenate([a_rk[k_], a_rb[k_]], axis=1), vw[k_]) for k_ in chains}
    yield
    q_eff = {k_: (buf["q_t"][k_] - _dot(a_rb[k_], w1[k_])).astype(BF16) for k_ in chains}
    yield
    g_mat = {k_: _dot(buf["kb_end"][k_], vw[k_], TN) for k_ in chains}
    yield
    m_mat = {k_: (jnp.where(diag, jnp.broadcast_to(buf["d_end"][k_], (2 * c, LANES)), 0.0)
                  - _dot(buf["b_end"][k_], w1[k_], TN)).astype(BF16) for k_ in chains}
    yield

    for ic in chunk_ids:
        y_c = [_dot(q_eff[(ic, p)], state[p]) + y_intra[(ic, p)] for p in range(n_pairs)]
        state[:] = [_dot(m_mat[(ic, p)], state[p]) + g_mat[(ic, p)] for p in range(n_pairs)]
        y_rows.append(jnp.concatenate(y_c, axis=1))
        yield


def _rwkv_finish(buf, y_rows, vec_ref, o_ref, rows):
    hd = RWKV_HEAD
    y = jnp.concatenate(y_rows, axis=0)
    ones_bd = _head_ones(hd)
    mean = _seg_sum(y, ones_bd) * (1.0 / hd)
    yc = y - mean
    var = _seg_sum(yc * yc, ones_bd) * (1.0 / hd)
    yn = yc * lax.rsqrt(var + GN_EPS) * vec_ref[_LNW:_LNW + 1, :] + vec_ref[_LNB:_LNB + 1, :]
    o_ref[0, rows, :] = ((yn + buf["bonus"]) * buf["gate"]).astype(o_ref.dtype)


def _run_interleaved(*gens):
    live = list(gens)
    while live:
        for gen in list(live):
            if next(gen, StopIteration) is StopIteration:
                live.remove(gen)


def _rwkv_kernel(z_ref, mu_ref, vec_ref, w2_ref, a2_ref, g2_ref, o_ref, st_ref, carry_ref,
                 *, d_rw, lora_wa, lora_g):
    @pl.when(pl.program_id(1) == 0)
    def _():
        st_ref[...] = jnp.zeros_like(st_ref)
        carry_ref[...] = jnp.zeros_like(carry_ref)

    n_pairs = d_rw // LANES
    sub = RWKV_SUB
    n_sub = z_ref.shape[1] // sub
    chunks = list(range(sub // RWKV_C))
    state = [st_ref[p] for p in range(n_pairs)]

    prev_row = carry_ref[0:1, :]
    bufs, prepares = [], []
    for i in range(n_sub):
        z = z_ref[0, i * sub:(i + 1) * sub, :].astype(F32)
        bufs.append({})
        prepares.append(_rwkv_prepare(z, prev_row, bufs[i], mu_ref, vec_ref, w2_ref, a2_ref, g2_ref,
                                      d_rw=d_rw, lora_wa=lora_wa, lora_g=lora_g))
        prev_row = z[sub - 1:sub, :]
    carry_ref[0:1, :] = prev_row

    _run_interleaved(prepares[0])
    for i in range(n_sub):
        y_rows = []
        delta = _rwkv_delta(bufs[i], chunks, state, y_rows, d_rw=d_rw)
        _run_interleaved(*([delta] + prepares[i + 1:i + 2]))
        _rwkv_finish(bufs[i], y_rows, vec_ref, o_ref, slice(i * sub, (i + 1) * sub))
    for p in range(n_pairs):
        st_ref[p] = state[p]


def _rwkv(z, zw, mu, vecs, w2p, a2p, g2p, d_rw, lora_wa, lora_g):
    bsz, seqlen, _ = z.shape
    lb = RWKV_LB
    const = lambda shape: pl.BlockSpec(shape, lambda b, j: (0,) * len(shape))
    params = (mu, vecs, w2p, a2p, g2p)
    return pl.pallas_call(
        functools.partial(_rwkv_kernel, d_rw=d_rw, lora_wa=lora_wa, lora_g=lora_g),
        grid=(bsz, seqlen // lb),
        in_specs=[pl.BlockSpec((1, lb, zw), lambda b, j: (b, j, 0))]
                 + [const(p.shape) for p in params],
        out_specs=pl.BlockSpec((1, lb, d_rw), lambda b, j: (b, j, 0)),
        out_shape=jax.ShapeDtypeStruct((bsz, seqlen, d_rw), BF16),
        scratch_shapes=[
            pltpu.VMEM((d_rw // LANES, LANES, LANES), F32),
            pltpu.VMEM((SUBLANES, zw), F32),
        ],
        compiler_params=pltpu.CompilerParams(
            dimension_semantics=("arbitrary", "arbitrary"),
            vmem_limit_bytes=VMEM_LIMIT),
        name="rwkv",
    )(z, *params)


def _out_ffn_kernel(x_ref, ys_ref, yr_ref, wo_ref, g_ref, wu_ref, wd_ref, o_ref, *, final_norm):
    x = x_ref[0]
    mix = jnp.concatenate([ys_ref[0], yr_ref[0]], axis=1)
    h = x + jnp.dot(mix, wo_ref[...], preferred_element_type=F32)
    ms = jnp.mean(h * h, axis=-1, keepdims=True)
    hn = (h * lax.rsqrt(ms + NORM_EPS) * g_ref[0:1, :]).astype(BF16)
    d_ff = wu_ref.shape[1]
    acc = h
    for j in range(d_ff // FFN_CHUNK):
        sl = slice(j * FFN_CHUNK, (j + 1) * FFN_CHUNK)
        up = jnp.dot(hn, wu_ref[:, sl], preferred_element_type=F32)
        up = jnp.square(jnp.maximum(up, 0.0)).astype(BF16)
        acc = acc + jnp.dot(up, wd_ref[sl, :], preferred_element_type=F32)
    if final_norm:
        ms2 = jnp.mean(acc * acc, axis=-1, keepdims=True)
        acc = acc * lax.rsqrt(ms2 + NORM_EPS) * g_ref[1:2, :]
    o_ref[0] = acc


def _out_ffn(x, y_s5, y_rw, w_out, gains, w_up, w_down, final_norm):
    bsz, seqlen, d_model = x.shape
    tm = FFN_TM
    const = lambda shape: pl.BlockSpec(shape, lambda b, i: (0,) * len(shape),
                                       pipeline_mode=pl.Buffered(1))
    act = lambda width: pl.BlockSpec((1, tm, width), lambda b, i: (b, i, 0))
    return pl.pallas_call(
        functools.partial(_out_ffn_kernel, final_norm=final_norm),
        grid=(bsz, seqlen // tm),
        in_specs=[
            act(d_model), act(y_s5.shape[2]), act(y_rw.shape[2]),
            const(w_out.shape), const(gains.shape), const(w_up.shape), const(w_down.shape),
        ],
        out_specs=act(d_model),
        out_shape=jax.ShapeDtypeStruct((bsz, seqlen, d_model), x.dtype),
        compiler_params=pltpu.CompilerParams(
            dimension_semantics=("arbitrary", "arbitrary"),
            vmem_limit_bytes=VMEM_LIMIT),
        name="out_ffn",
    )(x, y_s5, y_rw, w_out, gains, w_up, w_down)


def _round_up(n, m):
    return -(-n // m) * m


def _s5_weights(lam_re, lam_im, log_dt, b_re, b_im, c_re, c_im):
    groups, n_p = lam_re.shape
    n_h = b_re.shape[2]
    dt = jnp.exp(log_dt.astype(F32))[:, None]
    lr, li = lam_re.astype(F32), lam_im.astype(F32)
    mag = jnp.exp(lr * dt)
    a_re = mag * jnp.cos(li * dt)
    a_im = mag * jnp.sin(li * dt)
    den = lr * lr + li * li
    q_re = ((a_re - 1.0) * lr + a_im * li) / den
    q_im = (a_im * lr - (a_re - 1.0) * li) / den
    bb_re = q_re[:, :, None] * b_re - q_im[:, :, None] * b_im
    bb_im = q_re[:, :, None] * b_im + q_im[:, :, None] * b_re

    tile_groups = 2 * LANES // n_p
    k_groups = LANES // n_h
    n_tiles = groups // tile_groups
    place = np.zeros((n_tiles, k_groups, tile_groups), np.float32)
    for j in range(n_tiles):
        for g in range(tile_groups):
            place[j, (j * tile_groups + g) % k_groups, g] = 1.0

    def b_tiles(bb):
        t = jnp.einsum('jgph,jlg->jlhgp', bb.reshape(n_tiles, tile_groups, n_p, n_h), place)
        return t.reshape(n_tiles, LANES, 2 * LANES)

    wb = jnp.concatenate([b_tiles(bb_re), b_tiles(bb_im)], axis=0).astype(BF16)

    n_out = groups // k_groups
    eye = np.eye(k_groups, dtype=np.float32)

    def c_tiles(cc):
        t = jnp.einsum('mghp,gk->mgpkh', cc.astype(F32).reshape(n_out, k_groups, n_h, n_p), eye)
        return t.reshape(n_out, k_groups * n_p, LANES).astype(BF16)

    lam_bar = jnp.stack([a_re.reshape(-1), a_im.reshape(-1)])
    return wb, lam_bar, c_tiles(c_re), c_tiles(-c_im)


def kernel(x, norm_mix_g, w_in, lam_re, lam_im, log_dt, b_re, b_im, c_re, c_im, d_skip, w_glu, b_glu, s5_out_g, mu_shift, w0, w2, a0, a2, g2, k_k, k_a, r_k, ln_x_w, ln_x_b, w_out, norm_ffn_g, w_up, w_down, norm_final_g):
    depth, d_model, _ = w_in.shape
    d_s5 = d_skip.shape[1]
    d_rw = w0.shape[1]
    n_w, n_a, n_g = w2.shape[1], a2.shape[1], g2.shape[1]
    lora_wa = _round_up(n_w + n_a, LANES)
    lora_g = _round_up(n_g, LANES)
    d_rw_in = 3 * d_rw + n_w + n_a + n_g
    zw = _round_up(3 * d_rw + lora_wa + lora_g, d_s5)
    assert n_w + n_a == lora_wa and zw >= d_rw_in

    h = x
    for i in range(depth):
        main = d_rw_in // LANES * LANES
        w_pieces = (w_in[i][:, d_s5:d_s5 + main].astype(BF16),
                    jnp.pad(w_in[i][:, d_s5 + main:], ((0, 0), (0, zw - d_rw_in))).astype(BF16),
                    w_in[i][:, :d_s5].astype(BF16))
        mu = jnp.pad(mu_shift[i].astype(F32), (0, zw - d_rw_in)).reshape(1, zw)
        w2p = jnp.pad(w2[i], ((0, lora_wa - n_w), (0, 0))).astype(BF16)
        a2p = jnp.pad(a2[i], ((n_w, lora_wa - n_w - n_a), (0, 0))).astype(BF16)
        g2p = jnp.pad(g2[i], ((0, lora_g - n_g), (0, 0))).astype(BF16)
        rw_vecs = jnp.stack([w0[i], a0[i], k_k[i], k_a[i], r_k[i].reshape(-1), ln_x_w[i], ln_x_b[i],
                             jnp.zeros_like(w0[i])]).astype(F32)
        s5_vecs = jnp.stack([d_skip[i], b_glu[i], s5_out_g[i]]).astype(F32)
        gains = jnp.stack([norm_ffn_g[i], norm_final_g]).astype(F32)

        z = _in_proj(h, norm_mix_g[i].reshape(1, -1).astype(F32), w_pieces)

        wb, lam_bar, wc_re, wc_im = _s5_weights(
            lam_re[i], lam_im[i], log_dt[i], b_re[i], b_im[i], c_re[i], c_im[i])
        y_s5 = _s5(z, zw // d_s5, wb, lam_bar, wc_re, wc_im, s5_vecs, w_glu[i].astype(BF16))
        y_rw = _rwkv(z, zw, mu, rw_vecs, w2p, a2p, g2p, d_rw, lora_wa, lora_g)

        h = _out_ffn(h, y_s5, y_rw, w_out[i].astype(BF16), gains,
                     w_up[i].astype(BF16), w_down[i].astype(BF16), final_norm=(i == depth - 1))
    return h
```

```python
import functools
import math

import numpy as np
import jax
import jax.numpy as jnp
from jax import lax
from jax.experimental import pallas as pl
from jax.experimental.pallas import tpu as pltpu

F32 = jnp.float32
BF16 = jnp.bfloat16

NORM_EPS = 1e-6
GN_EPS = 64e-5

S5_GROUP = 16
S5_STATE = 64
RWKV_HEAD = 64
LANES = 128
SUBLANES = 8
VMEM_LIMIT = 56 * 1024 * 1024

IN_TM = 1024
IN_TN = 512
S5_TC = 128
S5_STRIP = 512
RWKV_LB = 512
RWKV_SUB = 256
RWKV_C = 64
FFN_TM = 1024
FFN_CHUNK = 1024

MM = (((1,), (0,)), ((), ()))
NT = (((1,), (1,)), ((), ()))
TN = (((0,), (0,)), ((), ()))


def _dot(a, b, dims=MM):
    return lax.dot_general(a.astype(BF16), b.astype(BF16), dims,
                           preferred_element_type=F32)


def _dot_split_rhs(a_bf16, b, passes):
    acc = None
    rem = b
    for _ in range(passes):
        part = rem.astype(BF16)
        term = jnp.dot(a_bf16, part, preferred_element_type=F32)
        acc = term if acc is None else acc + term
        rem = rem - part.astype(F32)
    return acc


def _in_proj_kernel(x_ref, g_ref, mu_ref, *refs):
    w_refs, z_ref, carry_ref = refs[:-2], refs[-2], refs[-1]
    n_shift = mu_ref.shape[1]
    x = x_ref[0]
    tm = x.shape[0]

    @pl.when(pl.program_id(1) == 0)
    def _():
        carry_ref[...] = jnp.zeros_like(carry_ref)

    ms = jnp.mean(x * x, axis=-1, keepdims=True)
    xn = (x * lax.rsqrt(ms + NORM_EPS) * g_ref[...]).astype(BF16)
    col = 0
    for w_ref in w_refs:
        n = w_ref.shape[1]
        for j0 in range(0, n, IN_TN):
            j1 = min(j0 + IN_TN, n)
            res = jnp.dot(xn, w_ref[:, j0:j1], preferred_element_type=F32)
            cols = slice(col + j0, col + j1)
            if col + j1 <= n_shift:
                first = lax.broadcasted_iota(jnp.int32, res.shape, 0) == 0
                prev = jnp.where(first, carry_ref[0:1, cols], pltpu.roll(res, 1, 0))
                carry_ref[0:1, cols] = res[tm - 1:tm, :]
                res = res + (prev - res) * mu_ref[:, cols]
            else:
                assert col + j0 >= n_shift
            z_ref[0, :, cols] = res.astype(BF16)
        col += n


def _in_proj(x, g, mu, weights):
    bsz, seqlen, d_model = x.shape
    tm = IN_TM
    width = sum(w.shape[1] for w in weights)
    const = lambda shape: pl.BlockSpec(shape, lambda b, i: (0,) * len(shape))
    return pl.pallas_call(
        _in_proj_kernel,
        grid=(bsz, seqlen // tm),
        in_specs=[pl.BlockSpec((1, tm, d_model), lambda b, i: (b, i, 0)), const(g.shape), const(mu.shape)]
                 + [const(w.shape) for w in weights],
        out_specs=pl.BlockSpec((1, tm, width), lambda b, i: (b, i, 0)),
        out_shape=jax.ShapeDtypeStruct((bsz, seqlen, width), BF16),
        scratch_shapes=[pltpu.VMEM((SUBLANES, mu.shape[1]), F32)],
        compiler_params=pltpu.CompilerParams(
            dimension_semantics=("arbitrary", "arbitrary"),
            vmem_limit_bytes=VMEM_LIMIT),
        name="in_proj",
    )(x, g, mu, *weights)


def _s5_kernel(u_ref, wb_ref, lam_ref, wcre_ref, wcim_ref, vec_ref, wglu_ref, o_ref,
               st_ref, carry_ref, io_ref):
    bsz, tc, d_s5 = u_ref.shape
    n_state = lam_ref.shape[1]
    n_slabs_half = n_state // LANES
    n_io = d_s5 // LANES
    d_skip, b_glu, g_out = vec_ref[0:1, :], vec_ref[1:2, :], vec_ref[2:3, :]

    @pl.when(pl.program_id(0) == 0)
    def _():
        carry_ref[...] = jnp.zeros_like(carry_ref)

    for b in range(bsz):
        u_b = u_ref[b].astype(F32)
        for q in range(n_io):
            io_ref[q, pl.ds(b, tc, stride=bsz), :] = u_b[:, q * LANES:(q + 1) * LANES]
    u_f32 = jnp.concatenate([io_ref[q] for q in range(n_io)], axis=1)
    u_bf = u_f32.astype(BF16)

    n_tiles = wb_ref.shape[0]
    tile_w = wb_ref.shape[2]
    tiles_per_half = n_tiles // 2
    slabs_per_tile = tile_w // LANES
    slabs_per_strip = S5_STRIP // LANES
    tiles_per_strip = slabs_per_strip // slabs_per_tile
    n_strips = n_slabs_half // slabs_per_strip
    assert wcre_ref.shape[0] == n_strips and wcre_ref.shape[1] == S5_STRIP

    def b_tile(j):
        kb = (j % tiles_per_half) * tile_w // (S5_STATE // S5_GROUP * LANES)
        res = jnp.dot(u_bf[:, kb * LANES:(kb + 1) * LANES], wb_ref[j],
                      preferred_element_type=F32)
        for h in range(slabs_per_tile):
            st_ref[j * slabs_per_tile + h] = res[:, h * LANES:(h + 1) * LANES]

    def strip_tiles(s):
        re_tiles = [s * tiles_per_strip + q for q in range(tiles_per_strip)]
        return re_tiles + [tiles_per_half + j for j in re_tiles]

    ys = [None] * n_strips

    def c_part(m, part):
        base = m * slabs_per_strip + (n_slabs_half if part else 0)
        xs = jnp.concatenate([st_ref[base + q] for q in range(slabs_per_strip)], axis=1).astype(BF16)
        res = jnp.dot(xs, (wcim_ref if part else wcre_ref)[m], preferred_element_type=F32)
        ys[m] = res if ys[m] is None else ys[m] + res

    def scan(s, n_yields):
        re_ids = [s * slabs_per_strip + q for q in range(slabs_per_strip)]
        im_ids = [n_slabs_half + i for i in re_ids]
        a_re = [jnp.broadcast_to(lam_ref[0:1, i * LANES:(i + 1) * LANES], (bsz, LANES)) for i in re_ids]
        a_im = [jnp.broadcast_to(lam_ref[1:2, i * LANES:(i + 1) * LANES], (bsz, LANES)) for i in re_ids]
        s_re = [carry_ref[i] for i in re_ids]
        s_im = [carry_ref[i] for i in im_ids]
        every = max(tc // max(n_yields, 1), 1)
        for t in range(tc):
            rows = slice(t * bsz, (t + 1) * bsz)
            n_re = [a_re[q] * s_re[q] - a_im[q] * s_im[q] + st_ref[re_ids[q], rows, :]
                    for q in range(slabs_per_strip)]
            n_im = [a_re[q] * s_im[q] + a_im[q] * s_re[q] + st_ref[im_ids[q], rows, :]
                    for q in range(slabs_per_strip)]
            for q in range(slabs_per_strip):
                st_ref[re_ids[q], rows, :] = n_re[q]
                st_ref[im_ids[q], rows, :] = n_im[q]
            s_re, s_im = n_re, n_im
            if t % every == every - 1:
                yield
        for q in range(slabs_per_strip):
            carry_ref[re_ids[q]] = s_re[q]
            carry_ref[im_ids[q]] = s_im[q]

    def mxu_pieces(fns):
        for fn in fns:
            fn()
            yield

    for j in strip_tiles(0):
        b_tile(j)
    for s in range(n_strips):
        fns = []
        if s + 1 < n_strips:
            fns += [functools.partial(b_tile, j) for j in strip_tiles(s + 1)]
        if s >= 1:
            fns += [functools.partial(c_part, s - 1, part) for part in (0, 1)]
        _run_interleaved(scan(s, len(fns)), mxu_pieces(fns))
    for part in (0, 1):
        c_part(n_strips - 1, part)

    y = jnp.concatenate(ys, axis=-1) + d_skip * u_f32
    g = 0.5 * y * (1.0 + lax.erf(y * (1.0 / math.sqrt(2.0))))
    gate = jax.nn.sigmoid(
        jnp.dot(g.astype(BF16), wglu_ref[...], preferred_element_type=F32) + b_glu)
    out = g * gate
    ms = jnp.mean(out * out, axis=-1, keepdims=True)
    out = out * lax.rsqrt(ms + NORM_EPS) * g_out
    for q in range(n_io):
        io_ref[q] = out[:, q * LANES:(q + 1) * LANES]
    for b in range(bsz):
        o_ref[b] = jnp.concatenate(
            [io_ref[q, pl.ds(b, tc, stride=bsz), :] for q in range(n_io)], axis=1).astype(o_ref.dtype)


def _s5(z, col_block, wb, lam_bar, wc_re, wc_im, vecs, w_glu):
    bsz, seqlen, _ = z.shape
    d_s5 = vecs.shape[1]
    tc = S5_TC
    n_state = lam_bar.shape[1]
    const = lambda shape: pl.BlockSpec(shape, lambda i: (0,) * len(shape))
    return pl.pallas_call(
        _s5_kernel,
        grid=(seqlen // tc,),
        in_specs=[
            pl.BlockSpec((bsz, tc, d_s5), lambda i: (0, i, col_block)),
            const(wb.shape), const(lam_bar.shape), const(wc_re.shape), const(wc_im.shape),
            const(vecs.shape), const(w_glu.shape),
        ],
        out_specs=pl.BlockSpec((bsz, tc, d_s5), lambda i: (0, i, 0)),
        out_shape=jax.ShapeDtypeStruct((bsz, seqlen, d_s5), BF16),
        scratch_shapes=[
            pltpu.VMEM((2 * n_state // LANES, bsz * tc, LANES), F32),
            pltpu.VMEM((2 * n_state // LANES, bsz, LANES), F32),
            pltpu.VMEM((d_s5 // LANES, bsz * tc, LANES), F32),
        ],
        compiler_params=pltpu.CompilerParams(
            dimension_semantics=("arbitrary",),
            vmem_limit_bytes=VMEM_LIMIT),
        name="s5",
    )(z, wb, lam_bar, wc_re, wc_im, vecs, w_glu)


_W0, _A0, _KK, _KA, _RK, _LNW, _LNB = range(7)


def _seg_sum(x, ones_bd):
    w = ones_bd.shape[0]
    outs = [_dot(x[:, i * w:(i + 1) * w], ones_bd) for i in range(x.shape[1] // w)]
    return jnp.concatenate(outs, axis=-1)


def _head_ones(hd):
    seg_w = 2 * LANES
    li = lax.broadcasted_iota(jnp.int32, (seg_w, seg_w), 0) // hd
    lj = lax.broadcasted_iota(jnp.int32, (seg_w, seg_w), 1) // hd
    return jnp.where(li == lj, 1.0, 0.0).astype(BF16)


def _rwkv_prepare(zz, buf, vec_ref, w2_ref, a2_ref, g2_ref, *, d_rw, lora_wa, lora_g):
    lb = zz.shape[0]
    c = RWKV_C
    hd = RWKV_HEAD
    n_pairs = d_rw // LANES
    vec = lambda i: vec_ref[i:i + 1, :]

    o = 3 * d_rw
    r = zz[:, 0:d_rw]
    k = zz[:, d_rw:2 * d_rw]
    v = zz[:, 2 * d_rw:o]
    zwa = zz[:, o:o + lora_wa]
    zg = zz[:, o + lora_wa:o + lora_wa + lora_g]

    wpre = vec(_W0) + _dot(jnp.tanh(zwa), w2_ref[...])
    logd = -math.exp(-0.5) * jax.nn.sigmoid(wpre)
    a = jax.nn.sigmoid(vec(_A0) + _dot(zwa, a2_ref[...]))
    yield
    for name in ("lhs1", "rhs1", "v_bd", "kk_bd", "kb_end", "b_end", "q_t", "d_end"):
        buf[name] = {}
    buf["gate"] = _dot(jax.nn.sigmoid(zg), g2_ref[...])
    yield

    ones_bd = _head_ones(hd)
    kk = k * vec(_KK)
    kk = kk * lax.rsqrt(jnp.maximum(_seg_sum(kk * kk, ones_bd), 1e-24))
    yield
    k2 = k * (1.0 + (a - 1.0) * vec(_KA))
    buf["bonus"] = _seg_sum(r * k2 * vec(_RK), ones_bd) * v
    bvec = kk * a
    yield

    ci = lax.broadcasted_iota(jnp.int32, (c, c), 0)
    cj = lax.broadcasted_iota(jnp.int32, (c, c), 1)
    ltri = jnp.where(cj <= ci, 1.0, 0.0).astype(BF16)
    head0 = lax.broadcasted_iota(jnp.int32, (c, LANES), 1) < hd

    def stack_heads(x):
        return jnp.concatenate([jnp.where(head0, x, 0.0), jnp.where(head0, 0.0, x)], axis=0)

    for ic in range(lb // c):
        rows = slice(ic * c, (ic + 1) * c)
        logd_c = logd[rows]
        cum = _dot_split_rhs(ltri, logd_c, 3)
        cum_end = cum[c - 1:c, :]
        e_inv = jnp.exp(-cum)
        e_end = jnp.exp(cum_end - cum)
        q_c = r[rows] * jnp.exp(cum)
        kk_c = kk[rows] * jnp.exp(cum - logd_c)
        k_hat = k2[rows] * e_inv
        b_hat = bvec[rows] * e_inv
        k_end = k2[rows] * e_end
        b_end = bvec[rows] * e_end
        d_c = jnp.exp(cum_end)
        for p in range(n_pairs):
            ls = slice(p * LANES, (p + 1) * LANES)
            key = (ic, p)
            b_end_bd = stack_heads(b_end[:, ls])
            buf["q_t"][key] = q_c[:, ls]
            buf["d_end"][key] = d_c[:, ls]
            buf["kk_bd"][key] = stack_heads(kk_c[:, ls]).astype(BF16)
            buf["v_bd"][key] = stack_heads(v[rows, ls]).astype(BF16)
            buf["b_end"][key] = b_end_bd.astype(BF16)
            buf["lhs1"][key] = jnp.concatenate([q_c[:, ls], kk_c[:, ls]], axis=0).astype(BF16)
            buf["rhs1"][key] = jnp.concatenate(
                [stack_heads(k_hat[:, ls]), stack_heads(b_hat[:, ls])], axis=0).astype(BF16)
            buf["kb_end"][key] = jnp.concatenate(
                [stack_heads(k_end[:, ls]), b_end_bd], axis=0).astype(BF16)
            if p % 2:
                yield


def _rwkv_delta(buf, chunk_ids, state, y_rows, *, d_rw, state_ready):
    c = RWKV_C
    hd = RWKV_HEAD
    n_pairs = d_rw // LANES
    chains = [(ic, p) for ic in chunk_ids for p in range(n_pairs)]

    ri = lax.broadcasted_iota(jnp.int32, (c, LANES), 0)
    rl = lax.broadcasted_iota(jnp.int32, (c, LANES), 1)
    head0 = rl < hd
    incl = (rl % hd) <= ri
    strict = (rl % hd) < ri
    pi = lax.broadcasted_iota(jnp.int32, (2 * c, 2 * c), 0)
    pj = lax.broadcasted_iota(jnp.int32, (2 * c, 2 * c), 1)
    level = pi ^ pj
    diag = pi == pj
    eye = jnp.where(diag, 1.0, 0.0)
    n_levels = int(math.log2(c))

    def stack_heads(x):
        return jnp.concatenate([jnp.where(head0, x, 0.0), jnp.where(head0, 0.0, x)], axis=0)

    p1 = {k_: lax.dot_general(buf["lhs1"][k_], buf["rhs1"][k_], NT,
                              preferred_element_type=F32) for k_ in chains}
    a_rk = {k_: jnp.where(incl, p1[k_][0:c, 0:LANES], 0.0).astype(BF16) for k_ in chains}
    a_rb = {k_: jnp.where(incl, p1[k_][0:c, LANES:2 * LANES], 0.0).astype(BF16) for k_ in chains}
    a_kk = {k_: stack_heads(jnp.where(strict, p1[k_][c:2 * c, 0:LANES], 0.0)).astype(BF16)
            for k_ in chains}
    n_bd = {k_: stack_heads(jnp.where(strict, p1[k_][c:2 * c, LANES:2 * LANES], 0.0)) for k_ in chains}

    t_inv = {k_: eye - jnp.where(level == 1, n_bd[k_], 0.0) for k_ in chains}
    yield
    for lg in range(1, n_levels):
        m = 1 << lg
        starts = list(range(m, 2 * c, 2 * m))
        take = (lambda x: x) if m < SUBLANES else (
            lambda x: jnp.concatenate([x[s:s + m] for s in starts], axis=0))
        t_bf = {k_: t_inv[k_].astype(BF16) for k_ in chains}
        nt = {k_: _dot(take(jnp.where((level >> lg) == 1, n_bd[k_], 0.0)), t_bf[k_]) for k_ in chains}
        yield
        if m < SUBLANES:
            t_inv = {k_: t_inv[k_] - _dot(t_bf[k_], nt[k_]) for k_ in chains}
        else:
            zero = jnp.zeros((m, LANES), F32)
            for k_ in chains:
                nt_full = jnp.concatenate(
                    [piece for i in range(len(starts)) for piece in (zero, nt[k_][i * m:(i + 1) * m])], axis=0)
                t_r = take(t_inv[k_])
                t_r = t_r - _dot(t_r, nt_full)
                t_inv[k_] = jnp.concatenate(
                    [piece for i, s in enumerate(starts)
                     for piece in (t_inv[k_][s - m:s], t_r[i * m:(i + 1) * m])], axis=0)
        yield
    t_bf = {k_: t_inv[k_].astype(BF16) for k_ in chains}

    av = {k_: _dot(a_kk[k_], buf["v_bd"][k_]).astype(BF16) for k_ in chains}
    yield
    w12 = {k_: _dot(t_bf[k_], jnp.concatenate([buf["kk_bd"][k_], av[k_]], axis=1))
           for k_ in chains}
    w1 = {k_: w12[k_][:, 0:LANES].astype(BF16) for k_ in chains}
    vw = {k_: jnp.concatenate([buf["v_bd"][k_], (-w12[k_][:, LANES:2 * LANES]).astype(BF16)], axis=0)
          for k_ in chains}
    yield

    y_intra = {k_: _dot(jnp.concatenate([a_rk[k_], a_rb[k_]], axis=1), vw[k_]) for k_ in chains}
    yield
    q_eff = {k_: (buf["q_t"][k_] - _dot(a_rb[k_], w1[k_])).astype(BF16) for k_ in chains}
    yield
    g_mat = {k_: _dot(buf["kb_end"][k_], vw[k_], TN) for k_ in chains}
    yield
    m_mat = {k_: (jnp.where(diag, jnp.broadcast_to(buf["d_end"][k_], (2 * c, LANES)), 0.0)
                  - _dot(buf["b_end"][k_], w1[k_], TN)).astype(BF16) for k_ in chains}
    yield

    assert state_ready()
    for ic in chunk_ids:
        y_c = [_dot(q_eff[(ic, p)], state[p]) + y_intra[(ic, p)] for p in range(n_pairs)]
        state[:] = [_dot(m_mat[(ic, p)], state[p]) + g_mat[(ic, p)] for p in range(n_pairs)]
        y_rows.append(jnp.concatenate(y_c, axis=1))
        yield


def _rwkv_finish(buf, y_rows, vec_ref, o_ref, rows):
    hd = RWKV_HEAD
    y = jnp.concatenate(y_rows, axis=0)
    ones_bd = _head_ones(hd)
    mean = _seg_sum(y, ones_bd) * (1.0 / hd)
    yc = y - mean
    var = _seg_sum(yc * yc, ones_bd) * (1.0 / hd)
    yn = yc * lax.rsqrt(var + GN_EPS) * vec_ref[_LNW:_LNW + 1, :] + vec_ref[_LNB:_LNB + 1, :]
    o_ref[0, rows, :] = ((yn + buf["bonus"]) * buf["gate"]).astype(o_ref.dtype)


def _run_interleaved(*gens):
    live = list(gens)
    while live:
        for gen in list(live):
            if next(gen, StopIteration) is StopIteration:
                live.remove(gen)


def _rwkv_kernel(z_ref, vec_ref, w2_ref, a2_ref, g2_ref, o_ref, st_ref, *, d_rw, lora_wa, lora_g):
    @pl.when(pl.program_id(1) == 0)
    def _():
        st_ref[...] = jnp.zeros_like(st_ref)

    n_pairs = d_rw // LANES
    sub = RWKV_SUB
    n_sub = z_ref.shape[1] // sub
    chunks = list(range(sub // RWKV_C))
    state = [st_ref[p] for p in range(n_pairs)]

    bufs, prepares = [], []
    for i in range(n_sub):
        zz = z_ref[0, i * sub:(i + 1) * sub, :].astype(F32)
        bufs.append({})
        prepares.append(_rwkv_prepare(zz, bufs[i], vec_ref, w2_ref, a2_ref, g2_ref,
                                      d_rw=d_rw, lora_wa=lora_wa, lora_g=lora_g))

    finished = []

    def delta_task(i):
        y_rows = []
        yield from _rwkv_delta(bufs[i], chunks, state, y_rows, d_rw=d_rw,
                               state_ready=lambda: len(finished) == i)
        _rwkv_finish(bufs[i], y_rows, vec_ref, o_ref, slice(i * sub, (i + 1) * sub))
        finished.append(i)

    _run_interleaved(prepares[0])
    live = [delta_task(0)] + prepares[1:2]
    started = 1
    while live:
        for gen in list(live):
            if next(gen, StopIteration) is StopIteration:
                live.remove(gen)
                if started < n_sub and gen is prepares[started]:
                    live.append(delta_task(started))
                    live.extend(prepares[started + 1:started + 2])
                    started += 1
    for p in range(n_pairs):
        st_ref[p] = state[p]


def _rwkv(z, zw, vecs, w2p, a2p, g2p, d_rw, lora_wa, lora_g):
    bsz, seqlen, _ = z.shape
    lb = RWKV_LB
    const = lambda shape: pl.BlockSpec(shape, lambda b, j: (0,) * len(shape))
    params = (vecs, w2p, a2p, g2p)
    return pl.pallas_call(
        functools.partial(_rwkv_kernel, d_rw=d_rw, lora_wa=lora_wa, lora_g=lora_g),
        grid=(bsz, seqlen // lb),
        in_specs=[pl.BlockSpec((1, lb, zw), lambda b, j: (b, j, 0))]
                 + [const(p.shape) for p in params],
        out_specs=pl.BlockSpec((1, lb, d_rw), lambda b, j: (b, j, 0)),
        out_shape=jax.ShapeDtypeStruct((bsz, seqlen, d_rw), BF16),
        scratch_shapes=[pltpu.VMEM((d_rw // LANES, LANES, LANES), F32)],
        compiler_params=pltpu.CompilerParams(
            dimension_semantics=("arbitrary", "arbitrary"),
            vmem_limit_bytes=VMEM_LIMIT),
        name="rwkv",
    )(z, *params)


def _out_ffn_kernel(x_ref, ys_ref, yr_ref, wo_ref, g_ref, wu_ref, wd_ref, o_ref, *, final_norm):
    x = x_ref[0]
    mix = jnp.concatenate([ys_ref[0], yr_ref[0]], axis=1)
    h = x + jnp.dot(mix, wo_ref[...], preferred_element_type=F32)
    ms = jnp.mean(h * h, axis=-1, keepdims=True)
    hn = (h * lax.rsqrt(ms + NORM_EPS) * g_ref[0:1, :]).astype(BF16)
    d_ff = wu_ref.shape[1]
    acc = h
    for j in range(d_ff // FFN_CHUNK):
        sl = slice(j * FFN_CHUNK, (j + 1) * FFN_CHUNK)
        up = jnp.dot(hn, wu_ref[:, sl], preferred_element_type=F32)
        up = jnp.square(jnp.maximum(up, 0.0)).astype(BF16)
        acc = acc + jnp.dot(up, wd_ref[sl, :], preferred_element_type=F32)
    if final_norm:
        ms2 = jnp.mean(acc * acc, axis=-1, keepdims=True)
        acc = acc * lax.rsqrt(ms2 + NORM_EPS) * g_ref[1:2, :]
    o_ref[0] = acc


def _out_ffn(x, y_s5, y_rw, w_out, gains, w_up, w_down, final_norm):
    bsz, seqlen, d_model = x.shape
    tm = FFN_TM
    const = lambda shape: pl.BlockSpec(shape, lambda b, i: (0,) * len(shape),
                                       pipeline_mode=pl.Buffered(1))
    act = lambda width: pl.BlockSpec((1, tm, width), lambda b, i: (b, i, 0))
    return pl.pallas_call(
        functools.partial(_out_ffn_kernel, final_norm=final_norm),
        grid=(bsz, seqlen // tm),
        in_specs=[
            act(d_model), act(y_s5.shape[2]), act(y_rw.shape[2]),
            const(w_out.shape), const(gains.shape), const(w_up.shape), const(w_down.shape),
        ],
        out_specs=act(d_model),
        out_shape=jax.ShapeDtypeStruct((bsz, seqlen, d_model), x.dtype),
        compiler_params=pltpu.CompilerParams(
            dimension_semantics=("arbitrary", "arbitrary"),
            vmem_limit_bytes=VMEM_LIMIT),
        name="out_ffn",
    )(x, y_s5, y_rw, w_out, gains, w_up, w_down)


def _round_up(n, m):
    return -(-n // m) * m


def _s5_weights(lam_re, lam_im, log_dt, b_re, b_im, c_re, c_im):
    groups, n_p = lam_re.shape
    n_h = b_re.shape[2]
    dt = jnp.exp(log_dt.astype(F32))[:, None]
    lr, li = lam_re.astype(F32), lam_im.astype(F32)
    mag = jnp.exp(lr * dt)
    a_re = mag * jnp.cos(li * dt)
    a_im = mag * jnp.sin(li * dt)
    den = lr * lr + li * li
    q_re = ((a_re - 1.0) * lr + a_im * li) / den
    q_im = (a_im * lr - (a_re - 1.0) * li) / den
    bb_re = q_re[:, :, None] * b_re - q_im[:, :, None] * b_im
    bb_im = q_re[:, :, None] * b_im + q_im[:, :, None] * b_re

    tile_groups = 2 * LANES // n_p
    k_groups = LANES // n_h
    n_tiles = groups // tile_groups
    place = np.zeros((n_tiles, k_groups, tile_groups), np.float32)
    for j in range(n_tiles):
        for g in range(tile_groups):
            place[j, (j * tile_groups + g) % k_groups, g] = 1.0

    def b_tiles(bb):
        t = jnp.einsum('jgph,jlg->jlhgp', bb.reshape(n_tiles, tile_groups, n_p, n_h), place)
        return t.reshape(n_tiles, LANES, 2 * LANES)

    wb = jnp.concatenate([b_tiles(bb_re), b_tiles(bb_im)], axis=0).astype(BF16)

    n_out = groups // k_groups
    eye = np.eye(k_groups, dtype=np.float32)

    def c_tiles(cc):
        t = jnp.einsum('mghp,gk->mgpkh', cc.astype(F32).reshape(n_out, k_groups, n_h, n_p), eye)
        return t.reshape(n_out, k_groups * n_p, LANES).astype(BF16)

    lam_bar = jnp.stack([a_re.reshape(-1), a_im.reshape(-1)])
    return wb, lam_bar, c_tiles(c_re), c_tiles(-c_im)


def kernel(x, norm_mix_g, w_in, lam_re, lam_im, log_dt, b_re, b_im, c_re, c_im, d_skip, w_glu, b_glu, s5_out_g, mu_shift, w0, w2, a0, a2, g2, k_k, k_a, r_k, ln_x_w, ln_x_b, w_out, norm_ffn_g, w_up, w_down, norm_final_g):
    depth, d_model, _ = w_in.shape
    d_s5 = d_skip.shape[1]
    d_rw = w0.shape[1]
    n_w, n_a, n_g = w2.shape[1], a2.shape[1], g2.shape[1]
    lora_wa = _round_up(n_w + n_a, LANES)
    lora_g = _round_up(n_g, LANES)
    d_rw_in = 3 * d_rw + n_w + n_a + n_g
    zw = _round_up(3 * d_rw + lora_wa + lora_g, d_s5)
    assert n_w + n_a == lora_wa and zw >= d_rw_in

    h = x
    for i in range(depth):
        main = d_rw_in // LANES * LANES
        w_pieces = (w_in[i][:, d_s5:d_s5 + main].astype(BF16),
                    jnp.pad(w_in[i][:, d_s5 + main:], ((0, 0), (0, zw - d_rw_in))).astype(BF16),
                    w_in[i][:, :d_s5].astype(BF16))
        mu = jnp.pad(mu_shift[i].astype(F32), (0, zw - d_rw_in)).reshape(1, zw)
        w2p = jnp.pad(w2[i], ((0, lora_wa - n_w), (0, 0))).astype(BF16)
        a2p = jnp.pad(a2[i], ((n_w, lora_wa - n_w - n_a), (0, 0))).astype(BF16)
        g2p = jnp.pad(g2[i], ((0, lora_g - n_g), (0, 0))).astype(BF16)
        rw_vecs = jnp.stack([w0[i], a0[i], k_k[i], k_a[i], r_k[i].reshape(-1), ln_x_w[i], ln_x_b[i],
                             jnp.zeros_like(w0[i])]).astype(F32)
        s5_vecs = jnp.stack([d_skip[i], b_glu[i], s5_out_g[i]]).astype(F32)
        gains = jnp.stack([norm_ffn_g[i], norm_final_g]).astype(F32)

        z = _in_proj(h, norm_mix_g[i].reshape(1, -1).astype(F32), mu, w_pieces)

        wb, lam_bar, wc_re, wc_im = _s5_weights(
            lam_re[i], lam_im[i], log_dt[i], b_re[i], b_im[i], c_re[i], c_im[i])
        y_s5 = _s5(z, zw // d_s5, wb, lam_bar, wc_re, wc_im, s5_vecs, w_glu[i].astype(BF16))
        y_rw = _rwkv(z, zw, rw_vecs, w2p, a2p, g2p, d_rw, lora_wa, lora_g)

        h = _out_ffn(h, y_s5, y_rw, w_out[i].astype(BF16), gains,
                     w_up[i].astype(BF16), w_down[i].astype(BF16), final_norm=(i == depth - 1))
    return h
```

```python
import functools
import math

import numpy as np
import jax
import jax.numpy as jnp
from jax import lax
from jax.experimental import pallas as pl
from jax.experimental.pallas import tpu as pltpu

F32 = jnp.float32
BF16 = jnp.bfloat16

NORM_EPS = 1e-6
GN_EPS = 64e-5

S5_GROUP = 16
S5_STATE = 64
RWKV_HEAD = 64
LANES = 128
SUBLANES = 8
VMEM_LIMIT = 56 * 1024 * 1024

IN_TM = 1024
IN_TN = 512
S5_TC = 128
S5_STRIP = 512
RWKV_LB = 512
RWKV_SUB = 256
RWKV_C = 64
FFN_TM = 1024
FFN_CHUNK = 1024

MM = (((1,), (0,)), ((), ()))
NT = (((1,), (1,)), ((), ()))
TN = (((0,), (0,)), ((), ()))


def _dot(a, b, dims=MM):
    return lax.dot_general(a.astype(BF16), b.astype(BF16), dims,
                           preferred_element_type=F32)


def _dot_split_rhs(a_bf16, b, passes):
    acc = None
    rem = b
    for _ in range(passes):
        part = rem.astype(BF16)
        term = jnp.dot(a_bf16, part, preferred_element_type=F32)
        acc = term if acc is None else acc + term
        rem = rem - part.astype(F32)
    return acc


def _in_proj_kernel(x_ref, g_ref, mu_ref, w_ref, z_ref, carry_ref, *, segments):
    n_shift = mu_ref.shape[1]
    x = x_ref[0]
    tm = x.shape[0]

    @pl.when(pl.program_id(1) == 0)
    def _():
        carry_ref[...] = jnp.zeros_like(carry_ref)

    ms = jnp.mean(x * x, axis=-1, keepdims=True)
    xn = (x * lax.rsqrt(ms + NORM_EPS) * g_ref[...]).astype(BF16)
    for src0, src1, dst0, width in segments:
        for j0 in range(0, width, IN_TN):
            j1 = min(j0 + IN_TN, width)
            s0, s1 = src0 + j0, min(src0 + j1, src1)
            res = jnp.dot(xn, w_ref[:, s0:s1], preferred_element_type=F32)
            if s1 - s0 < j1 - j0:
                res = jnp.concatenate([res, jnp.zeros((tm, j1 - j0 - (s1 - s0)), F32)], axis=1)
            cols = slice(dst0 + j0, dst0 + j1)
            if dst0 + j1 <= n_shift:
                first = lax.broadcasted_iota(jnp.int32, res.shape, 0) == 0
                prev = jnp.where(first, carry_ref[0:1, cols], pltpu.roll(res, 1, 0))
                carry_ref[0:1, cols] = res[tm - 1:tm, :]
                res = res + (prev - res) * mu_ref[:, cols]
            else:
                assert dst0 + j0 >= n_shift
            z_ref[0, :, cols] = res.astype(BF16)


def _in_proj(x, g, mu, w, segments):
    bsz, seqlen, d_model = x.shape
    tm = IN_TM
    width = max(dst0 + n for _, _, dst0, n in segments)
    const = lambda shape: pl.BlockSpec(shape, lambda b, i: (0,) * len(shape))
    return pl.pallas_call(
        functools.partial(_in_proj_kernel, segments=segments),
        grid=(bsz, seqlen // tm),
        in_specs=[pl.BlockSpec((1, tm, d_model), lambda b, i: (b, i, 0)),
                  const(g.shape), const(mu.shape), const(w.shape)],
        out_specs=pl.BlockSpec((1, tm, width), lambda b, i: (b, i, 0)),
        out_shape=jax.ShapeDtypeStruct((bsz, seqlen, width), BF16),
        scratch_shapes=[pltpu.VMEM((SUBLANES, mu.shape[1]), F32)],
        compiler_params=pltpu.CompilerParams(
            dimension_semantics=("arbitrary", "arbitrary"),
            vmem_limit_bytes=VMEM_LIMIT),
        name="in_proj",
    )(x, g, mu, w)


def _s5_kernel(u_ref, wb_ref, lam_ref, wcre_ref, wcim_ref, vec_ref, wglu_ref, o_ref,
               st_ref, carry_ref, io_ref):
    bsz, tc, d_s5 = u_ref.shape
    n_state = lam_ref.shape[1]
    n_slabs_half = n_state // LANES
    n_io = d_s5 // LANES
    d_skip, b_glu, g_out = vec_ref[0:1, :], vec_ref[1:2, :], vec_ref[2:3, :]

    @pl.when(pl.program_id(0) == 0)
    def _():
        carry_ref[...] = jnp.zeros_like(carry_ref)

    for b in range(bsz):
        u_b = u_ref[b].astype(F32)
        for q in range(n_io):
            io_ref[q, pl.ds(b, tc, stride=bsz), :] = u_b[:, q * LANES:(q + 1) * LANES]
    u_f32 = jnp.concatenate([io_ref[q] for q in range(n_io)], axis=1)
    u_bf = u_f32.astype(BF16)

    n_tiles = wb_ref.shape[0]
    tile_w = wb_ref.shape[2]
    tiles_per_half = n_tiles // 2
    slabs_per_tile = tile_w // LANES
    slabs_per_strip = S5_STRIP // LANES
    tiles_per_strip = slabs_per_strip // slabs_per_tile
    n_strips = n_slabs_half // slabs_per_strip
    assert wcre_ref.shape[0] == n_strips and wcre_ref.shape[1] == S5_STRIP

    def b_tile(j):
        kb = (j % tiles_per_half) * tile_w // (S5_STATE // S5_GROUP * LANES)
        res = jnp.dot(u_bf[:, kb * LANES:(kb + 1) * LANES], wb_ref[j],
                      preferred_element_type=F32)
        for h in range(slabs_per_tile):
            st_ref[j * slabs_per_tile + h] = res[:, h * LANES:(h + 1) * LANES]

    def strip_tiles(s):
        re_tiles = [s * tiles_per_strip + q for q in range(tiles_per_strip)]
        return re_tiles + [tiles_per_half + j for j in re_tiles]

    ys = [None] * n_strips

    def c_part(m, part):
        base = m * slabs_per_strip + (n_slabs_half if part else 0)
        xs = jnp.concatenate([st_ref[base + q] for q in range(slabs_per_strip)], axis=1).astype(BF16)
        res = jnp.dot(xs, (wcim_ref if part else wcre_ref)[m], preferred_element_type=F32)
        ys[m] = res if ys[m] is None else ys[m] + res

    def scan(s, n_yields):
        re_ids = [s * slabs_per_strip + q for q in range(slabs_per_strip)]
        im_ids = [n_slabs_half + i for i in re_ids]
        a_re = [jnp.broadcast_to(lam_ref[0:1, i * LANES:(i + 1) * LANES], (bsz, LANES)) for i in re_ids]
        a_im = [jnp.broadcast_to(lam_ref[1:2, i * LANES:(i + 1) * LANES], (bsz, LANES)) for i in re_ids]
        s_re = [carry_ref[i] for i in re_ids]
        s_im = [carry_ref[i] for i in im_ids]
        every = max(tc // max(n_yields, 1), 1)
        for t in range(tc):
            rows = slice(t * bsz, (t + 1) * bsz)
            n_re = [a_re[q] * s_re[q] - a_im[q] * s_im[q] + st_ref[re_ids[q], rows, :]
                    for q in range(slabs_per_strip)]
            n_im = [a_re[q] * s_im[q] + a_im[q] * s_re[q] + st_ref[im_ids[q], rows, :]
                    for q in range(slabs_per_strip)]
            for q in range(slabs_per_strip):
                st_ref[re_ids[q], rows, :] = n_re[q]
                st_ref[im_ids[q], rows, :] = n_im[q]
            s_re, s_im = n_re, n_im
            if t % every == every - 1:
                yield
        for q in range(slabs_per_strip):
            carry_ref[re_ids[q]] = s_re[q]
            carry_ref[im_ids[q]] = s_im[q]

    def mxu_pieces(fns):
        for fn in fns:
            fn()
            yield

    for j in strip_tiles(0):
        b_tile(j)
    for s in range(n_strips):
        fns = []
        if s + 1 < n_strips:
            fns += [functools.partial(b_tile, j) for j in strip_tiles(s + 1)]
        if s >= 1:
            fns += [functools.partial(c_part, s - 1, part) for part in (0, 1)]
        _run_interleaved(scan(s, len(fns)), mxu_pieces(fns))
    for part in (0, 1):
        c_part(n_strips - 1, part)

    y = jnp.concatenate(ys, axis=-1) + d_skip * u_f32
    g = 0.5 * y * (1.0 + lax.erf(y * (1.0 / math.sqrt(2.0))))
    gate = jax.nn.sigmoid(
        jnp.dot(g.astype(BF16), wglu_ref[...], preferred_element_type=F32) + b_glu)
    out = g * gate
    ms = jnp.mean(out * out, axis=-1, keepdims=True)
    out = out * lax.rsqrt(ms + NORM_EPS) * g_out
    for q in range(n_io):
        io_ref[q] = out[:, q * LANES:(q + 1) * LANES]
    for b in range(bsz):
        o_ref[b] = jnp.concatenate(
            [io_ref[q, pl.ds(b, tc, stride=bsz), :] for q in range(n_io)], axis=1).astype(o_ref.dtype)


def _s5(z, col_block, wb, lam_bar, wc_re, wc_im, vecs, w_glu):
    bsz, seqlen, _ = z.shape
    d_s5 = vecs.shape[1]
    tc = S5_TC
    n_state = lam_bar.shape[1]
    const = lambda shape: pl.BlockSpec(shape, lambda i: (0,) * len(shape))
    return pl.pallas_call(
        _s5_kernel,
        grid=(seqlen // tc,),
        in_specs=[
            pl.BlockSpec((bsz, tc, d_s5), lambda i: (0, i, col_block)),
            const(wb.shape), const(lam_bar.shape), const(wc_re.shape), const(wc_im.shape),
            const(vecs.shape), const(w_glu.shape),
        ],
        out_specs=pl.BlockSpec((bsz, tc, d_s5), lambda i: (0, i, 0)),
        out_shape=jax.ShapeDtypeStruct((bsz, seqlen, d_s5), BF16),
        scratch_shapes=[
            pltpu.VMEM((2 * n_state // LANES, bsz * tc, LANES), F32),
            pltpu.VMEM((2 * n_state // LANES, bsz, LANES), F32),
            pltpu.VMEM((d_s5 // LANES, bsz * tc, LANES), F32),
        ],
        compiler_params=pltpu.CompilerParams(
            dimension_semantics=("arbitrary",),
            vmem_limit_bytes=VMEM_LIMIT),
        name="s5",
    )(z, wb, lam_bar, wc_re, wc_im, vecs, w_glu)


_W0, _A0, _KK, _KA, _RK, _LNW, _LNB = range(7)


def _seg_sum(x, ones_bd):
    w = ones_bd.shape[0]
    outs = [_dot(x[:, i * w:(i + 1) * w], ones_bd) for i in range(x.shape[1] // w)]
    return jnp.concatenate(outs, axis=-1)


def _head_ones(hd):
    seg_w = 2 * LANES
    li = lax.broadcasted_iota(jnp.int32, (seg_w, seg_w), 0) // hd
    lj = lax.broadcasted_iota(jnp.int32, (seg_w, seg_w), 1) // hd
    return jnp.where(li == lj, 1.0, 0.0).astype(BF16)


def _rwkv_prepare(zz, buf, vec_ref, w2_ref, a2_ref, g2_ref, *, d_rw, lora_wa, lora_g):
    lb = zz.shape[0]
    c = RWKV_C
    hd = RWKV_HEAD
    n_pairs = d_rw // LANES
    vec = lambda i: vec_ref[i:i + 1, :]

    o = 3 * d_rw
    r = zz[:, 0:d_rw]
    k = zz[:, d_rw:2 * d_rw]
    v = zz[:, 2 * d_rw:o]
    zwa = zz[:, o:o + lora_wa]
    zg = zz[:, o + lora_wa:o + lora_wa + lora_g]

    wpre = vec(_W0) + _dot(jnp.tanh(zwa), w2_ref[...])
    logd = -math.exp(-0.5) * jax.nn.sigmoid(wpre)
    a = jax.nn.sigmoid(vec(_A0) + _dot(zwa, a2_ref[...]))
    yield
    for name in ("lhs1", "rhs1", "v_bd", "kk_bd", "kb_end", "b_end", "q_t", "d_end"):
        buf[name] = {}
    buf["gate"] = _dot(jax.nn.sigmoid(zg), g2_ref[...])
    yield

    ones_bd = _head_ones(hd)
    kk = k * vec(_KK)
    kk = kk * lax.rsqrt(jnp.maximum(_seg_sum(kk * kk, ones_bd), 1e-24))
    yield
    k2 = k * (1.0 + (a - 1.0) * vec(_KA))
    buf["bonus"] = _seg_sum(r * k2 * vec(_RK), ones_bd) * v
    bvec = kk * a
    yield

    ci = lax.broadcasted_iota(jnp.int32, (c, c), 0)
    cj = lax.broadcasted_iota(jnp.int32, (c, c), 1)
    ltri = jnp.where(cj <= ci, 1.0, 0.0).astype(BF16)
    head0 = lax.broadcasted_iota(jnp.int32, (c, LANES), 1) < hd

    def stack_heads(x):
        return jnp.concatenate([jnp.where(head0, x, 0.0), jnp.where(head0, 0.0, x)], axis=0)

    for ic in range(lb // c):
        rows = slice(ic * c, (ic + 1) * c)
        logd_c = logd[rows]
        cum = _dot_split_rhs(ltri, logd_c, 2)
        cum_end = cum[c - 1:c, :]
        e_inv = jnp.exp(-cum)
        e_end = jnp.exp(cum_end - cum)
        q_c = r[rows] * jnp.exp(cum)
        kk_c = kk[rows] * jnp.exp(cum - logd_c)
        k_hat = k2[rows] * e_inv
        b_hat = bvec[rows] * e_inv
        k_end = k2[rows] * e_end
        b_end = bvec[rows] * e_end
        d_c = jnp.exp(cum_end)
        for p in range(n_pairs):
            ls = slice(p * LANES, (p + 1) * LANES)
            key = (ic, p)
            b_end_bd = stack_heads(b_end[:, ls])
            buf["q_t"][key] = q_c[:, ls]
            buf["d_end"][key] = d_c[:, ls]
            buf["kk_bd"][key] = stack_heads(kk_c[:, ls]).astype(BF16)
            buf["v_bd"][key] = stack_heads(v[rows, ls]).astype(BF16)
            buf["b_end"][key] = b_end_bd.astype(BF16)
            buf["lhs1"][key] = jnp.concatenate([q_c[:, ls], kk_c[:, ls]], axis=0).astype(BF16)
            buf["rhs1"][key] = jnp.concatenate(
                [stack_heads(k_hat[:, ls]), stack_heads(b_hat[:, ls])], axis=0).astype(BF16)
            buf["kb_end"][key] = jnp.concatenate(
                [stack_heads(k_end[:, ls]), b_end_bd], axis=0).astype(BF16)
            if p % 2:
                yield


def _rwkv_delta(buf, chunk_ids, state, y_rows, *, d_rw, state_ready):
    c = RWKV_C
    hd = RWKV_HEAD
    n_pairs = d_rw // LANES
    chains = [(ic, p) for ic in chunk_ids for p in range(n_pairs)]

    ri = lax.broadcasted_iota(jnp.int32, (c, LANES), 0)
    rl = lax.broadcasted_iota(jnp.int32, (c, LANES), 1)
    head0 = rl < hd
    incl = (rl % hd) <= ri
    strict = (rl % hd) < ri
    pi = lax.broadcasted_iota(jnp.int32, (2 * c, 2 * c), 0)
    pj = lax.broadcasted_iota(jnp.int32, (2 * c, 2 * c), 1)
    level = pi ^ pj
    diag = pi == pj
    eye = jnp.where(diag, 1.0, 0.0)
    n_levels = int(math.log2(c))

    def stack_heads(x):
        return jnp.concatenate([jnp.where(head0, x, 0.0), jnp.where(head0, 0.0, x)], axis=0)

    p1 = {k_: lax.dot_general(buf["lhs1"][k_], buf["rhs1"][k_], NT,
                              preferred_element_type=F32) for k_ in chains}
    a_rk = {k_: jnp.where(incl, p1[k_][0:c, 0:LANES], 0.0).astype(BF16) for k_ in chains}
    a_rb = {k_: jnp.where(incl, p1[k_][0:c, LANES:2 * LANES], 0.0).astype(BF16) for k_ in chains}
    a_kk = {k_: stack_heads(jnp.where(strict, p1[k_][c:2 * c, 0:LANES], 0.0)).astype(BF16)
            for k_ in chains}
    n_bd = {k_: stack_heads(jnp.where(strict, p1[k_][c:2 * c, LANES:2 * LANES], 0.0)) for k_ in chains}

    t_inv = {k_: eye - jnp.where(level == 1, n_bd[k_], 0.0) for k_ in chains}
    yield
    for lg in range(1, n_levels):
        m = 1 << lg
        starts = list(range(m, 2 * c, 2 * m))
        take = (lambda x: x) if m < SUBLANES else (
            lambda x: jnp.concatenate([x[s:s + m] for s in starts], axis=0))
        t_bf = {k_: t_inv[k_].astype(BF16) for k_ in chains}
        nt = {k_: _dot(take(jnp.where((level >> lg) == 1, n_bd[k_], 0.0)), t_bf[k_]) for k_ in chains}
        yield
        if m < SUBLANES:
            t_inv = {k_: t_inv[k_] - _dot(t_bf[k_], nt[k_]) for k_ in chains}
        else:
            zero = jnp.zeros((m, LANES), F32)
            for k_ in chains:
                nt_full = jnp.concatenate(
                    [piece for i in range(len(starts)) for piece in (zero, nt[k_][i * m:(i + 1) * m])], axis=0)
                t_r = take(t_inv[k_])
                t_r = t_r - _dot(t_r, nt_full)
                t_inv[k_] = jnp.concatenate(
                    [piece for i, s in enumerate(starts)
                     for piece in (t_inv[k_][s - m:s], t_r[i * m:(i + 1) * m])], axis=0)
        yield
    t_bf = {k_: t_inv[k_].astype(BF16) for k_ in chains}

    av = {k_: _dot(a_kk[k_], buf["v_bd"][k_]).astype(BF16) for k_ in chains}
    yield
    w12 = {k_: _dot(t_bf[k_], jnp.concatenate([buf["kk_bd"][k_], av[k_]], axis=1))
           for k_ in chains}
    w1 = {k_: w12[k_][:, 0:LANES].astype(BF16) for k_ in chains}
    vw = {k_: jnp.concatenate([buf["v_bd"][k_], (-w12[k_][:, LANES:2 * LANES]).astype(BF16)], axis=0)
          for k_ in chains}
    yield

    y_intra = {k_: _dot(jnp.concatenate([a_rk[k_], a_rb[k_]], axis=1), vw[k_]) for k_ in chains}
    yield
    q_eff = {k_: (buf["q_t"][k_] - _dot(a_rb[k_], w1[k_])).astype(BF16) for k_ in chains}
    yield
    g_mat = {k_: _dot(buf["kb_end"][k_], vw[k_], TN) for k_ in chains}
    yield
    m_mat = {k_: (jnp.where(diag, jnp.broadcast_to(buf["d_end"][k_], (2 * c, LANES)), 0.0)
                  - _dot(buf["b_end"][k_], w1[k_], TN)).astype(BF16) for k_ in chains}
    yield

    assert state_ready()
    for ic in chunk_ids:
        y_c = [_dot(q_eff[(ic, p)], state[p]) + y_intra[(ic, p)] for p in range(n_pairs)]
        state[:] = [_dot(m_mat[(ic, p)], state[p]) + g_mat[(ic, p)] for p in range(n_pairs)]
        y_rows.append(jnp.concatenate(y_c, axis=1))
        yield


def _rwkv_finish(buf, y_rows, vec_ref, o_ref, rows):
    hd = RWKV_HEAD
    y = jnp.concatenate(y_rows, axis=0)
    ones_bd = _head_ones(hd)
    mean = _seg_sum(y, ones_bd) * (1.0 / hd)
    yc = y - mean
    var = _seg_sum(yc * yc, ones_bd) * (1.0 / hd)
    yn = yc * lax.rsqrt(var + GN_EPS) * vec_ref[_LNW:_LNW + 1, :] + vec_ref[_LNB:_LNB + 1, :]
    o_ref[0, rows, :] = ((yn + buf["bonus"]) * buf["gate"]).astype(o_ref.dtype)


def _run_interleaved(*gens):
    live = list(gens)
    while live:
        for gen in list(live):
            if next(gen, StopIteration) is StopIteration:
                live.remove(gen)


def _rwkv_kernel(z_ref, vec_ref, w2_ref, a2_ref, g2_ref, o_ref, st_ref, *, d_rw, lora_wa, lora_g):
    @pl.when(pl.program_id(1) == 0)
    def _():
        st_ref[...] = jnp.zeros_like(st_ref)

    n_pairs = d_rw // LANES
    sub = RWKV_SUB
    n_sub = z_ref.shape[1] // sub
    chunks = list(range(sub // RWKV_C))
    state = [st_ref[p] for p in range(n_pairs)]

    bufs, prepares = [], []
    for i in range(n_sub):
        zz = z_ref[0, i * sub:(i + 1) * sub, :].astype(F32)
        bufs.append({})
        prepares.append(_rwkv_prepare(zz, bufs[i], vec_ref, w2_ref, a2_ref, g2_ref,
                                      d_rw=d_rw, lora_wa=lora_wa, lora_g=lora_g))

    finished = []

    def delta_task(i):
        y_rows = []
        yield from _rwkv_delta(bufs[i], chunks, state, y_rows, d_rw=d_rw,
                               state_ready=lambda: len(finished) == i)
        _rwkv_finish(bufs[i], y_rows, vec_ref, o_ref, slice(i * sub, (i + 1) * sub))
        finished.append(i)

    _run_interleaved(prepares[0])
    live = [delta_task(0)] + prepares[1:2]
    started = 1
    while live:
        for gen in list(live):
            if next(gen, StopIteration) is StopIteration:
                live.remove(gen)
                if started < n_sub and gen is prepares[started]:
                    live.append(delta_task(started))
                    live.extend(prepares[started + 1:started + 2])
                    started += 1
    for p in range(n_pairs):
        st_ref[p] = state[p]


def _rwkv(z, zw, vecs, w2p, a2p, g2p, d_rw, lora_wa, lora_g):
    bsz, seqlen, _ = z.shape
    lb = RWKV_LB
    const = lambda shape: pl.BlockSpec(shape, lambda b, j: (0,) * len(shape))
    params = (vecs, w2p, a2p, g2p)
    return pl.pallas_call(
        functools.partial(_rwkv_kernel, d_rw=d_rw, lora_wa=lora_wa, lora_g=lora_g),
        grid=(bsz, seqlen // lb),
        in_specs=[pl.BlockSpec((1, lb, zw), lambda b, j: (b, j, 0))]
                 + [const(p.shape) for p in params],
        out_specs=pl.BlockSpec((1, lb, d_rw), lambda b, j: (b, j, 0)),
        out_shape=jax.ShapeDtypeStruct((bsz, seqlen, d_rw), BF16),
        scratch_shapes=[pltpu.VMEM((d_rw // LANES, LANES, LANES), F32)],
        compiler_params=pltpu.CompilerParams(
            dimension_semantics=("arbitrary", "arbitrary"),
            vmem_limit_bytes=VMEM_LIMIT),
        name="rwkv",
    )(z, *params)


def _out_ffn_kernel(x_ref, ys_ref, yr_ref, wo_ref, g_ref, wu_ref, wd_ref, o_ref, *, final_norm):
    x = x_ref[0]
    mix = jnp.concatenate([ys_ref[0], yr_ref[0]], axis=1)
    h = x + jnp.dot(mix, wo_ref[...], preferred_element_type=F32)
    ms = jnp.mean(h * h, axis=-1, keepdims=True)
    hn = (h * lax.rsqrt(ms + NORM_EPS) * g_ref[0:1, :]).astype(BF16)
    d_ff = wu_ref.shape[1]
    acc = h
    for j in range(d_ff // FFN_CHUNK):
        sl = slice(j * FFN_CHUNK, (j + 1) * FFN_CHUNK)
        up = jnp.dot(hn, wu_ref[:, sl], preferred_element_type=F32)
        up = jnp.square(jnp.maximum(up, 0.0)).astype(BF16)
        acc = acc + jnp.dot(up, wd_ref[sl, :], preferred_element_type=F32)
    if final_norm:
        ms2 = jnp.mean(acc * acc, axis=-1, keepdims=True)
        acc = acc * lax.rsqrt(ms2 + NORM_EPS) * g_ref[1:2, :]
    o_ref[0] = acc


def _out_ffn(x, y_s5, y_rw, w_out, gains, w_up, w_down, final_norm):
    bsz, seqlen, d_model = x.shape
    tm = FFN_TM
    const = lambda shape: pl.BlockSpec(shape, lambda b, i: (0,) * len(shape),
                                       pipeline_mode=pl.Buffered(1))
    act = lambda width: pl.BlockSpec((1, tm, width), lambda b, i: (b, i, 0))
    return pl.pallas_call(
        functools.partial(_out_ffn_kernel, final_norm=final_norm),
        grid=(bsz, seqlen // tm),
        in_specs=[
            act(d_model), act(y_s5.shape[2]), act(y_rw.shape[2]),
            const(w_out.shape), const(gains.shape), const(w_up.shape), const(w_down.shape),
        ],
        out_specs=act(d_model),
        out_shape=jax.ShapeDtypeStruct((bsz, seqlen, d_model), x.dtype),
        compiler_params=pltpu.CompilerParams(
            dimension_semantics=("arbitrary", "arbitrary"),
            vmem_limit_bytes=VMEM_LIMIT),
        name="out_ffn",
    )(x, y_s5, y_rw, w_out, gains, w_up, w_down)


def _round_up(n, m):
    return -(-n // m) * m


def _s5_weights(lam_re, lam_im, log_dt, b_re, b_im, c_re, c_im):
    groups, n_p = lam_re.shape
    n_h = b_re.shape[2]
    dt = jnp.exp(log_dt.astype(F32))[:, None]
    lr, li = lam_re.astype(F32), lam_im.astype(F32)
    mag = jnp.exp(lr * dt)
    a_re = mag * jnp.cos(li * dt)
    a_im = mag * jnp.sin(li * dt)
    den = lr * lr + li * li
    q_re = ((a_re - 1.0) * lr + a_im * li) / den
    q_im = (a_im * lr - (a_re - 1.0) * li) / den
    bb_re = q_re[:, :, None] * b_re - q_im[:, :, None] * b_im
    bb_im = q_re[:, :, None] * b_im + q_im[:, :, None] * b_re

    tile_groups = 2 * LANES // n_p
    k_groups = LANES // n_h
    n_tiles = groups // tile_groups
    place = np.zeros((n_tiles, k_groups, tile_groups), np.float32)
    for j in range(n_tiles):
        for g in range(tile_groups):
            place[j, (j * tile_groups + g) % k_groups, g] = 1.0

    def b_tiles(bb):
        t = jnp.einsum('jgph,jlg->jlhgp', bb.reshape(n_tiles, tile_groups, n_p, n_h), place)
        return t.reshape(n_tiles, LANES, 2 * LANES)

    wb = jnp.concatenate([b_tiles(bb_re), b_tiles(bb_im)], axis=0).astype(BF16)

    n_out = groups // k_groups
    eye = np.eye(k_groups, dtype=np.float32)

    def c_tiles(cc):
        t = jnp.einsum('mghp,gk->mgpkh', cc.astype(F32).reshape(n_out, k_groups, n_h, n_p), eye)
        return t.reshape(n_out, k_groups * n_p, LANES).astype(BF16)

    lam_bar = jnp.stack([a_re.reshape(-1), a_im.reshape(-1)])
    return wb, lam_bar, c_tiles(c_re), c_tiles(-c_im)


def kernel(x, norm_mix_g, w_in, lam_re, lam_im, log_dt, b_re, b_im, c_re, c_im, d_skip, w_glu, b_glu, s5_out_g, mu_shift, w0, w2, a0, a2, g2, k_k, k_a, r_k, ln_x_w, ln_x_b, w_out, norm_ffn_g, w_up, w_down, norm_final_g):
    depth, d_model, _ = w_in.shape
    d_s5 = d_skip.shape[1]
    d_rw = w0.shape[1]
    n_w, n_a, n_g = w2.shape[1], a2.shape[1], g2.shape[1]
    lora_wa = _round_up(n_w + n_a, LANES)
    lora_g = _round_up(n_g, LANES)
    d_rw_in = 3 * d_rw + n_w + n_a + n_g
    zw = _round_up(3 * d_rw + lora_wa + lora_g, d_s5)
    assert n_w + n_a == lora_wa and zw >= d_rw_in

    h = x
    for i in range(depth):
        main = d_rw_in // LANES * LANES
        d_in = d_s5 + d_rw_in
        segments = ((d_s5, d_s5 + main, 0, main), (d_s5 + main, d_in, main, zw - main), (0, d_s5, zw, d_s5))
        mu = jnp.pad(mu_shift[i].astype(F32), (0, zw - d_rw_in)).reshape(1, zw)
        w2p = jnp.pad(w2[i], ((0, lora_wa - n_w), (0, 0))).astype(BF16)
        a2p = jnp.pad(a2[i], ((n_w, lora_wa - n_w - n_a), (0, 0))).astype(BF16)
        g2p = jnp.pad(g2[i], ((0, lora_g - n_g), (0, 0))).astype(BF16)
        rw_vecs = jnp.stack([w0[i], a0[i], k_k[i], k_a[i], r_k[i].reshape(-1), ln_x_w[i], ln_x_b[i],
                             jnp.zeros_like(w0[i])]).astype(F32)
        s5_vecs = jnp.stack([d_skip[i], b_glu[i], s5_out_g[i]]).astype(F32)
        gains = jnp.stack([norm_ffn_g[i], norm_final_g]).astype(F32)

        z = _in_proj(h, norm_mix_g[i].reshape(1, -1).astype(F32), mu, w_in[i].astype(BF16), segments)

        wb, lam_bar, wc_re, wc_im = _s5_weights(
            lam_re[i], lam_im[i], log_dt[i], b_re[i], b_im[i], c_re[i], c_im[i])
        y_s5 = _s5(z, zw // d_s5, wb, lam_bar, wc_re, wc_im, s5_vecs, w_glu[i].astype(BF16))
        y_rw = _rwkv(z, zw, rw_vecs, w2p, a2p, g2p, d_rw, lora_wa, lora_g)

        h = _out_ffn(h, y_s5, y_rw, w_out[i].astype(BF16), gains,
                     w_up[i].astype(BF16), w_down[i].astype(BF16), final_norm=(i == depth - 1))
    return h
```

```python
import functools
import math

import numpy as np
import jax
import jax.numpy as jnp
from jax import lax
from jax.experimental import pallas as pl
from jax.experimental.pallas import tpu as pltpu

F32 = jnp.float32
BF16 = jnp.bfloat16

NORM_EPS = 1e-6
GN_EPS = 64e-5

S5_GROUP = 16
S5_STATE = 64
RWKV_HEAD = 64
LANES = 128
SUBLANES = 8
VMEM_LIMIT = 56 * 1024 * 1024

IN_TM = 1024
IN_TN = 512
S5_TC = 128
S5_STRIP = 512
RWKV_LB = 1024
RWKV_SUB = 256
RWKV_C = 64
FFN_TM = 1024
FFN_CHUNK = 1024

MM = (((1,), (0,)), ((), ()))
NT = (((1,), (1,)), ((), ()))
TN = (((0,), (0,)), ((), ()))


def _dot(a, b, dims=MM):
    return lax.dot_general(a.astype(BF16), b.astype(BF16), dims,
                           preferred_element_type=F32)


def _dot_split_rhs(a_bf16, b, passes):
    acc = None
    rem = b
    for _ in range(passes):
        part = rem.astype(BF16)
        term = jnp.dot(a_bf16, part, preferred_element_type=F32)
        acc = term if acc is None else acc + term
        rem = rem - part.astype(F32)
    return acc


def _in_proj_kernel(x_ref, g_ref, mu_ref, w_ref, z_ref, carry_ref, *, segments):
    n_shift = mu_ref.shape[1]
    x = x_ref[0]
    tm = x.shape[0]

    @pl.when(pl.program_id(1) == 0)
    def _():
        carry_ref[...] = jnp.zeros_like(carry_ref)

    ms = jnp.mean(x * x, axis=-1, keepdims=True)
    xn = (x * lax.rsqrt(ms + NORM_EPS) * g_ref[...]).astype(BF16)
    for src0, src1, dst0, width in segments:
        for j0 in range(0, width, IN_TN):
            j1 = min(j0 + IN_TN, width)
            s0, s1 = src0 + j0, min(src0 + j1, src1)
            res = jnp.dot(xn, w_ref[:, s0:s1].astype(BF16), preferred_element_type=F32)
            if s1 - s0 < j1 - j0:
                res = jnp.concatenate([res, jnp.zeros((tm, j1 - j0 - (s1 - s0)), F32)], axis=1)
            cols = slice(dst0 + j0, dst0 + j1)
            if dst0 + j1 <= n_shift:
                first = lax.broadcasted_iota(jnp.int32, res.shape, 0) == 0
                prev = jnp.where(first, carry_ref[0:1, cols], pltpu.roll(res, 1, 0))
                carry_ref[0:1, cols] = res[tm - 1:tm, :]
                res = res + (prev - res) * mu_ref[:, cols]
            else:
                assert dst0 + j0 >= n_shift
            z_ref[0, :, cols] = res.astype(BF16)


def _in_proj(x, g, mu, w, segments):
    bsz, seqlen, d_model = x.shape
    tm = IN_TM
    width = max(dst0 + n for _, _, dst0, n in segments)
    const = lambda shape: pl.BlockSpec(shape, lambda b, i: (0,) * len(shape))
    return pl.pallas_call(
        functools.partial(_in_proj_kernel, segments=segments),
        grid=(bsz, seqlen // tm),
        in_specs=[pl.BlockSpec((1, tm, d_model), lambda b, i: (b, i, 0)),
                  const(g.shape), const(mu.shape),
                  pl.BlockSpec(w.shape, lambda b, i: (0, 0), pipeline_mode=pl.Buffered(1))],
        out_specs=pl.BlockSpec((1, tm, width), lambda b, i: (b, i, 0)),
        out_shape=jax.ShapeDtypeStruct((bsz, seqlen, width), BF16),
        scratch_shapes=[pltpu.VMEM((SUBLANES, mu.shape[1]), F32)],
        compiler_params=pltpu.CompilerParams(
            dimension_semantics=("arbitrary", "arbitrary"),
            vmem_limit_bytes=VMEM_LIMIT),
        name="in_proj",
    )(x, g, mu, w)


def _s5_kernel(u_ref, wb_ref, lam_ref, wcre_ref, wcim_ref, vec_ref, wglu_ref, o_ref,
               st_ref, carry_ref, io_ref):
    bsz, tc, d_s5 = u_ref.shape
    n_state = lam_ref.shape[1]
    n_slabs_half = n_state // LANES
    n_io = d_s5 // LANES
    d_skip, b_glu, g_out = vec_ref[0:1, :], vec_ref[1:2, :], vec_ref[2:3, :]

    @pl.when(pl.program_id(0) == 0)
    def _():
        carry_ref[...] = jnp.zeros_like(carry_ref)

    for b in range(bsz):
        u_b = u_ref[b].astype(F32)
        for q in range(n_io):
            io_ref[q, pl.ds(b, tc, stride=bsz), :] = u_b[:, q * LANES:(q + 1) * LANES]
    u_f32 = jnp.concatenate([io_ref[q] for q in range(n_io)], axis=1)
    u_bf = u_f32.astype(BF16)

    n_tiles = wb_ref.shape[0]
    tile_w = wb_ref.shape[2]
    tiles_per_half = n_tiles // 2
    slabs_per_tile = tile_w // LANES
    slabs_per_strip = S5_STRIP // LANES
    tiles_per_strip = slabs_per_strip // slabs_per_tile
    n_strips = n_slabs_half // slabs_per_strip
    assert wcre_ref.shape[0] == n_strips and wcre_ref.shape[1] == S5_STRIP

    def b_tile(j):
        kb = (j % tiles_per_half) * tile_w // (S5_STATE // S5_GROUP * LANES)
        res = jnp.dot(u_bf[:, kb * LANES:(kb + 1) * LANES], wb_ref[j],
                      preferred_element_type=F32)
        for h in range(slabs_per_tile):
            st_ref[j * slabs_per_tile + h] = res[:, h * LANES:(h + 1) * LANES]

    def strip_tiles(s):
        re_tiles = [s * tiles_per_strip + q for q in range(tiles_per_strip)]
        return re_tiles + [tiles_per_half + j for j in re_tiles]

    ys = [None] * n_strips

    def c_part(m, part):
        base = m * slabs_per_strip + (n_slabs_half if part else 0)
        xs = jnp.concatenate([st_ref[base + q] for q in range(slabs_per_strip)], axis=1).astype(BF16)
        res = jnp.dot(xs, (wcim_ref if part else wcre_ref)[m], preferred_element_type=F32)
        ys[m] = res if ys[m] is None else ys[m] + res

    def scan(s, n_yields):
        re_ids = [s * slabs_per_strip + q for q in range(slabs_per_strip)]
        im_ids = [n_slabs_half + i for i in re_ids]
        a_re = [jnp.broadcast_to(lam_ref[0:1, i * LANES:(i + 1) * LANES], (bsz, LANES)) for i in re_ids]
        a_im = [jnp.broadcast_to(lam_ref[1:2, i * LANES:(i + 1) * LANES], (bsz, LANES)) for i in re_ids]
        s_re = [carry_ref[i] for i in re_ids]
        s_im = [carry_ref[i] for i in im_ids]
        every = max(tc // max(n_yields, 1), 1)
        for t in range(tc):
            rows = slice(t * bsz, (t + 1) * bsz)
            n_re = [a_re[q] * s_re[q] - a_im[q] * s_im[q] + st_ref[re_ids[q], rows, :]
                    for q in range(slabs_per_strip)]
            n_im = [a_re[q] * s_im[q] + a_im[q] * s_re[q] + st_ref[im_ids[q], rows, :]
                    for q in range(slabs_per_strip)]
            for q in range(slabs_per_strip):
                st_ref[re_ids[q], rows, :] = n_re[q]
                st_ref[im_ids[q], rows, :] = n_im[q]
            s_re, s_im = n_re, n_im
            if t % every == every - 1:
                yield
        for q in range(slabs_per_strip):
            carry_ref[re_ids[q]] = s_re[q]
            carry_ref[im_ids[q]] = s_im[q]

    def mxu_pieces(fns):
        for fn in fns:
            fn()
            yield

    for j in strip_tiles(0):
        b_tile(j)
    for s in range(n_strips):
        fns = []
        if s + 1 < n_strips:
            fns += [functools.partial(b_tile, j) for j in strip_tiles(s + 1)]
        if s >= 1:
            fns += [functools.partial(c_part, s - 1, part) for part in (0, 1)]
        _run_interleaved(scan(s, len(fns)), mxu_pieces(fns))
    for part in (0, 1):
        c_part(n_strips - 1, part)

    y = jnp.concatenate(ys, axis=-1) + d_skip * u_f32
    g = 0.5 * y * (1.0 + lax.erf(y * (1.0 / math.sqrt(2.0))))
    gate = jax.nn.sigmoid(
        jnp.dot(g.astype(BF16), wglu_ref[...], preferred_element_type=F32) + b_glu)
    out = g * gate
    ms = jnp.mean(out * out, axis=-1, keepdims=True)
    out = out * lax.rsqrt(ms + NORM_EPS) * g_out
    for q in range(n_io):
        io_ref[q] = out[:, q * LANES:(q + 1) * LANES]
    for b in range(bsz):
        o_ref[b] = jnp.concatenate(
            [io_ref[q, pl.ds(b, tc, stride=bsz), :] for q in range(n_io)], axis=1).astype(o_ref.dtype)


def _s5(z, col_block, wb, lam_bar, wc_re, wc_im, vecs, w_glu):
    bsz, seqlen, _ = z.shape
    d_s5 = vecs.shape[1]
    tc = S5_TC
    n_state = lam_bar.shape[1]
    const = lambda shape: pl.BlockSpec(shape, lambda i: (0,) * len(shape))
    return pl.pallas_call(
        _s5_kernel,
        grid=(seqlen // tc,),
        in_specs=[
            pl.BlockSpec((bsz, tc, d_s5), lambda i: (0, i, col_block)),
            const(wb.shape), const(lam_bar.shape), const(wc_re.shape), const(wc_im.shape),
            const(vecs.shape), const(w_glu.shape),
        ],
        out_specs=pl.BlockSpec((bsz, tc, d_s5), lambda i: (0, i, 0)),
        out_shape=jax.ShapeDtypeStruct((bsz, seqlen, d_s5), BF16),
        scratch_shapes=[
            pltpu.VMEM((2 * n_state // LANES, bsz * tc, LANES), F32),
            pltpu.VMEM((2 * n_state // LANES, bsz, LANES), F32),
            pltpu.VMEM((d_s5 // LANES, bsz * tc, LANES), F32),
        ],
        compiler_params=pltpu.CompilerParams(
            dimension_semantics=("arbitrary",),
            vmem_limit_bytes=VMEM_LIMIT),
        name="s5",
    )(z, wb, lam_bar, wc_re, wc_im, vecs, w_glu)


_W0, _A0, _KK, _KA, _RK, _LNW, _LNB = range(7)


def _seg_sum(x, ones_bd):
    w = ones_bd.shape[0]
    outs = [_dot(x[:, i * w:(i + 1) * w], ones_bd) for i in range(x.shape[1] // w)]
    return jnp.concatenate(outs, axis=-1)


def _head_ones(hd):
    seg_w = 2 * LANES
    li = lax.broadcasted_iota(jnp.int32, (seg_w, seg_w), 0) // hd
    lj = lax.broadcasted_iota(jnp.int32, (seg_w, seg_w), 1) // hd
    return jnp.where(li == lj, 1.0, 0.0).astype(BF16)


def _rwkv_prepare(zz, buf, vec_ref, w2_ref, a2_ref, g2_ref, *, d_rw, lora_wa, lora_g):
    lb = zz.shape[0]
    c = RWKV_C
    hd = RWKV_HEAD
    n_pairs = d_rw // LANES
    vec = lambda i: vec_ref[i:i + 1, :]

    o = 3 * d_rw
    r = zz[:, 0:d_rw]
    k = zz[:, d_rw:2 * d_rw]
    v = zz[:, 2 * d_rw:o]
    zwa = zz[:, o:o + lora_wa]
    zg = zz[:, o + lora_wa:o + lora_wa + lora_g]

    wpre = vec(_W0) + _dot(jnp.tanh(zwa), w2_ref[...])
    logd = -math.exp(-0.5) * jax.nn.sigmoid(wpre)
    a = jax.nn.sigmoid(vec(_A0) + _dot(zwa, a2_ref[...]))
    yield
    for name in ("lhs1", "rhs1", "v_bd", "kk_bd", "kb_end", "b_end", "q_t", "d_end"):
        buf[name] = {}
    buf["gate"] = _dot(jax.nn.sigmoid(zg), g2_ref[...])
    yield

    ones_bd = _head_ones(hd)
    kk = k * vec(_KK)
    kk = kk * lax.rsqrt(jnp.maximum(_seg_sum(kk * kk, ones_bd), 1e-24))
    yield
    k2 = k * (1.0 + (a - 1.0) * vec(_KA))
    buf["bonus"] = _seg_sum(r * k2 * vec(_RK), ones_bd) * v
    bvec = kk * a
    yield

    ci = lax.broadcasted_iota(jnp.int32, (c, c), 0)
    cj = lax.broadcasted_iota(jnp.int32, (c, c), 1)
    ltri = jnp.where(cj <= ci, 1.0, 0.0).astype(BF16)
    head0 = lax.broadcasted_iota(jnp.int32, (c, LANES), 1) < hd

    def stack_heads(x):
        return jnp.concatenate([jnp.where(head0, x, 0.0), jnp.where(head0, 0.0, x)], axis=0)

    for ic in range(lb // c):
        rows = slice(ic * c, (ic + 1) * c)
        logd_c = logd[rows]
        cum = _dot_split_rhs(ltri, logd_c, 2)
        cum_end = cum[c - 1:c, :]
        e_inv = jnp.exp(-cum)
        e_end = jnp.exp(cum_end - cum)
        q_c = r[rows] * jnp.exp(cum)
        kk_c = kk[rows] * jnp.exp(cum - logd_c)
        k_hat = k2[rows] * e_inv
        b_hat = bvec[rows] * e_inv
        k_end = k2[rows] * e_end
        b_end = bvec[rows] * e_end
        d_c = jnp.exp(cum_end)
        for p in range(n_pairs):
            ls = slice(p * LANES, (p + 1) * LANES)
            key = (ic, p)
            b_end_bd = stack_heads(b_end[:, ls])
            buf["q_t"][key] = q_c[:, ls]
            buf["d_end"][key] = d_c[:, ls]
            buf["kk_bd"][key] = stack_heads(kk_c[:, ls]).astype(BF16)
            buf["v_bd"][key] = stack_heads(v[rows, ls]).astype(BF16)
            buf["b_end"][key] = b_end_bd.astype(BF16)
            buf["lhs1"][key] = jnp.concatenate([q_c[:, ls], kk_c[:, ls]], axis=0).astype(BF16)
            buf["rhs1"][key] = jnp.concatenate(
                [stack_heads(k_hat[:, ls]), stack_heads(b_hat[:, ls])], axis=0).astype(BF16)
            buf["kb_end"][key] = jnp.concatenate(
                [stack_heads(k_end[:, ls]), b_end_bd], axis=0).astype(BF16)
            if p % 2:
                yield


def _rwkv_delta(buf, chunk_ids, state, y_rows, *, d_rw, state_ready):
    c = RWKV_C
    hd = RWKV_HEAD
    n_pairs = d_rw // LANES
    chains = [(ic, p) for ic in chunk_ids for p in range(n_pairs)]

    ri = lax.broadcasted_iota(jnp.int32, (c, LANES), 0)
    rl = lax.broadcasted_iota(jnp.int32, (c, LANES), 1)
    head0 = rl < hd
    incl = (rl % hd) <= ri
    strict = (rl % hd) < ri
    pi = lax.broadcasted_iota(jnp.int32, (2 * c, 2 * c), 0)
    pj = lax.broadcasted_iota(jnp.int32, (2 * c, 2 * c), 1)
    level = pi ^ pj
    diag = pi == pj
    eye = jnp.where(diag, 1.0, 0.0)
    n_levels = int(math.log2(c))

    def stack_heads(x):
        return jnp.concatenate([jnp.where(head0, x, 0.0), jnp.where(head0, 0.0, x)], axis=0)

    p1 = {k_: lax.dot_general(buf["lhs1"][k_], buf["rhs1"][k_], NT,
                              preferred_element_type=F32) for k_ in chains}
    a_rk = {k_: jnp.where(incl, p1[k_][0:c, 0:LANES], 0.0).astype(BF16) for k_ in chains}
    a_rb = {k_: jnp.where(incl, p1[k_][0:c, LANES:2 * LANES], 0.0).astype(BF16) for k_ in chains}
    a_kk = {k_: stack_heads(jnp.where(strict, p1[k_][c:2 * c, 0:LANES], 0.0)).astype(BF16)
            for k_ in chains}
    n_bd = {k_: stack_heads(jnp.where(strict, p1[k_][c:2 * c, LANES:2 * LANES], 0.0)) for k_ in chains}

    t_inv = {k_: eye - jnp.where(level == 1, n_bd[k_], 0.0) for k_ in chains}
    yield
    for lg in range(1, n_levels):
        m = 1 << lg
        starts = list(range(m, 2 * c, 2 * m))
        take = (lambda x: x) if m < SUBLANES else (
            lambda x: jnp.concatenate([x[s:s + m] for s in starts], axis=0))
        t_bf = {k_: t_inv[k_].astype(BF16) for k_ in chains}
        nt = {k_: _dot(take(jnp.where((level >> lg) == 1, n_bd[k_], 0.0)), t_bf[k_]) for k_ in chains}
        yield
        if m < SUBLANES:
            t_inv = {k_: t_inv[k_] - _dot(t_bf[k_], nt[k_]) for k_ in chains}
        else:
            zero = jnp.zeros((m, LANES), F32)
            for k_ in chains:
                nt_full = jnp.concatenate(
                    [piece for i in range(len(starts)) for piece in (zero, nt[k_][i * m:(i + 1) * m])], axis=0)
                t_r = take(t_inv[k_])
                t_r = t_r - _dot(t_r, nt_full)
                t_inv[k_] = jnp.concatenate(
                    [piece for i, s in enumerate(starts)
                     for piece in (t_inv[k_][s - m:s], t_r[i * m:(i + 1) * m])], axis=0)
        yield
    t_bf = {k_: t_inv[k_].astype(BF16) for k_ in chains}

    av = {k_: _dot(a_kk[k_], buf["v_bd"][k_]).astype(BF16) for k_ in chains}
    yield
    w12 = {k_: _dot(t_bf[k_], jnp.concatenate([buf["kk_bd"][k_], av[k_]], axis=1))
           for k_ in chains}
    w1 = {k_: w12[k_][:, 0:LANES].astype(BF16) for k_ in chains}
    vw = {k_: jnp.concatenate([buf["v_bd"][k_], (-w12[k_][:, LANES:2 * LANES]).astype(BF16)], axis=0)
          for k_ in chains}
    yield

    y_intra = {k_: _dot(jnp.concatenate([a_rk[k_], a_rb[k_]], axis=1), vw[k_]) for k_ in chains}
    yield
    q_eff = {k_: (buf["q_t"][k_] - _dot(a_rb[k_], w1[k_])).astype(BF16) for k_ in chains}
    yield
    g_mat = {k_: _dot(buf["kb_end"][k_], vw[k_], TN) for k_ in chains}
    yield
    m_mat = {k_: (jnp.where(diag, jnp.broadcast_to(buf["d_end"][k_], (2 * c, LANES)), 0.0)
                  - _dot(buf["b_end"][k_], w1[k_], TN)).astype(BF16) for k_ in chains}
    yield

    assert state_ready()
    for ic in chunk_ids:
        y_c = [_dot(q_eff[(ic, p)], state[p]) + y_intra[(ic, p)] for p in range(n_pairs)]
        state[:] = [_dot(m_mat[(ic, p)], state[p]) + g_mat[(ic, p)] for p in range(n_pairs)]
        y_rows.append(jnp.concatenate(y_c, axis=1))
        yield


def _rwkv_finish(buf, y_rows, vec_ref, o_ref, rows):
    hd = RWKV_HEAD
    y = jnp.concatenate(y_rows, axis=0)
    ones_bd = _head_ones(hd)
    mean = _seg_sum(y, ones_bd) * (1.0 / hd)
    yc = y - mean
    var = _seg_sum(yc * yc, ones_bd) * (1.0 / hd)
    yn = yc * lax.rsqrt(var + GN_EPS) * vec_ref[_LNW:_LNW + 1, :] + vec_ref[_LNB:_LNB + 1, :]
    o_ref[0, rows, :] = ((yn + buf["bonus"]) * buf["gate"]).astype(o_ref.dtype)


def _run_interleaved(*gens):
    live = list(gens)
    while live:
        for gen in list(live):
            if next(gen, StopIteration) is StopIteration:
                live.remove(gen)


def _rwkv_kernel(z_ref, vec_ref, w2_ref, a2_ref, g2_ref, o_ref, st_ref, *, d_rw, lora_wa, lora_g):
    @pl.when(pl.program_id(1) == 0)
    def _():
        st_ref[...] = jnp.zeros_like(st_ref)

    n_pairs = d_rw // LANES
    sub = RWKV_SUB
    n_sub = z_ref.shape[1] // sub
    chunks = list(range(sub // RWKV_C))
    state = [st_ref[p] for p in range(n_pairs)]

    bufs, prepares = [], []
    for i in range(n_sub):
        zz = z_ref[0, i * sub:(i + 1) * sub, :].astype(F32)
        bufs.append({})
        prepares.append(_rwkv_prepare(zz, bufs[i], vec_ref, w2_ref, a2_ref, g2_ref,
                                      d_rw=d_rw, lora_wa=lora_wa, lora_g=lora_g))

    finished = []

    def delta_task(i):
        y_rows = []
        yield from _rwkv_delta(bufs[i], chunks, state, y_rows, d_rw=d_rw,
                               state_ready=lambda: len(finished) == i)
        _rwkv_finish(bufs[i], y_rows, vec_ref, o_ref, slice(i * sub, (i + 1) * sub))
        finished.append(i)

    _run_interleaved(prepares[0])
    live = [delta_task(0)] + prepares[1:2]
    started = 1
    while live:
        for gen in list(live):
            if next(gen, StopIteration) is StopIteration:
                live.remove(gen)
                if started < n_sub and gen is prepares[started]:
                    live.append(delta_task(started))
                    live.extend(prepares[started + 1:started + 2])
                    started += 1
    for p in range(n_pairs):
        st_ref[p] = state[p]


def _rwkv(z, zw, vecs, w2p, a2p, g2p, d_rw, lora_wa, lora_g):
    bsz, seqlen, _ = z.shape
    lb = RWKV_LB
    const = lambda shape: pl.BlockSpec(shape, lambda b, j: (0,) * len(shape))
    params = (vecs, w2p, a2p, g2p)
    return pl.pallas_call(
        functools.partial(_rwkv_kernel, d_rw=d_rw, lora_wa=lora_wa, lora_g=lora_g),
        grid=(bsz, seqlen // lb),
        in_specs=[pl.BlockSpec((1, lb, zw), lambda b, j: (b, j, 0))]
                 + [const(p.shape) for p in params],
        out_specs=pl.BlockSpec((1, lb, d_rw), lambda b, j: (b, j, 0)),
        out_shape=jax.ShapeDtypeStruct((bsz, seqlen, d_rw), BF16),
        scratch_shapes=[pltpu.VMEM((d_rw // LANES, LANES, LANES), F32)],
        compiler_params=pltpu.CompilerParams(
            dimension_semantics=("arbitrary", "arbitrary"),
            vmem_limit_bytes=VMEM_LIMIT),
        name="rwkv",
    )(z, *params)


def _out_ffn_kernel(x_ref, ys_ref, yr_ref, wo_ref, g_ref, wu_ref, wd_ref, o_ref, *, final_norm):
    x = x_ref[0]
    mix = jnp.concatenate([ys_ref[0], yr_ref[0]], axis=1)
    h = x + jnp.dot(mix, wo_ref[...], preferred_element_type=F32)
    ms = jnp.mean(h * h, axis=-1, keepdims=True)
    hn = (h * lax.rsqrt(ms + NORM_EPS) * g_ref[0:1, :]).astype(BF16)
    d_ff = wu_ref.shape[1]
    acc = h
    for j in range(d_ff // FFN_CHUNK):
        sl = slice(j * FFN_CHUNK, (j + 1) * FFN_CHUNK)
        up = jnp.dot(hn, wu_ref[:, sl], preferred_element_type=F32)
        up = jnp.square(jnp.maximum(up, 0.0)).astype(BF16)
        acc = acc + jnp.dot(up, wd_ref[sl, :], preferred_element_type=F32)
    if final_norm:
        ms2 = jnp.mean(acc * acc, axis=-1, keepdims=True)
        acc = acc * lax.rsqrt(ms2 + NORM_EPS) * g_ref[1:2, :]
    o_ref[0] = acc


def _out_ffn(x, y_s5, y_rw, w_out, gains, w_up, w_down, final_norm):
    bsz, seqlen, d_model = x.shape
    tm = FFN_TM
    const = lambda shape: pl.BlockSpec(shape, lambda b, i: (0,) * len(shape),
                                       pipeline_mode=pl.Buffered(1))
    act = lambda width: pl.BlockSpec((1, tm, width), lambda b, i: (b, i, 0))
    return pl.pallas_call(
        functools.partial(_out_ffn_kernel, final_norm=final_norm),
        grid=(bsz, seqlen // tm),
        in_specs=[
            act(d_model), act(y_s5.shape[2]), act(y_rw.shape[2]),
            const(w_out.shape), const(gains.shape), const(w_up.shape), const(w_down.shape),
        ],
        out_specs=act(d_model),
        out_shape=jax.ShapeDtypeStruct((bsz, seqlen, d_model), x.dtype),
        compiler_params=pltpu.CompilerParams(
            dimension_semantics=("arbitrary", "arbitrary"),
            vmem_limit_bytes=VMEM_LIMIT),
        name="out_ffn",
    )(x, y_s5, y_rw, w_out, gains, w_up, w_down)


def _round_up(n, m):
    return -(-n // m) * m


def _s5_weights(lam_re, lam_im, log_dt, b_re, b_im, c_re, c_im):
    groups, n_p = lam_re.shape
    n_h = b_re.shape[2]
    dt = jnp.exp(log_dt.astype(F32))[:, None]
    lr, li = lam_re.astype(F32), lam_im.astype(F32)
    mag = jnp.exp(lr * dt)
    a_re = mag * jnp.cos(li * dt)
    a_im = mag * jnp.sin(li * dt)
    den = lr * lr + li * li
    q_re = ((a_re - 1.0) * lr + a_im * li) / den
    q_im = (a_im * lr - (a_re - 1.0) * li) / den
    bb_re = q_re[:, :, None] * b_re - q_im[:, :, None] * b_im
    bb_im = q_re[:, :, None] * b_im + q_im[:, :, None] * b_re

    tile_groups = 2 * LANES // n_p
    k_groups = LANES // n_h
    n_tiles = groups // tile_groups
    place = np.zeros((n_tiles, k_groups, tile_groups), np.float32)
    for j in range(n_tiles):
        for g in range(tile_groups):
            place[j, (j * tile_groups + g) % k_groups, g] = 1.0

    def b_tiles(bb):
        t = jnp.einsum('jgph,jlg->jlhgp', bb.reshape(n_tiles, tile_groups, n_p, n_h), place)
        return t.reshape(n_tiles, LANES, 2 * LANES)

    wb = jnp.concatenate([b_tiles(bb_re), b_tiles(bb_im)], axis=0).astype(BF16)

    n_out = groups // k_groups
    eye = np.eye(k_groups, dtype=np.float32)

    def c_tiles(cc):
        t = jnp.einsum('mghp,gk->mgpkh', cc.astype(F32).reshape(n_out, k_groups, n_h, n_p), eye)
        return t.reshape(n_out, k_groups * n_p, LANES).astype(BF16)

    lam_bar = jnp.stack([a_re.reshape(-1), a_im.reshape(-1)])
    return wb, lam_bar, c_tiles(c_re), c_tiles(-c_im)


def kernel(x, norm_mix_g, w_in, lam_re, lam_im, log_dt, b_re, b_im, c_re, c_im, d_skip, w_glu, b_glu, s5_out_g, mu_shift, w0, w2, a0, a2, g2, k_k, k_a, r_k, ln_x_w, ln_x_b, w_out, norm_ffn_g, w_up, w_down, norm_final_g):
    depth, d_model, _ = w_in.shape
    d_s5 = d_skip.shape[1]
    d_rw = w0.shape[1]
    n_w, n_a, n_g = w2.shape[1], a2.shape[1], g2.shape[1]
    lora_wa = _round_up(n_w + n_a, LANES)
    lora_g = _round_up(n_g, LANES)
    d_rw_in = 3 * d_rw + n_w + n_a + n_g
    zw = _round_up(3 * d_rw + lora_wa + lora_g, d_s5)
    assert n_w + n_a == lora_wa and zw >= d_rw_in

    h = x
    for i in range(depth):
        main = d_rw_in // LANES * LANES
        d_in = d_s5 + d_rw_in
        segments = ((d_s5, d_s5 + main, 0, main), (d_s5 + main, d_in, main, zw - main), (0, d_s5, zw, d_s5))
        mu = jnp.pad(mu_shift[i].astype(F32), (0, zw - d_rw_in)).reshape(1, zw)
        w2p = jnp.pad(w2[i], ((0, lora_wa - n_w), (0, 0))).astype(BF16)
        a2p = jnp.pad(a2[i], ((n_w, lora_wa - n_w - n_a), (0, 0))).astype(BF16)
        g2p = jnp.pad(g2[i], ((0, lora_g - n_g), (0, 0))).astype(BF16)
        rw_vecs = jnp.stack([w0[i], a0[i], k_k[i], k_a[i], r_k[i].reshape(-1), ln_x_w[i], ln_x_b[i],
                             jnp.zeros_like(w0[i])]).astype(F32)
        s5_vecs = jnp.stack([d_skip[i], b_glu[i], s5_out_g[i]]).astype(F32)
        gains = jnp.stack([norm_ffn_g[i], norm_final_g]).astype(F32)

        z = _in_proj(h, norm_mix_g[i].reshape(1, -1).astype(F32), mu, w_in[i], segments)

        wb, lam_bar, wc_re, wc_im = _s5_weights(
            lam_re[i], lam_im[i], log_dt[i], b_re[i], b_im[i], c_re[i], c_im[i])
        y_s5 = _s5(z, zw // d_s5, wb, lam_bar, wc_re, wc_im, s5_vecs, w_glu[i].astype(BF16))
        y_rw = _rwkv(z, zw, rw_vecs, w2p, a2p, g2p, d_rw, lora_wa, lora_g)

        h = _out_ffn(h, y_s5, y_rw, w_out[i].astype(BF16), gains,
                     w_up[i].astype(BF16), w_down[i].astype(BF16), final_norm=(i == depth - 1))
    return h
```

```python
import functools
import math

import numpy as np
import jax
import jax.numpy as jnp
from jax import lax
from jax.experimental import pallas as pl
from jax.experimental.pallas import tpu as pltpu

F32 = jnp.float32
BF16 = jnp.bfloat16

NORM_EPS = 1e-6
GN_EPS = 64e-5

S5_GROUP = 16
S5_STATE = 64
RWKV_HEAD = 64
LANES = 128
SUBLANES = 8
VMEM_LIMIT = 56 * 1024 * 1024

IN_TM = 1024
IN_TN = 512
S5_TC = 128
S5_STRIP = 512
RWKV_LB = 512
RWKV_SUB = 256
RWKV_C = 64
FFN_TM = 1024
FFN_CHUNK = 1024

MM = (((1,), (0,)), ((), ()))
NT = (((1,), (1,)), ((), ()))
TN = (((0,), (0,)), ((), ()))


def _dot(a, b, dims=MM):
    return lax.dot_general(a.astype(BF16), b.astype(BF16), dims,
                           preferred_element_type=F32)


def _dot_split_rhs(a_bf16, b, passes):
    acc = None
    rem = b
    for _ in range(passes):
        part = rem.astype(BF16)
        term = jnp.dot(a_bf16, part, preferred_element_type=F32)
        acc = term if acc is None else acc + term
        rem = rem - part.astype(F32)
    return acc


def _in_proj_kernel(x_ref, g_ref, mu_ref, w_ref, z_ref, carry_ref, *, segments):
    n_shift = mu_ref.shape[1]
    x = x_ref[0]
    tm = x.shape[0]

    @pl.when(pl.program_id(1) == 0)
    def _():
        carry_ref[...] = jnp.zeros_like(carry_ref)

    ms = jnp.mean(x * x, axis=-1, keepdims=True)
    xn = (x * lax.rsqrt(ms + NORM_EPS) * g_ref[...]).astype(BF16)
    for src0, src1, dst0, width in segments:
        for j0 in range(0, width, IN_TN):
            j1 = min(j0 + IN_TN, width)
            s0, s1 = src0 + j0, min(src0 + j1, src1)
            res = lax.dot_general(xn, w_ref[s0:s1, :], NT, preferred_element_type=F32)
            if s1 - s0 < j1 - j0:
                res = jnp.concatenate([res, jnp.zeros((tm, j1 - j0 - (s1 - s0)), F32)], axis=1)
            cols = slice(dst0 + j0, dst0 + j1)
            if dst0 + j1 <= n_shift:
                first = lax.broadcasted_iota(jnp.int32, res.shape, 0) == 0
                prev = jnp.where(first, carry_ref[0:1, cols], pltpu.roll(res, 1, 0))
                carry_ref[0:1, cols] = res[tm - 1:tm, :]
                res = res + (prev - res) * mu_ref[:, cols]
            else:
                assert dst0 + j0 >= n_shift
            z_ref[0, :, cols] = res.astype(BF16)


def _in_proj(x, g, mu, w, segments):
    bsz, seqlen, d_model = x.shape
    tm = IN_TM
    width = max(dst0 + n for _, _, dst0, n in segments)
    const = lambda shape: pl.BlockSpec(shape, lambda b, i: (0,) * len(shape))
    return pl.pallas_call(
        functools.partial(_in_proj_kernel, segments=segments),
        grid=(bsz, seqlen // tm),
        in_specs=[pl.BlockSpec((1, tm, d_model), lambda b, i: (b, i, 0)),
                  const(g.shape), const(mu.shape), const(w.shape)],
        out_specs=pl.BlockSpec((1, tm, width), lambda b, i: (b, i, 0)),
        out_shape=jax.ShapeDtypeStruct((bsz, seqlen, width), BF16),
        scratch_shapes=[pltpu.VMEM((SUBLANES, mu.shape[1]), F32)],
        compiler_params=pltpu.CompilerParams(
            dimension_semantics=("arbitrary", "arbitrary"),
            vmem_limit_bytes=VMEM_LIMIT),
        name="in_proj",
    )(x, g, mu, w)


def _s5_kernel(u_ref, wb_ref, lam_ref, wcre_ref, wcim_ref, vec_ref, wglu_ref, o_ref,
               st_ref, carry_ref, io_ref):
    bsz, tc, d_s5 = u_ref.shape
    n_state = lam_ref.shape[1]
    n_slabs_half = n_state // LANES
    n_io = d_s5 // LANES
    d_skip, b_glu, g_out = vec_ref[0:1, :], vec_ref[1:2, :], vec_ref[2:3, :]

    @pl.when(pl.program_id(0) == 0)
    def _():
        carry_ref[...] = jnp.zeros_like(carry_ref)

    for b in range(bsz):
        u_b = u_ref[b].astype(F32)
        for q in range(n_io):
            io_ref[q, pl.ds(b, tc, stride=bsz), :] = u_b[:, q * LANES:(q + 1) * LANES]
    u_f32 = jnp.concatenate([io_ref[q] for q in range(n_io)], axis=1)
    u_bf = u_f32.astype(BF16)

    n_tiles = wb_ref.shape[0]
    tile_w = wb_ref.shape[2]
    tiles_per_half = n_tiles // 2
    slabs_per_tile = tile_w // LANES
    slabs_per_strip = S5_STRIP // LANES
    tiles_per_strip = slabs_per_strip // slabs_per_tile
    n_strips = n_slabs_half // slabs_per_strip
    assert wcre_ref.shape[0] == n_strips and wcre_ref.shape[1] == S5_STRIP

    def b_tile(j):
        kb = (j % tiles_per_half) * tile_w // (S5_STATE // S5_GROUP * LANES)
        res = jnp.dot(u_bf[:, kb * LANES:(kb + 1) * LANES], wb_ref[j],
                      preferred_element_type=F32)
        for h in range(slabs_per_tile):
            st_ref[j * slabs_per_tile + h] = res[:, h * LANES:(h + 1) * LANES]

    def strip_tiles(s):
        re_tiles = [s * tiles_per_strip + q for q in range(tiles_per_strip)]
        return re_tiles + [tiles_per_half + j for j in re_tiles]

    ys = [None] * n_strips

    def c_part(m, part):
        base = m * slabs_per_strip + (n_slabs_half if part else 0)
        xs = jnp.concatenate([st_ref[base + q] for q in range(slabs_per_strip)], axis=1).astype(BF16)
        res = jnp.dot(xs, (wcim_ref if part else wcre_ref)[m], preferred_element_type=F32)
        ys[m] = res if ys[m] is None else ys[m] + res

    def scan(s, n_yields):
        re_ids = [s * slabs_per_strip + q for q in range(slabs_per_strip)]
        im_ids = [n_slabs_half + i for i in re_ids]
        a_re = [jnp.broadcast_to(lam_ref[0:1, i * LANES:(i + 1) * LANES], (bsz, LANES)) for i in re_ids]
        a_im = [jnp.broadcast_to(lam_ref[1:2, i * LANES:(i + 1) * LANES], (bsz, LANES)) for i in re_ids]
        s_re = [carry_ref[i] for i in re_ids]
        s_im = [carry_ref[i] for i in im_ids]
        every = max(tc // max(n_yields, 1), 1)
        for t in range(tc):
            rows = slice(t * bsz, (t + 1) * bsz)
            n_re = [a_re[q] * s_re[q] - a_im[q] * s_im[q] + st_ref[re_ids[q], rows, :]
                    for q in range(slabs_per_strip)]
            n_im = [a_re[q] * s_im[q] + a_im[q] * s_re[q] + st_ref[im_ids[q], rows, :]
                    for q in range(slabs_per_strip)]
            for q in range(slabs_per_strip):
                st_ref[re_ids[q], rows, :] = n_re[q]
                st_ref[im_ids[q], rows, :] = n_im[q]
            s_re, s_im = n_re, n_im
            if t % every == every - 1:
                yield
        for q in range(slabs_per_strip):
            carry_ref[re_ids[q]] = s_re[q]
            carry_ref[im_ids[q]] = s_im[q]

    def mxu_pieces(fns):
        for fn in fns:
            fn()
            yield

    for j in strip_tiles(0):
        b_tile(j)
    for s in range(n_strips):
        fns = []
        if s + 1 < n_strips:
            fns += [functools.partial(b_tile, j) for j in strip_tiles(s + 1)]
        if s >= 1:
            fns += [functools.partial(c_part, s - 1, part) for part in (0, 1)]
        _run_interleaved(scan(s, len(fns)), mxu_pieces(fns))
    for part in (0, 1):
        c_part(n_strips - 1, part)

    y = jnp.concatenate(ys, axis=-1) + d_skip * u_f32
    g = 0.5 * y * (1.0 + lax.erf(y * (1.0 / math.sqrt(2.0))))
    gate = jax.nn.sigmoid(
        jnp.dot(g.astype(BF16), wglu_ref[...], preferred_element_type=F32) + b_glu)
    out = g * gate
    ms = jnp.mean(out * out, axis=-1, keepdims=True)
    out = out * lax.rsqrt(ms + NORM_EPS) * g_out
    for q in range(n_io):
        io_ref[q] = out[:, q * LANES:(q + 1) * LANES]
    for b in range(bsz):
        o_ref[b] = jnp.concatenate(
            [io_ref[q, pl.ds(b, tc, stride=bsz), :] for q in range(n_io)], axis=1).astype(o_ref.dtype)


def _s5(z, col_block, wb, lam_bar, wc_re, wc_im, vecs, w_glu):
    bsz, seqlen, _ = z.shape
    d_s5 = vecs.shape[1]
    tc = S5_TC
    n_state = lam_bar.shape[1]
    const = lambda shape: pl.BlockSpec(shape, lambda i: (0,) * len(shape))
    return pl.pallas_call(
        _s5_kernel,
        grid=(seqlen // tc,),
        in_specs=[
            pl.BlockSpec((bsz, tc, d_s5), lambda i: (0, i, col_block)),
            const(wb.shape), const(lam_bar.shape), const(wc_re.shape), const(wc_im.shape),
            const(vecs.shape), const(w_glu.shape),
        ],
        out_specs=pl.BlockSpec((bsz, tc, d_s5), lambda i: (0, i, 0)),
        out_shape=jax.ShapeDtypeStruct((bsz, seqlen, d_s5), BF16),
        scratch_shapes=[
            pltpu.VMEM((2 * n_state // LANES, bsz * tc, LANES), F32),
            pltpu.VMEM((2 * n_state // LANES, bsz, LANES), F32),
            pltpu.VMEM((d_s5 // LANES, bsz * tc, LANES), F32),
        ],
        compiler_params=pltpu.CompilerParams(
            dimension_semantics=("arbitrary",),
            vmem_limit_bytes=VMEM_LIMIT),
        name="s5",
    )(z, wb, lam_bar, wc_re, wc_im, vecs, w_glu)


_W0, _A0, _KK, _KA, _RK, _LNW, _LNB = range(7)


def _seg_sum(x, ones_bd):
    w = ones_bd.shape[0]
    outs = [_dot(x[:, i * w:(i + 1) * w], ones_bd) for i in range(x.shape[1] // w)]
    return jnp.concatenate(outs, axis=-1)


def _head_ones(hd):
    seg_w = 2 * LANES
    li = lax.broadcasted_iota(jnp.int32, (seg_w, seg_w), 0) // hd
    lj = lax.broadcasted_iota(jnp.int32, (seg_w, seg_w), 1) // hd
    return jnp.where(li == lj, 1.0, 0.0).astype(BF16)


def _rwkv_prepare(zz, buf, vec_ref, w2_ref, a2_ref, g2_ref, *, d_rw, lora_wa, lora_g):
    lb = zz.shape[0]
    c = RWKV_C
    hd = RWKV_HEAD
    n_pairs = d_rw // LANES
    vec = lambda i: vec_ref[i:i + 1, :]

    o = 3 * d_rw
    r = zz[:, 0:d_rw]
    k = zz[:, d_rw:2 * d_rw]
    v = zz[:, 2 * d_rw:o]
    zwa = zz[:, o:o + lora_wa]
    zg = zz[:, o + lora_wa:o + lora_wa + lora_g]

    wpre = vec(_W0) + _dot(jnp.tanh(zwa), w2_ref[...])
    logd = -math.exp(-0.5) * jax.nn.sigmoid(wpre)
    a = jax.nn.sigmoid(vec(_A0) + _dot(zwa, a2_ref[...]))
    yield
    for name in ("lhs1", "rhs1", "v_bd", "kk_bd", "kb_end", "b_end", "q_t", "d_end"):
        buf[name] = {}
    buf["gate"] = _dot(jax.nn.sigmoid(zg), g2_ref[...])
    yield

    ones_bd = _head_ones(hd)
    kk = k * vec(_KK)
    kk = kk * lax.rsqrt(jnp.maximum(_seg_sum(kk * kk, ones_bd), 1e-24))
    yield
    k2 = k * (1.0 + (a - 1.0) * vec(_KA))
    buf["bonus"] = _seg_sum(r * k2 * vec(_RK), ones_bd) * v
    bvec = kk * a
    yield

    ci = lax.broadcasted_iota(jnp.int32, (c, c), 0)
    cj = lax.broadcasted_iota(jnp.int32, (c, c), 1)
    ltri = jnp.where(cj <= ci, 1.0, 0.0).astype(BF16)
    head0 = lax.broadcasted_iota(jnp.int32, (c, LANES), 1) < hd

    def stack_heads(x):
        return jnp.concatenate([jnp.where(head0, x, 0.0), jnp.where(head0, 0.0, x)], axis=0)

    for ic in range(lb // c):
        rows = slice(ic * c, (ic + 1) * c)
        logd_c = logd[rows]
        cum = _dot_split_rhs(ltri, logd_c, 2)
        cum_end = cum[c - 1:c, :]
        e_inv = jnp.exp(-cum)
        e_end = jnp.exp(cum_end - cum)
        q_c = r[rows] * jnp.exp(cum)
        kk_c = kk[rows] * jnp.exp(cum - logd_c)
        k_hat = k2[rows] * e_inv
        b_hat = bvec[rows] * e_inv
        k_end = k2[rows] * e_end
        b_end = bvec[rows] * e_end
        d_c = jnp.exp(cum_end)
        for p in range(n_pairs):
            ls = slice(p * LANES, (p + 1) * LANES)
            key = (ic, p)
            b_end_bd = stack_heads(b_end[:, ls])
            buf["q_t"][key] = q_c[:, ls]
            buf["d_end"][key] = d_c[:, ls]
            buf["kk_bd"][key] = stack_heads(kk_c[:, ls]).astype(BF16)
            buf["v_bd"][key] = stack_heads(v[rows, ls]).astype(BF16)
            buf["b_end"][key] = b_end_bd.astype(BF16)
            buf["lhs1"][key] = jnp.concatenate([q_c[:, ls], kk_c[:, ls]], axis=0).astype(BF16)
            buf["rhs1"][key] = jnp.concatenate(
                [stack_heads(k_hat[:, ls]), stack_heads(b_hat[:, ls])], axis=0).astype(BF16)
            buf["kb_end"][key] = jnp.concatenate(
                [stack_heads(k_end[:, ls]), b_end_bd], axis=0).astype(BF16)
            if p % 2:
                yield


def _rwkv_delta(buf, chunk_ids, state, y_rows, *, d_rw, state_ready):
    c = RWKV_C
    hd = RWKV_HEAD
    n_pairs = d_rw // LANES
    chains = [(ic, p) for ic in chunk_ids for p in range(n_pairs)]

    ri = lax.broadcasted_iota(jnp.int32, (c, LANES), 0)
    rl = lax.broadcasted_iota(jnp.int32, (c, LANES), 1)
    head0 = rl < hd
    incl = (rl % hd) <= ri
    strict = (rl % hd) < ri
    pi = lax.broadcasted_iota(jnp.int32, (2 * c, 2 * c), 0)
    pj = lax.broadcasted_iota(jnp.int32, (2 * c, 2 * c), 1)
    level = pi ^ pj
    diag = pi == pj
    eye = jnp.where(diag, 1.0, 0.0)
    n_levels = int(math.log2(c))

    def stack_heads(x):
        return jnp.concatenate([jnp.where(head0, x, 0.0), jnp.where(head0, 0.0, x)], axis=0)

    p1 = {k_: lax.dot_general(buf["lhs1"][k_], buf["rhs1"][k_], NT,
                              preferred_element_type=F32) for k_ in chains}
    a_rk = {k_: jnp.where(incl, p1[k_][0:c, 0:LANES], 0.0).astype(BF16) for k_ in chains}
    a_rb = {k_: jnp.where(incl, p1[k_][0:c, LANES:2 * LANES], 0.0).astype(BF16) for k_ in chains}
    a_kk = {k_: stack_heads(jnp.where(strict, p1[k_][c:2 * c, 0:LANES], 0.0)).astype(BF16)
            for k_ in chains}
    n_bd = {k_: stack_heads(jnp.where(strict, p1[k_][c:2 * c, LANES:2 * LANES], 0.0)) for k_ in chains}

    t_inv = {k_: eye - jnp.where(level == 1, n_bd[k_], 0.0) for k_ in chains}
    yield
    for lg in range(1, n_levels):
        m = 1 << lg
        starts = list(range(m, 2 * c, 2 * m))
        take = (lambda x: x) if m < SUBLANES else (
            lambda x: jnp.concatenate([x[s:s + m] for s in starts], axis=0))
        t_bf = {k_: t_inv[k_].astype(BF16) for k_ in chains}
        nt = {k_: _dot(take(jnp.where((level >> lg) == 1, n_bd[k_], 0.0)), t_bf[k_]) for k_ in chains}
        yield
        if m < SUBLANES:
            t_inv = {k_: t_inv[k_] - _dot(t_bf[k_], nt[k_]) for k_ in chains}
        else:
            zero = jnp.zeros((m, LANES), F32)
            for k_ in chains:
                nt_full = jnp.concatenate(
                    [piece for i in range(len(starts)) for piece in (zero, nt[k_][i * m:(i + 1) * m])], axis=0)
                t_r = take(t_inv[k_])
                t_r = t_r - _dot(t_r, nt_full)
                t_inv[k_] = jnp.concatenate(
                    [piece for i, s in enumerate(starts)
                     for piece in (t_inv[k_][s - m:s], t_r[i * m:(i + 1) * m])], axis=0)
        yield
    t_bf = {k_: t_inv[k_].astype(BF16) for k_ in chains}

    av = {k_: _dot(a_kk[k_], buf["v_bd"][k_]).astype(BF16) for k_ in chains}
    yield
    w12 = {k_: _dot(t_bf[k_], jnp.concatenate([buf["kk_bd"][k_], av[k_]], axis=1))
           for k_ in chains}
    w1 = {k_: w12[k_][:, 0:LANES].astype(BF16) for k_ in chains}
    vw = {k_: jnp.concatenate([buf["v_bd"][k_], (-w12[k_][:, LANES:2 * LANES]).astype(BF16)], axis=0)
          for k_ in chains}
    yield

    y_intra = {k_: _dot(jnp.concatenate([a_rk[k_], a_rb[k_]], axis=1), vw[k_]) for k_ in chains}
    yield
    q_eff = {k_: (buf["q_t"][k_] - _dot(a_rb[k_], w1[k_])).astype(BF16) for k_ in chains}
    yield
    g_mat = {k_: _dot(buf["kb_end"][k_], vw[k_], TN) for k_ in chains}
    yield
    m_mat = {k_: (jnp.where(diag, jnp.broadcast_to(buf["d_end"][k_], (2 * c, LANES)), 0.0)
                  - _dot(buf["b_end"][k_], w1[k_], TN)).astype(BF16) for k_ in chains}
    yield

    assert state_ready()
    for ic in chunk_ids:
        y_c = [_dot(q_eff[(ic, p)], state[p]) + y_intra[(ic, p)] for p in range(n_pairs)]
        state[:] = [_dot(m_mat[(ic, p)], state[p]) + g_mat[(ic, p)] for p in range(n_pairs)]
        y_rows.append(jnp.concatenate(y_c, axis=1))
        yield


def _rwkv_finish(buf, y_rows, vec_ref, o_ref, rows):
    hd = RWKV_HEAD
    y = jnp.concatenate(y_rows, axis=0)
    ones_bd = _head_ones(hd)
    mean = _seg_sum(y, ones_bd) * (1.0 / hd)
    yc = y - mean
    var = _seg_sum(yc * yc, ones_bd) * (1.0 / hd)
    yn = yc * lax.rsqrt(var + GN_EPS) * vec_ref[_LNW:_LNW + 1, :] + vec_ref[_LNB:_LNB + 1, :]
    o_ref[0, rows, :] = ((yn + buf["bonus"]) * buf["gate"]).astype(o_ref.dtype)


def _run_interleaved(*gens):
    live = list(gens)
    while live:
        for gen in list(live):
            if next(gen, StopIteration) is StopIteration:
                live.remove(gen)


def _rwkv_kernel(z_ref, vec_ref, w2_ref, a2_ref, g2_ref, o_ref, st_ref, *, d_rw, lora_wa, lora_g):
    @pl.when(pl.program_id(1) == 0)
    def _():
        st_ref[...] = jnp.zeros_like(st_ref)

    n_pairs = d_rw // LANES
    sub = RWKV_SUB
    n_sub = z_ref.shape[1] // sub
    chunks = list(range(sub // RWKV_C))
    state = [st_ref[p] for p in range(n_pairs)]

    bufs, prepares = [], []
    for i in range(n_sub):
        zz = z_ref[0, i * sub:(i + 1) * sub, :].astype(F32)
        bufs.append({})
        prepares.append(_rwkv_prepare(zz, bufs[i], vec_ref, w2_ref, a2_ref, g2_ref,
                                      d_rw=d_rw, lora_wa=lora_wa, lora_g=lora_g))

    finished = []

    def delta_task(i):
        y_rows = []
        yield from _rwkv_delta(bufs[i], chunks, state, y_rows, d_rw=d_rw,
                               state_ready=lambda: len(finished) == i)
        _rwkv_finish(bufs[i], y_rows, vec_ref, o_ref, slice(i * sub, (i + 1) * sub))
        finished.append(i)

    _run_interleaved(prepares[0])
    live = [delta_task(0)] + prepares[1:2]
    started = 1
    while live:
        for gen in list(live):
            if next(gen, StopIteration) is StopIteration:
                live.remove(gen)
                if started < n_sub and gen is prepares[started]:
                    live.append(delta_task(started))
                    live.extend(prepares[started + 1:started + 2])
                    started += 1
    for p in range(n_pairs):
        st_ref[p] = state[p]


def _rwkv(z, zw, vecs, w2p, a2p, g2p, d_rw, lora_wa, lora_g):
    bsz, seqlen, _ = z.shape
    lb = RWKV_LB
    const = lambda shape: pl.BlockSpec(shape, lambda b, j: (0,) * len(shape))
    params = (vecs, w2p, a2p, g2p)
    return pl.pallas_call(
        functools.partial(_rwkv_kernel, d_rw=d_rw, lora_wa=lora_wa, lora_g=lora_g),
        grid=(bsz, seqlen // lb),
        in_specs=[pl.BlockSpec((1, lb, zw), lambda b, j: (b, j, 0))]
                 + [const(p.shape) for p in params],
        out_specs=pl.BlockSpec((1, lb, d_rw), lambda b, j: (b, j, 0)),
        out_shape=jax.ShapeDtypeStruct((bsz, seqlen, d_rw), BF16),
        scratch_shapes=[pltpu.VMEM((d_rw // LANES, LANES, LANES), F32)],
        compiler_params=pltpu.CompilerParams(
            dimension_semantics=("arbitrary", "arbitrary"),
            vmem_limit_bytes=VMEM_LIMIT),
        name="rwkv",
    )(z, *params)


def _out_ffn_kernel(x_ref, ys_ref, yr_ref, wo_ref, g_ref, wu_ref, wd_ref, o_ref, *, final_norm):
    x = x_ref[0]
    mix = jnp.concatenate([ys_ref[0], yr_ref[0]], axis=1)
    h = x + jnp.dot(mix, wo_ref[...], preferred_element_type=F32)
    ms = jnp.mean(h * h, axis=-1, keepdims=True)
    hn = (h * lax.rsqrt(ms + NORM_EPS) * g_ref[0:1, :]).astype(BF16)
    d_ff = wu_ref.shape[1]
    acc = h
    for j in range(d_ff // FFN_CHUNK):
        sl = slice(j * FFN_CHUNK, (j + 1) * FFN_CHUNK)
        up = jnp.dot(hn, wu_ref[:, sl], preferred_element_type=F32)
        up = jnp.square(jnp.maximum(up, 0.0)).astype(BF16)
        acc = acc + jnp.dot(up, wd_ref[sl, :], preferred_element_type=F32)
    if final_norm:
        ms2 = jnp.mean(acc * acc, axis=-1, keepdims=True)
        acc = acc * lax.rsqrt(ms2 + NORM_EPS) * g_ref[1:2, :]
    o_ref[0] = acc


def _out_ffn(x, y_s5, y_rw, w_out, gains, w_up, w_down, final_norm):
    bsz, seqlen, d_model = x.shape
    tm = FFN_TM
    const = lambda shape: pl.BlockSpec(shape, lambda b, i: (0,) * len(shape),
                                       pipeline_mode=pl.Buffered(1))
    act = lambda width: pl.BlockSpec((1, tm, width), lambda b, i: (b, i, 0))
    return pl.pallas_call(
        functools.partial(_out_ffn_kernel, final_norm=final_norm),
        grid=(bsz, seqlen // tm),
        in_specs=[
            act(d_model), act(y_s5.shape[2]), act(y_rw.shape[2]),
            const(w_out.shape), const(gains.shape), const(w_up.shape), const(w_down.shape),
        ],
        out_specs=act(d_model),
        out_shape=jax.ShapeDtypeStruct((bsz, seqlen, d_model), x.dtype),
        compiler_params=pltpu.CompilerParams(
            dimension_semantics=("arbitrary", "arbitrary"),
            vmem_limit_bytes=VMEM_LIMIT),
        name="out_ffn",
    )(x, y_s5, y_rw, w_out, gains, w_up, w_down)


def _round_up(n, m):
    return -(-n // m) * m


def _s5_weights(lam_re, lam_im, log_dt, b_re, b_im, c_re, c_im):
    groups, n_p = lam_re.shape
    n_h = b_re.shape[2]
    dt = jnp.exp(log_dt.astype(F32))[:, None]
    lr, li = lam_re.astype(F32), lam_im.astype(F32)
    mag = jnp.exp(lr * dt)
    a_re = mag * jnp.cos(li * dt)
    a_im = mag * jnp.sin(li * dt)
    den = lr * lr + li * li
    q_re = ((a_re - 1.0) * lr + a_im * li) / den
    q_im = (a_im * lr - (a_re - 1.0) * li) / den
    bb_re = q_re[:, :, None] * b_re - q_im[:, :, None] * b_im
    bb_im = q_re[:, :, None] * b_im + q_im[:, :, None] * b_re

    tile_groups = 2 * LANES // n_p
    k_groups = LANES // n_h
    n_tiles = groups // tile_groups
    place = np.zeros((n_tiles, k_groups, tile_groups), np.float32)
    for j in range(n_tiles):
        for g in range(tile_groups):
            place[j, (j * tile_groups + g) % k_groups, g] = 1.0

    def b_tiles(bb):
        t = jnp.einsum('jgph,jlg->jlhgp', bb.reshape(n_tiles, tile_groups, n_p, n_h), place)
        return t.reshape(n_tiles, LANES, 2 * LANES)

    wb = jnp.concatenate([b_tiles(bb_re), b_tiles(bb_im)], axis=0).astype(BF16)

    n_out = groups // k_groups
    eye = np.eye(k_groups, dtype=np.float32)

    def c_tiles(cc):
        t = jnp.einsum('mghp,gk->mgpkh', cc.astype(F32).reshape(n_out, k_groups, n_h, n_p), eye)
        return t.reshape(n_out, k_groups * n_p, LANES).astype(BF16)

    lam_bar = jnp.stack([a_re.reshape(-1), a_im.reshape(-1)])
    return wb, lam_bar, c_tiles(c_re), c_tiles(-c_im)


def kernel(x, norm_mix_g, w_in, lam_re, lam_im, log_dt, b_re, b_im, c_re, c_im, d_skip, w_glu, b_glu, s5_out_g, mu_shift, w0, w2, a0, a2, g2, k_k, k_a, r_k, ln_x_w, ln_x_b, w_out, norm_ffn_g, w_up, w_down, norm_final_g):
    depth, d_model, _ = w_in.shape
    d_s5 = d_skip.shape[1]
    d_rw = w0.shape[1]
    n_w, n_a, n_g = w2.shape[1], a2.shape[1], g2.shape[1]
    lora_wa = _round_up(n_w + n_a, LANES)
    lora_g = _round_up(n_g, LANES)
    d_rw_in = 3 * d_rw + n_w + n_a + n_g
    zw = _round_up(3 * d_rw + lora_wa + lora_g, d_s5)
    assert n_w + n_a == lora_wa and zw >= d_rw_in

    h = x
    for i in range(depth):
        main = d_rw_in // LANES * LANES
        d_in = d_s5 + d_rw_in
        segments = ((d_s5, d_s5 + main, 0, main), (d_s5 + main, d_in, main, zw - main), (0, d_s5, zw, d_s5))
        mu = jnp.pad(mu_shift[i].astype(F32), (0, zw - d_rw_in)).reshape(1, zw)
        w2p = jnp.pad(w2[i], ((0, lora_wa - n_w), (0, 0))).astype(BF16)
        a2p = jnp.pad(a2[i], ((n_w, lora_wa - n_w - n_a), (0, 0))).astype(BF16)
        g2p = jnp.pad(g2[i], ((0, lora_g - n_g), (0, 0))).astype(BF16)
        rw_vecs = jnp.stack([w0[i], a0[i], k_k[i], k_a[i], r_k[i].reshape(-1), ln_x_w[i], ln_x_b[i],
                             jnp.zeros_like(w0[i])]).astype(F32)
        s5_vecs = jnp.stack([d_skip[i], b_glu[i], s5_out_g[i]]).astype(F32)
        gains = jnp.stack([norm_ffn_g[i], norm_final_g]).astype(F32)

        w_t = jnp.swapaxes(w_in[i], 0, 1).astype(BF16)
        z = _in_proj(h, norm_mix_g[i].reshape(1, -1).astype(F32), mu, w_t, segments)

        wb, lam_bar, wc_re, wc_im = _s5_weights(
            lam_re[i], lam_im[i], log_dt[i], b_re[i], b_im[i], c_re[i], c_im[i])
        y_s5 = _s5(z, zw // d_s5, wb, lam_bar, wc_re, wc_im, s5_vecs, w_glu[i].astype(BF16))
        y_rw = _rwkv(z, zw, rw_vecs, w2p, a2p, g2p, d_rw, lora_wa, lora_g)

        h = _out_ffn(h, y_s5, y_rw, w_out[i].astype(BF16), gains,
                     w_up[i].astype(BF16), w_down[i].astype(BF16), final_norm=(i == depth - 1))
    return h
```

```python
import functools
import math

import numpy as np
import jax
import jax.numpy as jnp
from jax import lax
from jax.experimental import pallas as pl
from jax.experimental.pallas import tpu as pltpu

F32 = jnp.float32
BF16 = jnp.bfloat16

NORM_EPS = 1e-6
GN_EPS = 64e-5

S5_GROUP = 16
S5_STATE = 64
RWKV_HEAD = 64
LANES = 128
SUBLANES = 8
VMEM_LIMIT = 56 * 1024 * 1024

IN_TM = 1024
IN_TN = 512
S5_TC = 128
S5_STRIP = 512
RWKV_LB = 512
RWKV_SUB = 256
RWKV_C = 64
FFN_TM = 1024
FFN_CHUNK = 1024

MM = (((1,), (0,)), ((), ()))
NT = (((1,), (1,)), ((), ()))
TN = (((0,), (0,)), ((), ()))


def _dot(a, b, dims=MM):
    return lax.dot_general(a.astype(BF16), b.astype(BF16), dims,
                           preferred_element_type=F32)


def _dot_split_rhs(a_bf16, b, passes):
    acc = None
    rem = b
    for _ in range(passes):
        part = rem.astype(BF16)
        term = jnp.dot(a_bf16, part, preferred_element_type=F32)
        acc = term if acc is None else acc + term
        rem = rem - part.astype(F32)
    return acc


def _in_proj_kernel(x_ref, g_ref, mu_ref, w_ref, z_ref, carry_ref, *, segments):
    n_shift = mu_ref.shape[1]
    x = x_ref[0]
    tm = x.shape[0]

    @pl.when(pl.program_id(1) == 0)
    def _():
        carry_ref[...] = jnp.zeros_like(carry_ref)

    ms = jnp.mean(x * x, axis=-1, keepdims=True)
    xn = (x * lax.rsqrt(ms + NORM_EPS) * g_ref[...]).astype(BF16)
    for src0, src1, dst0, width in segments:
        for j0 in range(0, width, IN_TN):
            j1 = min(j0 + IN_TN, width)
            s0, s1 = src0 + j0, min(src0 + j1, src1)
            res = jnp.dot(xn, w_ref[:, s0:s1], preferred_element_type=F32)
            if s1 - s0 < j1 - j0:
                res = jnp.concatenate([res, jnp.zeros((tm, j1 - j0 - (s1 - s0)), F32)], axis=1)
            cols = slice(dst0 + j0, dst0 + j1)
            if dst0 + j1 <= n_shift:
                first = lax.broadcasted_iota(jnp.int32, res.shape, 0) == 0
                prev = jnp.where(first, carry_ref[0:1, cols], pltpu.roll(res, 1, 0))
                carry_ref[0:1, cols] = res[tm - 1:tm, :]
                res = res + (prev - res) * mu_ref[:, cols]
            else:
                assert dst0 + j0 >= n_shift
            z_ref[0, :, cols] = res.astype(BF16)


def _in_proj(x, g, mu, w, segments):
    bsz, seqlen, d_model = x.shape
    tm = IN_TM
    width = max(dst0 + n for _, _, dst0, n in segments)
    const = lambda shape: pl.BlockSpec(shape, lambda b, i: (0,) * len(shape))
    return pl.pallas_call(
        functools.partial(_in_proj_kernel, segments=segments),
        grid=(bsz, seqlen // tm),
        in_specs=[pl.BlockSpec((1, tm, d_model), lambda b, i: (b, i, 0)),
                  const(g.shape), const(mu.shape), const(w.shape)],
        out_specs=pl.BlockSpec((1, tm, width), lambda b, i: (b, i, 0)),
        out_shape=jax.ShapeDtypeStruct((bsz, seqlen, width), BF16),
        scratch_shapes=[pltpu.VMEM((SUBLANES, mu.shape[1]), F32)],
        compiler_params=pltpu.CompilerParams(
            dimension_semantics=("arbitrary", "arbitrary"),
            vmem_limit_bytes=VMEM_LIMIT),
        name="in_proj",
    )(x, g, mu, w)


def _s5_kernel(u_ref, wb_ref, lam_ref, wcre_ref, wcim_ref, vec_ref, wglu_ref, o_ref,
               st_ref, carry_ref, io_ref):
    bsz, tc, d_s5 = u_ref.shape
    n_state = lam_ref.shape[1]
    n_slabs_half = n_state // LANES
    n_io = d_s5 // LANES
    d_skip, b_glu, g_out = vec_ref[0:1, :], vec_ref[1:2, :], vec_ref[2:3, :]

    @pl.when(pl.program_id(0) == 0)
    def _():
        carry_ref[...] = jnp.zeros_like(carry_ref)

    for b in range(bsz):
        u_b = u_ref[b].astype(F32)
        for q in range(n_io):
            io_ref[q, pl.ds(b, tc, stride=bsz), :] = u_b[:, q * LANES:(q + 1) * LANES]
    u_f32 = jnp.concatenate([io_ref[q] for q in range(n_io)], axis=1)
    u_bf = u_f32.astype(BF16)

    n_tiles = wb_ref.shape[0]
    tile_w = wb_ref.shape[2]
    tiles_per_half = n_tiles // 2
    slabs_per_tile = tile_w // LANES
    slabs_per_strip = S5_STRIP // LANES
    tiles_per_strip = slabs_per_strip // slabs_per_tile
    n_strips = n_slabs_half // slabs_per_strip
    assert wcre_ref.shape[0] == n_strips and wcre_ref.shape[1] == S5_STRIP

    def b_tile(j):
        kb = (j % tiles_per_half) * tile_w // (S5_STATE // S5_GROUP * LANES)
        res = jnp.dot(u_bf[:, kb * LANES:(kb + 1) * LANES], wb_ref[j],
                      preferred_element_type=F32)
        for h in range(slabs_per_tile):
            st_ref[j * slabs_per_tile + h] = res[:, h * LANES:(h + 1) * LANES]

    def strip_tiles(s):
        re_tiles = [s * tiles_per_strip + q for q in range(tiles_per_strip)]
        return re_tiles + [tiles_per_half + j for j in re_tiles]

    ys = [None] * n_strips

    def c_part(m, part):
        base = m * slabs_per_strip + (n_slabs_half if part else 0)
        xs = jnp.concatenate([st_ref[base + q] for q in range(slabs_per_strip)], axis=1).astype(BF16)
        res = jnp.dot(xs, (wcim_ref if part else wcre_ref)[m], preferred_element_type=F32)
        ys[m] = res if ys[m] is None else ys[m] + res

    def scan(s, n_yields):
        re_ids = [s * slabs_per_strip + q for q in range(slabs_per_strip)]
        im_ids = [n_slabs_half + i for i in re_ids]
        a_re = [jnp.broadcast_to(lam_ref[0:1, i * LANES:(i + 1) * LANES], (bsz, LANES)) for i in re_ids]
        a_im = [jnp.broadcast_to(lam_ref[1:2, i * LANES:(i + 1) * LANES], (bsz, LANES)) for i in re_ids]
        s_re = [carry_ref[i] for i in re_ids]
        s_im = [carry_ref[i] for i in im_ids]
        every = max(tc // max(n_yields, 1), 1)
        for t in range(tc):
            rows = slice(t * bsz, (t + 1) * bsz)
            n_re = [a_re[q] * s_re[q] - a_im[q] * s_im[q] + st_ref[re_ids[q], rows, :]
                    for q in range(slabs_per_strip)]
            n_im = [a_re[q] * s_im[q] + a_im[q] * s_re[q] + st_ref[im_ids[q], rows, :]
                    for q in range(slabs_per_strip)]
            for q in range(slabs_per_strip):
                st_ref[re_ids[q], rows, :] = n_re[q]
                st_ref[im_ids[q], rows, :] = n_im[q]
            s_re, s_im = n_re, n_im
            if t % every == every - 1:
                yield
        for q in range(slabs_per_strip):
            carry_ref[re_ids[q]] = s_re[q]
            carry_ref[im_ids[q]] = s_im[q]

    def mxu_pieces(fns):
        for fn in fns:
            fn()
            yield

    for j in strip_tiles(0):
        b_tile(j)
    for s in range(n_strips):
        fns = []
        if s + 1 < n_strips:
            fns += [functools.partial(b_tile, j) for j in strip_tiles(s + 1)]
        if s >= 1:
            fns += [functools.partial(c_part, s - 1, part) for part in (0, 1)]
        _run_interleaved(scan(s, len(fns)), mxu_pieces(fns))
    for part in (0, 1):
        c_part(n_strips - 1, part)

    y = jnp.concatenate(ys, axis=-1) + d_skip * u_f32
    g = 0.5 * y * (1.0 + lax.erf(y * (1.0 / math.sqrt(2.0))))
    gate = jax.nn.sigmoid(
        jnp.dot(g.astype(BF16), wglu_ref[...], preferred_element_type=F32) + b_glu)
    out = g * gate
    ms = jnp.mean(out * out, axis=-1, keepdims=True)
    out = out * lax.rsqrt(ms + NORM_EPS) * g_out
    for q in range(n_io):
        io_ref[q] = out[:, q * LANES:(q + 1) * LANES]
    for b in range(bsz):
        o_ref[b] = jnp.concatenate(
            [io_ref[q, pl.ds(b, tc, stride=bsz), :] for q in range(n_io)], axis=1).astype(o_ref.dtype)


def _s5(z, col_block, wb, lam_bar, wc_re, wc_im, vecs, w_glu):
    bsz, seqlen, _ = z.shape
    d_s5 = vecs.shape[1]
    tc = S5_TC
    n_state = lam_bar.shape[1]
    const = lambda shape: pl.BlockSpec(shape, lambda i: (0,) * len(shape))
    return pl.pallas_call(
        _s5_kernel,
        grid=(seqlen // tc,),
        in_specs=[
            pl.BlockSpec((bsz, tc, d_s5), lambda i: (0, i, col_block)),
            const(wb.shape), const(lam_bar.shape), const(wc_re.shape), const(wc_im.shape),
            const(vecs.shape), const(w_glu.shape),
        ],
        out_specs=pl.BlockSpec((bsz, tc, d_s5), lambda i: (0, i, 0)),
        out_shape=jax.ShapeDtypeStruct((bsz, seqlen, d_s5), BF16),
        scratch_shapes=[
            pltpu.VMEM((2 * n_state // LANES, bsz * tc, LANES), F32),
            pltpu.VMEM((2 * n_state // LANES, bsz, LANES), F32),
            pltpu.VMEM((d_s5 // LANES, bsz * tc, LANES), F32),
        ],
        compiler_params=pltpu.CompilerParams(
            dimension_semantics=("arbitrary",),
            vmem_limit_bytes=VMEM_LIMIT),
        name="s5",
    )(z, wb, lam_bar, wc_re, wc_im, vecs, w_glu)


_W0, _A0, _KK, _KA, _RK, _LNW, _LNB = range(7)


def _seg_sum(x, ones_bd):
    w = ones_bd.shape[0]
    outs = [_dot(x[:, i * w:(i + 1) * w], ones_bd) for i in range(x.shape[1] // w)]
    return jnp.concatenate(outs, axis=-1)


def _head_ones(hd):
    seg_w = 2 * LANES
    li = lax.broadcasted_iota(jnp.int32, (seg_w, seg_w), 0) // hd
    lj = lax.broadcasted_iota(jnp.int32, (seg_w, seg_w), 1) // hd
    return jnp.where(li == lj, 1.0, 0.0).astype(BF16)


def _rwkv_prepare(zz, buf, vec_ref, w2_ref, a2_ref, g2_ref, *, d_rw, lora_wa, lora_g):
    lb = zz.shape[0]
    c = RWKV_C
    hd = RWKV_HEAD
    n_pairs = d_rw // LANES
    vec = lambda i: vec_ref[i:i + 1, :]

    o = 3 * d_rw
    r = zz[:, 0:d_rw]
    k = zz[:, d_rw:2 * d_rw]
    v = zz[:, 2 * d_rw:o]
    zwa = zz[:, o:o + lora_wa]
    zg = zz[:, o + lora_wa:o + lora_wa + lora_g]

    wpre = vec(_W0) + _dot(jnp.tanh(zwa), w2_ref[...])
    logd = -math.exp(-0.5) * jax.nn.sigmoid(wpre)
    a = jax.nn.sigmoid(vec(_A0) + _dot(zwa, a2_ref[...]))
    yield
    for name in ("lhs1", "rhs1", "v_bd", "kk_bd", "kb_end", "b_end", "q_t", "d_end"):
        buf[name] = {}
    buf["gate"] = _dot(jax.nn.sigmoid(zg), g2_ref[...])
    yield

    ones_bd = _head_ones(hd)
    kk = k * vec(_KK)
    kk = kk * lax.rsqrt(jnp.maximum(_seg_sum(kk * kk, ones_bd), 1e-24))
    yield
    k2 = k * (1.0 + (a - 1.0) * vec(_KA))
    buf["bonus"] = _seg_sum(r * k2 * vec(_RK), ones_bd) * v
    bvec = kk * a
    yield

    ci = lax.broadcasted_iota(jnp.int32, (c, c), 0)
    cj = lax.broadcasted_iota(jnp.int32, (c, c), 1)
    ltri = jnp.where(cj <= ci, 1.0, 0.0).astype(BF16)
    head0 = lax.broadcasted_iota(jnp.int32, (c, LANES), 1) < hd

    def stack_heads(x):
        return jnp.concatenate([jnp.where(head0, x, 0.0), jnp.where(head0, 0.0, x)], axis=0)

    for ic in range(lb // c):
        rows = slice(ic * c, (ic + 1) * c)
        logd_c = logd[rows]
        cum = _dot_split_rhs(ltri, logd_c, 2)
        cum_end = cum[c - 1:c, :]
        e_inv = jnp.exp(-cum)
        e_end = jnp.exp(cum_end - cum)
        q_c = r[rows] * jnp.exp(cum)
        kk_c = kk[rows] * jnp.exp(cum - logd_c)
        k_hat = k2[rows] * e_inv
        b_hat = bvec[rows] * e_inv
        k_end = k2[rows] * e_end
        b_end = bvec[rows] * e_end
        d_c = jnp.exp(cum_end)
        for p in range(n_pairs):
            ls = slice(p * LANES, (p + 1) * LANES)
            key = (ic, p)
            b_end_bd = stack_heads(b_end[:, ls])
            buf["q_t"][key] = q_c[:, ls]
            buf["d_end"][key] = d_c[:, ls]
            buf["kk_bd"][key] = stack_heads(kk_c[:, ls]).astype(BF16)
            buf["v_bd"][key] = stack_heads(v[rows, ls]).astype(BF16)
            buf["b_end"][key] = b_end_bd.astype(BF16)
            buf["lhs1"][key] = jnp.concatenate([q_c[:, ls], kk_c[:, ls]], axis=0).astype(BF16)
            buf["rhs1"][key] = jnp.concatenate(
                [stack_heads(k_hat[:, ls]), stack_heads(b_hat[:, ls])], axis=0).astype(BF16)
            buf["kb_end"][key] = jnp.concatenate(
                [stack_heads(k_end[:, ls]), b_end_bd], axis=0).astype(BF16)
            if p % 2:
                yield


def _rwkv_delta(buf, chunk_ids, state, y_rows, *, d_rw, state_ready):
    c = RWKV_C
    hd = RWKV_HEAD
    n_pairs = d_rw // LANES
    chains = [(ic, p) for ic in chunk_ids for p in range(n_pairs)]

    ri = lax.broadcasted_iota(jnp.int32, (c, LANES), 0)
    rl = lax.broadcasted_iota(jnp.int32, (c, LANES), 1)
    head0 = rl < hd
    incl = (rl % hd) <= ri
    strict = (rl % hd) < ri
    pi = lax.broadcasted_iota(jnp.int32, (2 * c, 2 * c), 0)
    pj = lax.broadcasted_iota(jnp.int32, (2 * c, 2 * c), 1)
    level = pi ^ pj
    diag = pi == pj
    eye = jnp.where(diag, 1.0, 0.0)
    n_levels = int(math.log2(c))

    def stack_heads(x):
        return jnp.concatenate([jnp.where(head0, x, 0.0), jnp.where(head0, 0.0, x)], axis=0)

    p1 = {k_: lax.dot_general(buf["lhs1"][k_], buf["rhs1"][k_], NT,
                              preferred_element_type=F32) for k_ in chains}
    a_rk = {k_: jnp.where(incl, p1[k_][0:c, 0:LANES], 0.0).astype(BF16) for k_ in chains}
    a_rb = {k_: jnp.where(incl, p1[k_][0:c, LANES:2 * LANES], 0.0).astype(BF16) for k_ in chains}
    a_kk = {k_: stack_heads(jnp.where(strict, p1[k_][c:2 * c, 0:LANES], 0.0)).astype(BF16)
            for k_ in chains}
    n_bd = {k_: stack_heads(jnp.where(strict, p1[k_][c:2 * c, LANES:2 * LANES], 0.0)) for k_ in chains}

    t_inv = {k_: eye - jnp.where(level == 1, n_bd[k_], 0.0) for k_ in chains}
    yield
    for lg in range(1, n_levels):
        m = 1 << lg
        starts = list(range(m, 2 * c, 2 * m))
        take = (lambda x: x) if m < SUBLANES else (
            lambda x: jnp.concatenate([x[s:s + m] for s in starts], axis=0))
        t_bf = {k_: t_inv[k_].astype(BF16) for k_ in chains}
        nt = {k_: _dot(take(jnp.where((level >> lg) == 1, n_bd[k_], 0.0)), t_bf[k_]) for k_ in chains}
        yield
        if m < SUBLANES:
            t_inv = {k_: t_inv[k_] - _dot(t_bf[k_], nt[k_]) for k_ in chains}
        else:
            zero = jnp.zeros((m, LANES), F32)
            for k_ in chains:
                nt_full = jnp.concatenate(
                    [piece for i in range(len(starts)) for piece in (zero, nt[k_][i * m:(i + 1) * m])], axis=0)
                t_r = take(t_inv[k_])
                t_r = t_r - _dot(t_r, nt_full)
                t_inv[k_] = jnp.concatenate(
                    [piece for i, s in enumerate(starts)
                     for piece in (t_inv[k_][s - m:s], t_r[i * m:(i + 1) * m])], axis=0)
        yield
    t_bf = {k_: t_inv[k_].astype(BF16) for k_ in chains}

    av = {k_: _dot(a_kk[k_], buf["v_bd"][k_]).astype(BF16) for k_ in chains}
    yield
    w12 = {k_: _dot(t_bf[k_], jnp.concatenate([buf["kk_bd"][k_], av[k_]], axis=1))
           for k_ in chains}
    w1 = {k_: w12[k_][:, 0:LANES].astype(BF16) for k_ in chains}
    vw = {k_: jnp.concatenate([buf["v_bd"][k_], (-w12[k_][:, LANES:2 * LANES]).astype(BF16)], axis=0)
          for k_ in chains}
    yield

    y_intra = {k_: _dot(jnp.concatenate([a_rk[k_], a_rb[k_]], axis=1), vw[k_]) for k_ in chains}
    yield
    q_eff = {k_: (buf["q_t"][k_] - _dot(a_rb[k_], w1[k_])).astype(BF16) for k_ in chains}
    yield
    g_mat = {k_: _dot(buf["kb_end"][k_], vw[k_], TN) for k_ in chains}
    yield
    m_mat = {k_: (jnp.where(diag, jnp.broadcast_to(buf["d_end"][k_], (2 * c, LANES)), 0.0)
                  - _dot(buf["b_end"][k_], w1[k_], TN)).astype(BF16) for k_ in chains}
    yield

    assert state_ready()
    for ic in chunk_ids:
        y_c = [_dot(q_eff[(ic, p)], state[p]) + y_intra[(ic, p)] for p in range(n_pairs)]
        state[:] = [_dot(m_mat[(ic, p)], state[p]) + g_mat[(ic, p)] for p in range(n_pairs)]
        y_rows.append(jnp.concatenate(y_c, axis=1))
        yield


def _rwkv_finish(buf, y_rows, vec_ref, o_ref, rows):
    hd = RWKV_HEAD
    y = jnp.concatenate(y_rows, axis=0)
    ones_bd = _head_ones(hd)
    mean = _seg_sum(y, ones_bd) * (1.0 / hd)
    yc = y - mean
    var = _seg_sum(yc * yc, ones_bd) * (1.0 / hd)
    yn = yc * lax.rsqrt(var + GN_EPS) * vec_ref[_LNW:_LNW + 1, :] + vec_ref[_LNB:_LNB + 1, :]
    o_ref[0, rows, :] = ((yn + buf["bonus"]) * buf["gate"]).astype(o_ref.dtype)


def _run_interleaved(*gens):
    live = list(gens)
    while live:
        for gen in list(live):
            if next(gen, StopIteration) is StopIteration:
                live.remove(gen)


def _rwkv_kernel(z_ref, vec_ref, w2_ref, a2_ref, g2_ref, o_ref, st_ref, *, d_rw, lora_wa, lora_g):
    @pl.when(pl.program_id(1) == 0)
    def _():
        st_ref[...] = jnp.zeros_like(st_ref)

    n_pairs = d_rw // LANES
    sub = RWKV_SUB
    n_sub = z_ref.shape[1] // sub
    chunks = list(range(sub // RWKV_C))
    state = [st_ref[p] for p in range(n_pairs)]

    bufs, prepares = [], []
    for i in range(n_sub):
        zz = z_ref[0, i * sub:(i + 1) * sub, :].astype(F32)
        bufs.append({})
        prepares.append(_rwkv_prepare(zz, bufs[i], vec_ref, w2_ref, a2_ref, g2_ref,
                                      d_rw=d_rw, lora_wa=lora_wa, lora_g=lora_g))

    finished = []

    def delta_task(i):
        y_rows = []
        yield from _rwkv_delta(bufs[i], chunks, state, y_rows, d_rw=d_rw,
                               state_ready=lambda: len(finished) == i)
        _rwkv_finish(bufs[i], y_rows, vec_ref, o_ref, slice(i * sub, (i + 1) * sub))
        finished.append(i)

    _run_interleaved(prepares[0])
    live = [delta_task(0)] + prepares[1:2]
    started = 1
    while live:
        for gen in list(live):
            if next(gen, StopIteration) is StopIteration:
                live.remove(gen)
                if started < n_sub and gen is prepares[started]:
                    live.append(delta_task(started))
                    live.extend(prepares[started + 1:started + 2])
                    started += 1
    for p in range(n_pairs):
        st_ref[p] = state[p]


def _rwkv(z, zw, vecs, w2p, a2p, g2p, d_rw, lora_wa, lora_g):
    bsz, seqlen, _ = z.shape
    lb = RWKV_LB
    const = lambda shape: pl.BlockSpec(shape, lambda b, j: (0,) * len(shape))
    params = (vecs, w2p, a2p, g2p)
    return pl.pallas_call(
        functools.partial(_rwkv_kernel, d_rw=d_rw, lora_wa=lora_wa, lora_g=lora_g),
        grid=(bsz, seqlen // lb),
        in_specs=[pl.BlockSpec((1, lb, zw), lambda b, j: (b, j, 0))]
                 + [const(p.shape) for p in params],
        out_specs=pl.BlockSpec((1, lb, d_rw), lambda b, j: (b, j, 0)),
        out_shape=jax.ShapeDtypeStruct((bsz, seqlen, d_rw), BF16),
        scratch_shapes=[pltpu.VMEM((d_rw // LANES, LANES, LANES), F32)],
        compiler_params=pltpu.CompilerParams(
            dimension_semantics=("arbitrary", "arbitrary"),
            vmem_limit_bytes=VMEM_LIMIT),
        name="rwkv",
    )(z, *params)


def _out_ffn_kernel(x_ref, ys_ref, yr_ref, wo_ref, g_ref, wu_ref, wd_ref, o_ref, *, final_norm):
    x = x_ref[0]
    mix = jnp.concatenate([ys_ref[0], yr_ref[0]], axis=1)
    h = x + jnp.dot(mix, wo_ref[...], preferred_element_type=F32)
    ms = jnp.mean(h * h, axis=-1, keepdims=True)
    hn = (h * lax.rsqrt(ms + NORM_EPS) * g_ref[0:1, :]).astype(BF16)
    d_ff = wu_ref.shape[1]
    acc = h
    for j in range(d_ff // FFN_CHUNK):
        sl = slice(j * FFN_CHUNK, (j + 1) * FFN_CHUNK)
        up = jnp.dot(hn, wu_ref[:, sl], preferred_element_type=F32)
        up = jnp.square(jnp.maximum(up, 0.0)).astype(BF16)
        acc = acc + jnp.dot(up, wd_ref[sl, :], preferred_element_type=F32)
    if final_norm:
        ms2 = jnp.mean(acc * acc, axis=-1, keepdims=True)
        acc = acc * lax.rsqrt(ms2 + NORM_EPS) * g_ref[1:2, :]
    o_ref[0] = acc


def _out_ffn(x, y_s5, y_rw, w_out, gains, w_up, w_down, final_norm):
    bsz, seqlen, d_model = x.shape
    tm = FFN_TM
    const = lambda shape: pl.BlockSpec(shape, lambda b, i: (0,) * len(shape),
                                       pipeline_mode=pl.Buffered(1))
    act = lambda width: pl.BlockSpec((1, tm, width), lambda b, i: (b, i, 0))
    return pl.pallas_call(
        functools.partial(_out_ffn_kernel, final_norm=final_norm),
        grid=(bsz, seqlen // tm),
        in_specs=[
            act(d_model), act(y_s5.shape[2]), act(y_rw.shape[2]),
            const(w_out.shape), const(gains.shape), const(w_up.shape), const(w_down.shape),
        ],
        out_specs=act(d_model),
        out_shape=jax.ShapeDtypeStruct((bsz, seqlen, d_model), x.dtype),
        compiler_params=pltpu.CompilerParams(
            dimension_semantics=("arbitrary", "arbitrary"),
            vmem_limit_bytes=VMEM_LIMIT),
        name="out_ffn",
    )(x, y_s5, y_rw, w_out, gains, w_up, w_down)


def _round_up(n, m):
    return -(-n // m) * m


def _s5_weights(lam_re, lam_im, log_dt, b_re, b_im, c_re, c_im):
    groups, n_p = lam_re.shape
    n_h = b_re.shape[2]
    dt = jnp.exp(log_dt.astype(F32))[:, None]
    lr, li = lam_re.astype(F32), lam_im.astype(F32)
    mag = jnp.exp(lr * dt)
    a_re = mag * jnp.cos(li * dt)
    a_im = mag * jnp.sin(li * dt)
    den = lr * lr + li * li
    q_re = ((a_re - 1.0) * lr + a_im * li) / den
    q_im = (a_im * lr - (a_re - 1.0) * li) / den
    bb_re = q_re[:, :, None] * b_re - q_im[:, :, None] * b_im
    bb_im = q_re[:, :, None] * b_im + q_im[:, :, None] * b_re

    tile_groups = 2 * LANES // n_p
    k_groups = LANES // n_h
    n_tiles = groups // tile_groups
    place = np.zeros((n_tiles, k_groups, tile_groups), np.float32)
    for j in range(n_tiles):
        for g in range(tile_groups):
            place[j, (j * tile_groups + g) % k_groups, g] = 1.0

    def b_tiles(bb):
        t = jnp.einsum('jgph,jlg->jlhgp', bb.reshape(n_tiles, tile_groups, n_p, n_h), place)
        return t.reshape(n_tiles, LANES, 2 * LANES)

    wb = jnp.concatenate([b_tiles(bb_re), b_tiles(bb_im)], axis=0).astype(BF16)

    n_out = groups // k_groups
    eye = np.eye(k_groups, dtype=np.float32)

    def c_tiles(cc):
        t = jnp.einsum('mghp,gk->mgpkh', cc.astype(F32).reshape(n_out, k_groups, n_h, n_p), eye)
        return t.reshape(n_out, k_groups * n_p, LANES).astype(BF16)

    lam_bar = jnp.stack([a_re.reshape(-1), a_im.reshape(-1)])
    return wb, lam_bar, c_tiles(c_re), c_tiles(-c_im)


def kernel(x, norm_mix_g, w_in, lam_re, lam_im, log_dt, b_re, b_im, c_re, c_im, d_skip, w_glu, b_glu, s5_out_g, mu_shift, w0, w2, a0, a2, g2, k_k, k_a, r_k, ln_x_w, ln_x_b, w_out, norm_ffn_g, w_up, w_down, norm_final_g):
    depth, d_model, _ = w_in.shape
    d_s5 = d_skip.shape[1]
    d_rw = w0.shape[1]
    n_w, n_a, n_g = w2.shape[1], a2.shape[1], g2.shape[1]
    lora_wa = _round_up(n_w + n_a, LANES)
    lora_g = _round_up(n_g, LANES)
    d_rw_in = 3 * d_rw + n_w + n_a + n_g
    zw = _round_up(3 * d_rw + lora_wa + lora_g, d_s5)
    assert n_w + n_a == lora_wa and zw >= d_rw_in

    h = x
    for i in range(depth):
        main = d_rw_in // LANES * LANES
        d_in = d_s5 + d_rw_in
        segments = ((d_s5, d_s5 + main, 0, main), (d_s5 + main, d_in, main, zw - main), (0, d_s5, zw, d_s5))
        mu = jnp.pad(mu_shift[i].astype(F32), (0, zw - d_rw_in)).reshape(1, zw)
        w2p = jnp.pad(w2[i], ((0, lora_wa - n_w), (0, 0))).astype(BF16)
        a2p = jnp.pad(a2[i], ((n_w, lora_wa - n_w - n_a), (0, 0))).astype(BF16)
        g2p = jnp.pad(g2[i], ((0, lora_g - n_g), (0, 0))).astype(BF16)
        rw_vecs = jnp.stack([w0[i], a0[i], k_k[i], k_a[i], r_k[i].reshape(-1), ln_x_w[i], ln_x_b[i],
                             jnp.zeros_like(w0[i])]).astype(F32)
        s5_vecs = jnp.stack([d_skip[i], b_glu[i], s5_out_g[i]]).astype(F32)
        gains = jnp.stack([norm_ffn_g[i], norm_final_g]).astype(F32)

        z = _in_proj(h, norm_mix_g[i].reshape(1, -1).astype(F32), mu, w_in[i].astype(BF16), segments)

        wb, lam_bar, wc_re, wc_im = _s5_weights(
            lam_re[i], lam_im[i], log_dt[i], b_re[i], b_im[i], c_re[i], c_im[i])
        y_s5 = _s5(z, zw // d_s5, wb, lam_bar, wc_re, wc_im, s5_vecs, w_glu[i].astype(BF16))
        y_rw = _rwkv(z, zw, rw_vecs, w2p, a2p, g2p, d_rw, lora_wa, lora_g)

        h = _out_ffn(h, y_s5, y_rw, w_out[i].astype(BF16), gains,
                     w_up[i].astype(BF16), w_down[i].astype(BF16), final_norm=(i == depth - 1))
    return h
```
